```python
import math
import jax
import jax.numpy as jnp
from jax import lax
import numpy as np

D_MODEL = 2048
BATCH = 8
SEQ = 4096
DEPTH = 4

N_MEM = 256
EPS = 1e-6
Q_BLOCK = 128
N_BRANCH = 4
BRANCH_WIDTH = D_MODEL // 4

MLA_HEADS = 4
MLA_NOPE = 128
MLA_ROPE = 64
MLA_V = 128
MLA_Q_RANK = 384
MLA_KV_RANK = 256
ROPE_THETA = 10000.0

FOX_HEADS = 4
FOX_DH = 128

GLA_HEADS = 4
GLA_DK = 64
GLA_DV = 128
GLA_GATE_RANK = 16
GLA_GATE_NORM = 16.0
GLA_CHUNK = 64

SSD_HEADS = 8
SSD_HEADDIM = 64
SSD_INNER = SSD_HEADS * SSD_HEADDIM
SSD_GROUPS = 2
SSD_STATE = 128
SSD_CONV = 4
SSD_CONV_DIM = SSD_INNER + 2 * SSD_GROUPS * SSD_STATE
SSD_CHUNK = 128

XATTN_HEADS = 4
XATTN_DH = D_MODEL // XATTN_HEADS

D_FF = 5632

IN_SIZES = (
    N_BRANCH * D_MODEL,
    MLA_Q_RANK,
    MLA_KV_RANK,
    MLA_ROPE,
    3 * FOX_HEADS * FOX_DH,
    FOX_HEADS,
    GLA_HEADS * GLA_DK,
    GLA_HEADS * GLA_DK,
    GLA_HEADS * GLA_DV,
    GLA_GATE_RANK,
    GLA_HEADS * GLA_DV,
    SSD_INNER,
    SSD_CONV_DIM,
    SSD_HEADS,
)
IN_DIM = sum(IN_SIZES)

kernel_name = "hybrid_gated_mla_fox_gla_ssd_macaron"


def _rmsnorm(t, w):
    tf = t.astype(jnp.float32)
    tf = tf * lax.rsqrt(jnp.mean(tf * tf, axis=-1, keepdims=True) + EPS)
    return (tf * w.astype(jnp.float32)).astype(t.dtype)


def _swiglu(t, w_gate, w_up, w_down):
    return (jax.nn.silu(t @ w_gate) * (t @ w_up)) @ w_down


def _split_columns(proj):
    offsets = []
    acc = 0
    for size in IN_SIZES[:-1]:
        acc += size
        offsets.append(acc)
    return jnp.split(proj, offsets, axis=-1)


def _rope(t, cos, sin):
    half = t.shape[-1] // 2
    tf = t.astype(jnp.float32)
    t1, t2 = tf[..., :half], tf[..., half:]
    return jnp.concatenate([t1 * cos - t2 * sin, t1 * sin + t2 * cos], axis=-1).astype(t.dtype)


def _causal_block_attention(q, k, v, log_f_cum=None):
    b, h, s, dk = q.shape
    nb = s // Q_BLOCK
    q_blocks = jnp.moveaxis(q.reshape(b, h, nb, Q_BLOCK, dk), 2, 0)
    c_blocks = None if log_f_cum is None else jnp.moveaxis(log_f_cum.reshape(b, h, nb, Q_BLOCK), 2, 0)
    key_pos = jnp.arange(s)

    def body(args):
        i, q_blk, c_blk = args
        logits = jnp.einsum("bhqd,bhkd->bhqk", q_blk, k, preferred_element_type=jnp.float32)
        if c_blk is not None:
            logits = logits + (c_blk[..., :, None] - log_f_cum[..., None, :])
        q_pos = i * Q_BLOCK + jnp.arange(Q_BLOCK)
        logits = jnp.where(q_pos[:, None] >= key_pos[None, :], logits, -jnp.inf)
        p = jax.nn.softmax(logits, axis=-1)
        return jnp.einsum("bhqk,bhkd->bhqd", p.astype(v.dtype), v)

    out = lax.map(body, (jnp.arange(nb), q_blocks, c_blocks))
    return jnp.moveaxis(out, 0, 2).reshape(b, h, s, v.shape[-1])


def _mla(cq, ckv, k_rope, q_norm, kv_norm, w_uq, w_ukv, cos, sin):
    b, s, _ = cq.shape
    q = (_rmsnorm(cq, q_norm) @ w_uq).reshape(b, s, MLA_HEADS, MLA_NOPE + MLA_ROPE)
    q_nope, q_pe = q[..., :MLA_NOPE], q[..., MLA_NOPE:]
    q_pe = _rope(q_pe, cos[:, :, None, :], sin[:, :, None, :])
    kv = (_rmsnorm(ckv, kv_norm) @ w_ukv).reshape(b, s, MLA_HEADS, MLA_NOPE + MLA_V)
    k_nope, v = kv[..., :MLA_NOPE], kv[..., MLA_NOPE:]
    k_pe = _rope(k_rope, cos, sin)
    k_pe = jnp.broadcast_to(k_pe[:, :, None, :], (b, s, MLA_HEADS, MLA_ROPE))
    scale = (MLA_NOPE + MLA_ROPE) ** -0.5
    qh = jnp.concatenate([q_nope, q_pe], axis=-1) * scale
    kh = jnp.concatenate([k_nope, k_pe], axis=-1)
    o = _causal_block_attention(qh.transpose(0, 2, 1, 3), kh.transpose(0, 2, 1, 3), v.transpose(0, 2, 1, 3))
    return o.transpose(0, 2, 1, 3).reshape(b, s, MLA_HEADS * MLA_V)


def _fox(qkv, f_logit, f_bias):
    b, s, _ = qkv.shape
    qkv = qkv.reshape(b, s, 3, FOX_HEADS, FOX_DH).transpose(2, 0, 3, 1, 4)
    q, k, v = qkv[0] * (FOX_DH ** -0.5), qkv[1], qkv[2]
    log_f = jax.nn.log_sigmoid(f_logit.astype(jnp.float32) + f_bias.astype(jnp.float32))
    log_f_cum = jnp.cumsum(log_f, axis=1).transpose(0, 2, 1)
    o = _causal_block_attention(q, k, v, log_f_cum)
    return o.transpose(0, 2, 1, 3).reshape(b, s, FOX_HEADS * FOX_DH)


def _gla(q, k, v, g_lr, r, w_g2, b_g2, norm_w):
    b, s, _ = q.shape
    nc = s // GLA_CHUNK
    log_g = jax.nn.log_sigmoid((g_lr @ w_g2 + b_g2).astype(jnp.float32)) / GLA_GATE_NORM

    def chunked(t, d):
        return t.astype(jnp.float32).reshape(b, nc, GLA_CHUNK, GLA_HEADS, d)

    q_c = chunked(q, GLA_DK) * (GLA_DK ** -0.5)
    k_c = chunked(k, GLA_DK)
    v_c = chunked(v, GLA_DV)
    g_cum = jnp.cumsum(chunked(log_g, GLA_DK), axis=2)
    g_end = g_cum[:, :, -1:]
    q_dec = q_c * jnp.exp(g_cum)
    k_dec = k_c * jnp.exp(-g_cum)
    causal = jnp.tril(jnp.ones((GLA_CHUNK, GLA_CHUNK), dtype=bool))
    att = jnp.where(causal, jnp.einsum("bnihd,bnjhd->bnhij", q_dec, k_dec), 0.0)
    o_intra = jnp.einsum("bnhij,bnjhv->bnihv", att, v_c)
    chunk_upd = jnp.einsum("bnjhd,bnjhv->bnhdv", k_c * jnp.exp(g_end - g_cum), v_c)
    chunk_decay = jnp.exp(g_end[:, :, 0])

    def step(state, inp):
        upd, dcy = inp
        return dcy[..., None] * state + upd, state

    init = jnp.zeros((b, GLA_HEADS, GLA_DK, GLA_DV), jnp.float32)
    _, prev = lax.scan(step, init, (jnp.moveaxis(chunk_upd, 1, 0), jnp.moveaxis(chunk_decay, 1, 0)))
    prev = jnp.moveaxis(prev, 0, 1)
    o_inter = jnp.einsum("bnihd,bnhdv->bnihv", q_dec, prev)
    o = (o_intra + o_inter).reshape(b, s, GLA_HEADS, GLA_DV)
    o = _rmsnorm(o, norm_w).reshape(b, s, GLA_HEADS * GLA_DV).astype(r.dtype)
    return o * jax.nn.silu(r)


def _ssd(z, xbc, dt_raw, conv_w, conv_b, dt_bias, a_log, d_skip, norm_w):
    b, s, _ = z.shape
    nc = s // SSD_CHUNK
    hg = SSD_HEADS // SSD_GROUPS
    xbc = lax.conv_general_dilated(
        xbc, conv_w[:, None, :], window_strides=(1,), padding=[(SSD_CONV - 1, 0)],
        dimension_numbers=("NWC", "WIO", "NWC"), feature_group_count=SSD_CONV_DIM)
    xbc = jax.nn.silu(xbc + conv_b)
    xs = xbc[..., :SSD_INNER]
    bm = xbc[..., SSD_INNER:SSD_INNER + SSD_GROUPS * SSD_STATE]
    cm = xbc[..., SSD_INNER + SSD_GROUPS * SSD_STATE:]
    dt = jax.nn.softplus(dt_raw.astype(jnp.float32) + dt_bias.astype(jnp.float32))
    a = -jnp.exp(a_log.astype(jnp.float32))
    x_c = xs.astype(jnp.float32).reshape(b, nc, SSD_CHUNK, SSD_GROUPS, hg, SSD_HEADDIM)
    b_c = bm.astype(jnp.float32).reshape(b, nc, SSD_CHUNK, SSD_GROUPS, SSD_STATE)
    c_c = cm.astype(jnp.float32).reshape(b, nc, SSD_CHUNK, SSD_GROUPS, SSD_STATE)
    dt_c = dt.reshape(b, nc, SSD_CHUNK, SSD_GROUPS, hg)
    a_cum = jnp.cumsum(dt_c * a.reshape(SSD_GROUPS, hg), axis=2)
    x_dt = x_c * dt_c[..., None]
    causal = jnp.tril(jnp.ones((SSD_CHUNK, SSD_CHUNK), dtype=bool))[:, :, None, None]
    seg = a_cum[:, :, :, None] - a_cum[:, :, None, :]
    decay_ls = jnp.exp(jnp.where(causal, seg, -jnp.inf))
    cb = jnp.einsum("bclgn,bcsgn->bclsg", c_c, b_c)
    y_diag = jnp.einsum("bclsg,bclsgh,bcsghp->bclghp", cb, decay_ls, x_dt)
    decay_to_end = jnp.exp(a_cum[:, :, -1:] - a_cum)
    chunk_states = jnp.einsum("bcsgn,bcsgh,bcsghp->bcghpn", b_c, decay_to_end, x_dt)
    chunk_decay = jnp.exp(a_cum[:, :, -1])

    def step(state, inp):
        upd, dcy = inp
        return dcy[..., None, None] * state + upd, state

    init = jnp.zeros((b, SSD_GROUPS, hg, SSD_HEADDIM, SSD_STATE), jnp.float32)
    _, prev = lax.scan(step, init, (jnp.moveaxis(chunk_states, 1, 0), jnp.moveaxis(chunk_decay, 1, 0)))
    prev = jnp.moveaxis(prev, 0, 1)
    y_off = jnp.einsum("bclgn,bcghpn,bclgh->bclghp", c_c, prev, jnp.exp(a_cum))
    y = y_diag + y_off + d_skip.astype(jnp.float32).reshape(SSD_GROUPS, hg)[:, :, None] * x_c
    y = y.reshape(b, s, SSD_INNER).astype(z.dtype) * jax.nn.silu(z)
    gsz = SSD_INNER // SSD_GROUPS
    y = _rmsnorm(y.reshape(b, s, SSD_GROUPS, gsz), norm_w.reshape(SSD_GROUPS, gsz))
    return y.reshape(b, s, SSD_INNER)


def _cross_attention(h, m, w_q, w_k, w_v, w_o):
    b, s, _ = h.shape
    n = m.shape[1]
    q = (h @ w_q).reshape(b, s, XATTN_HEADS, XATTN_DH)
    k = (m @ w_k).reshape(b, n, XATTN_HEADS, XATTN_DH)
    v = (m @ w_v).reshape(b, n, XATTN_HEADS, XATTN_DH)
    logits = jnp.einsum("bshd,bmhd->bhsm", q, k, preferred_element_type=jnp.float32) * (XATTN_DH ** -0.5)
    p = jax.nn.softmax(logits, axis=-1).astype(v.dtype)
    o = jnp.einsum("bhsm,bmhd->bshd", p, v).reshape(b, s, D_MODEL)
    return o @ w_o


def setup_inputs(seed: int = 0) -> dict:
    key = jax.random.key(seed)
    k_x, k_mem, k_pos, k_par = jax.random.split(key, 4)
    keys = iter(jax.random.split(k_par, 40))

    def dense(shape, fan_in):
        return jax.random.normal(next(keys), shape, jnp.float32) * (fan_in ** -0.5)

    def gain(shape):
        return 1.0 + 0.05 * jax.random.normal(next(keys), shape, jnp.float32)

    def small(shape, scale=0.02):
        return scale * jax.random.normal(next(keys), shape, jnp.float32)

    L = DEPTH
    x = jax.random.normal(k_x, (BATCH, SEQ, D_MODEL), jnp.float32)
    mem = jax.random.normal(k_mem, (BATCH, N_MEM, D_MODEL), jnp.float32)
    offsets = jax.random.randint(k_pos, (BATCH, 1), 0, 1024, dtype=jnp.int32)
    positions = offsets + jnp.arange(SEQ, dtype=jnp.int32)[None, :]

    ffn1_norm = gain((L, D_MODEL))
    ffn1_w_gate = dense((L, D_MODEL, D_FF), D_MODEL)
    ffn1_w_up = dense((L, D_MODEL, D_FF), D_MODEL)
    ffn1_w_down = dense((L, D_FF, D_MODEL), D_FF)
    mix_norm = gain((L, D_MODEL))
    w_in = dense((L, D_MODEL, IN_DIM), D_MODEL)
    mla_q_norm = gain((L, MLA_Q_RANK))
    mla_kv_norm = gain((L, MLA_KV_RANK))
    mla_w_uq = dense((L, MLA_Q_RANK, MLA_HEADS * (MLA_NOPE + MLA_ROPE)), MLA_Q_RANK)
    mla_w_ukv = dense((L, MLA_KV_RANK, MLA_HEADS * (MLA_NOPE + MLA_V)), MLA_KV_RANK)
    fox_f_bias = 3.0 + 0.5 * jax.random.normal(next(keys), (L, FOX_HEADS), jnp.float32)
    gla_w_g2 = dense((L, GLA_GATE_RANK, GLA_HEADS * GLA_DK), GLA_GATE_RANK)
    gla_b_g2 = small((L, GLA_HEADS * GLA_DK), 0.1)
    gla_norm = gain((L, GLA_DV))
    ssd_conv_w = dense((L, SSD_CONV, SSD_CONV_DIM), SSD_CONV)
    ssd_conv_b = small((L, SSD_CONV_DIM))
    dt0 = jnp.exp(jax.random.uniform(next(keys), (L, SSD_HEADS), jnp.float32, math.log(1e-3), math.log(1e-1)))
    ssd_dt_bias = dt0 + jnp.log(-jnp.expm1(-dt0))
    ssd_a_log = jnp.log(jax.random.uniform(next(keys), (L, SSD_HEADS), jnp.float32, 1.0, 16.0))
    ssd_d = gain((L, SSD_HEADS))
    ssd_norm = gain((L, SSD_INNER))
    w_branch = dense((L, N_BRANCH, BRANCH_WIDTH, D_MODEL), BRANCH_WIDTH)
    w_out = dense((L, D_MODEL, D_MODEL), D_MODEL)
    xattn_norm = gain((L, D_MODEL))
    mem_norm = gain((L, D_MODEL))
    xattn_w_q = dense((L, D_MODEL, D_MODEL), D_MODEL)
    xattn_w_k = dense((L, D_MODEL, D_MODEL), D_MODEL)
    xattn_w_v = dense((L, D_MODEL, D_MODEL), D_MODEL)
    xattn_w_o = dense((L, D_MODEL, D_MODEL), D_MODEL)
    ffn2_norm = gain((L, D_MODEL))
    ffn2_w_gate = dense((L, D_MODEL, D_FF), D_MODEL)
    ffn2_w_up = dense((L, D_MODEL, D_FF), D_MODEL)
    ffn2_w_down = dense((L, D_FF, D_MODEL), D_FF)
    final_norm = gain((D_MODEL,))
    return {
        "x": x, "mem": mem, "positions": positions,
        "ffn1_norm": ffn1_norm, "ffn1_w_gate": ffn1_w_gate, "ffn1_w_up": ffn1_w_up, "ffn1_w_down": ffn1_w_down,
        "mix_norm": mix_norm, "w_in": w_in,
        "mla_q_norm": mla_q_norm, "mla_kv_norm": mla_kv_norm, "mla_w_uq": mla_w_uq, "mla_w_ukv": mla_w_ukv,
        "fox_f_bias": fox_f_bias,
        "gla_w_g2": gla_w_g2, "gla_b_g2": gla_b_g2, "gla_norm": gla_norm,
        "ssd_conv_w": ssd_conv_w, "ssd_conv_b": ssd_conv_b, "ssd_dt_bias": ssd_dt_bias,
        "ssd_a_log": ssd_a_log, "ssd_d": ssd_d, "ssd_norm": ssd_norm,
        "w_branch": w_branch, "w_out": w_out,
        "xattn_norm": xattn_norm, "mem_norm": mem_norm, "xattn_w_q": xattn_w_q, "xattn_w_k": xattn_w_k,
        "xattn_w_v": xattn_w_v, "xattn_w_o": xattn_w_o,
        "ffn2_norm": ffn2_norm, "ffn2_w_gate": ffn2_w_gate, "ffn2_w_up": ffn2_w_up, "ffn2_w_down": ffn2_w_down,
        "final_norm": final_norm,
    }


def reference(x, mem, positions,
              ffn1_norm, ffn1_w_gate, ffn1_w_up, ffn1_w_down,
              mix_norm, w_in,
              mla_q_norm, mla_kv_norm, mla_w_uq, mla_w_ukv,
              fox_f_bias,
              gla_w_g2, gla_b_g2, gla_norm,
              ssd_conv_w, ssd_conv_b, ssd_dt_bias, ssd_a_log, ssd_d, ssd_norm,
              w_branch, w_out,
              xattn_norm, mem_norm, xattn_w_q, xattn_w_k, xattn_w_v, xattn_w_o,
              ffn2_norm, ffn2_w_gate, ffn2_w_up, ffn2_w_down,
              final_norm):
    b, s, _ = x.shape
    half = MLA_ROPE // 2
    inv_freq = ROPE_THETA ** (-jnp.arange(half, dtype=jnp.float32) / half)
    ang = positions.astype(jnp.float32)[..., None] * inv_freq
    cos, sin = jnp.cos(ang), jnp.sin(ang)

    for l in range(DEPTH):
        x = x + 0.5 * _swiglu(_rmsnorm(x, ffn1_norm[l]), ffn1_w_gate[l], ffn1_w_up[l], ffn1_w_down[l])

        h = _rmsnorm(x, mix_norm[l])
        (gate_cols, mla_cq, mla_ckv, mla_kr, fox_qkv, fox_f,
         gla_q, gla_k, gla_v, gla_glr, gla_r, ssd_z, ssd_xbc, ssd_dt) = _split_columns(h @ w_in[l])
        y_mla = _mla(mla_cq, mla_ckv, mla_kr, mla_q_norm[l], mla_kv_norm[l], mla_w_uq[l], mla_w_ukv[l], cos, sin)
        y_fox = _fox(fox_qkv, fox_f, fox_f_bias[l])
        y_gla = _gla(gla_q, gla_k, gla_v, gla_glr, gla_r, gla_w_g2[l], gla_b_g2[l], gla_norm[l])
        y_ssd = _ssd(ssd_z, ssd_xbc, ssd_dt, ssd_conv_w[l], ssd_conv_b[l], ssd_dt_bias[l],
                     ssd_a_log[l], ssd_d[l], ssd_norm[l])
        gates = jax.nn.sigmoid(gate_cols).reshape(b, s, N_BRANCH, D_MODEL)
        branch_outputs = (y_mla, y_fox, y_gla, y_ssd)
        merged = gates[:, :, 0] * (branch_outputs[0] @ w_branch[l, 0])
        for i in range(1, N_BRANCH):
            merged = merged + gates[:, :, i] * (branch_outputs[i] @ w_branch[l, i])
        x = x + merged @ w_out[l]

        x = x + _cross_attention(_rmsnorm(x, xattn_norm[l]), _rmsnorm(mem, mem_norm[l]),
                                 xattn_w_q[l], xattn_w_k[l], xattn_w_v[l], xattn_w_o[l])

        x = x + 0.5 * _swiglu(_rmsnorm(x, ffn2_norm[l]), ffn2_w_gate[l], ffn2_w_up[l], ffn2_w_down[l])

    return _rmsnorm(x, final_norm)
```

```python
import functools

import numpy as np
import jax
import jax.numpy as jnp
from jax import lax
from jax.experimental import pallas as pl
from jax.experimental.pallas import tpu as pltpu

D_MODEL = 2048
DEPTH = 4
EPS = 1e-6
N_BRANCH = 4
BRANCH_WIDTH = 512
MLA_HEADS, MLA_NOPE, MLA_ROPE, MLA_V = 4, 128, 64, 128
MLA_Q_RANK, MLA_KV_RANK = 384, 256
MLA_DK_PAD = 256
ROPE_THETA = 10000.0
FOX_HEADS, FOX_DH = 4, 128
GLA_HEADS, GLA_DK, GLA_DV = 4, 64, 128
GLA_GATE_RANK, GLA_GATE_NORM, GLA_CHUNK = 16, 16.0, 64
SSD_HEADS, SSD_HEADDIM, SSD_GROUPS, SSD_STATE, SSD_CONV, SSD_CHUNK = 8, 64, 2, 128, 4, 128
SSD_INNER = SSD_HEADS * SSD_HEADDIM
XATTN_HEADS = 4
XATTN_DH = D_MODEL // XATTN_HEADS
D_FF = 5632

LANES = 128
V7X_VMEM_LIMIT_BYTES = 60 * 1024 * 1024

MXU_DTYPE = jnp.bfloat16
F32 = jnp.float32

P_CQ, P_CKV, P_KR, P_KRROT = 0, 512, 768, 896
P_FOX_Q, P_FOX_K, P_FOX_V = 1024, 1536, 2048
P_GLA_Q, P_GLA_K, P_GLA_V, P_GLA_R = 2560, 2816, 3072, 3584
P_SSD_Z, P_SSD_X, P_SSD_B, P_SSD_C = 4096, 4608, 5120, 5376
P_WIDTH = 5632
PS_GLR, PS_DT, PS_F = 0, 16, 24
PST_F, PST_DT, PST_ROWS = 0, 8, 16


def _cparams(sem, vmem_bytes):
    return pltpu.CompilerParams(dimension_semantics=sem,
                                vmem_limit_bytes=int(min(vmem_bytes, V7X_VMEM_LIMIT_BYTES)))


def _dot(a, b):
    return jnp.dot(a, b, preferred_element_type=F32)


def _dot_nt(a, b):
    return lax.dot_general(a, b, (((1,), (1,)), ((), ())), preferred_element_type=F32)


def _dot_tn(a, b):
    return lax.dot_general(a, b, (((0,), (0,)), ((), ())), preferred_element_type=F32)


def _split3(x):
    hi = x.astype(MXU_DTYPE)
    r1 = x - hi.astype(F32)
    mid = r1.astype(MXU_DTYPE)
    lo = (r1 - mid.astype(F32)).astype(MXU_DTYPE)
    return hi, mid, lo


def _dot01_right(x, m01):
    if MXU_DTYPE == F32:
        return _dot(x, m01.astype(F32))
    m = m01.astype(MXU_DTYPE)
    hi, mid, lo = _split3(x)
    return _dot(hi, m) + _dot(mid, m) + _dot(lo, m)


def _dot01_left(m01, x):
    if MXU_DTYPE == F32:
        return _dot(m01.astype(F32), x)
    m = m01.astype(MXU_DTYPE)
    hi, mid, lo = _split3(x)
    return _dot(m, hi) + _dot(m, mid) + _dot(m, lo)


def _rmsnorm(x, g):
    ms = jnp.mean(x * x, axis=-1, keepdims=True)
    return x * lax.rsqrt(ms + EPS) * g


def _log_sigmoid(x):
    return jnp.minimum(x, 0.0) - jnp.log1p(jnp.exp(-jnp.abs(x)))


def _softplus(x):
    return jnp.maximum(x, 0.0) + jnp.log1p(jnp.exp(-jnp.abs(x)))


def _silu(x):
    return x * jax.nn.sigmoid(x)


def _chunk_masks(n, chunk):
    shift = chunk.bit_length() - 1
    assert 1 << shift == chunk
    r = lax.broadcasted_iota(jnp.int32, (n, n), 0)
    c = lax.broadcasted_iota(jnp.int32, (n, n), 1)
    same = jnp.right_shift(r, shift) == jnp.right_shift(c, shift)
    return jnp.logical_and(same, r >= c), jnp.logical_and(same, r <= c), same


def _ffn_kernel(x_ref, g_ref, wg_ref, wu_ref, wd_ref, fg_ref, o_ref, h_ref, *, final_norm):
    j = pl.program_id(1)

    @pl.when(j == 0)
    def _():
        x = x_ref[...]
        h_ref[...] = _rmsnorm(x, g_ref[...]).astype(h_ref.dtype)
        o_ref[...] = x

    h = h_ref[...]
    a = _dot(h, wg_ref[...])
    u = _dot(h, wu_ref[...])
    t = (_silu(a) * (0.5 * u)).astype(MXU_DTYPE)
    o_ref[...] += _dot(t, wd_ref[...])

    if final_norm:
        @pl.when(j == pl.num_programs(1) - 1)
        def _():
            o_ref[...] = _rmsnorm(o_ref[...], fg_ref[...])


def _ffn(x, g, wg, wu, wd, fg, *, final_norm, tm, tf):
    m, d = x.shape
    f = wg.shape[1]
    wb = jnp.dtype(wg.dtype).itemsize
    vmem = 4 * tm * d * 4 + tm * d * wb + 2 * 3 * d * tf * wb + 4 * tm * tf * 4 + (4 << 20)
    return pl.pallas_call(
        functools.partial(_ffn_kernel, final_norm=final_norm),
        out_shape=jax.ShapeDtypeStruct((m, d), F32),
        grid=(m // tm, f // tf),
        in_specs=[
            pl.BlockSpec((tm, d), lambda i, j: (i, 0)),
            pl.BlockSpec((1, d), lambda i, j: (0, 0)),
            pl.BlockSpec((d, tf), lambda i, j: (0, j)),
            pl.BlockSpec((d, tf), lambda i, j: (0, j)),
            pl.BlockSpec((tf, d), lambda i, j: (j, 0)),
            pl.BlockSpec((1, d), lambda i, j: (0, 0)),
        ],
        out_specs=pl.BlockSpec((tm, d), lambda i, j: (i, 0)),
        scratch_shapes=[pltpu.VMEM((tm, d), wg.dtype)],
        compiler_params=_cparams(("parallel", "arbitrary"), vmem),
        name="ffn",
    )(x, g, wg, wu, wd, fg)


def _proj_kernel(x_ref, g_ref, w_ref, cs_ref, ws_ref, wst_ref, p_ref, ps_ref, pst_ref, h_ref):
    j = pl.program_id(1)

    @pl.when(j == 0)
    def _():
        h = _rmsnorm(x_ref[...], g_ref[...]).astype(h_ref.dtype)
        h_ref[...] = h
        ps_ref[...] = _dot(h, ws_ref[...])
        pst_ref[...] = _dot_nt(wst_ref[...], h)

    p_ref[...] = (_dot(h_ref[...], w_ref[...]) * cs_ref[...]).astype(p_ref.dtype)


def _proj(x, g, w, cs, ws, wst, *, tm, tn):
    m, d = x.shape
    n = w.shape[1]
    wb = jnp.dtype(w.dtype).itemsize
    vmem = 2 * tm * d * 4 + tm * d * wb + 2 * d * tn * wb + 3 * tm * tn * 4 + (6 << 20)
    return pl.pallas_call(
        _proj_kernel,
        out_shape=(jax.ShapeDtypeStruct((m, n), w.dtype),
                   jax.ShapeDtypeStruct((m, LANES), F32),
                   jax.ShapeDtypeStruct((PST_ROWS, m), F32)),
        grid=(m // tm, n // tn),
        in_specs=[
            pl.BlockSpec((tm, d), lambda i, j: (i, 0)),
            pl.BlockSpec((1, d), lambda i, j: (0, 0)),
            pl.BlockSpec((d, tn), lambda i, j: (0, j)),
            pl.BlockSpec((1, tn), lambda i, j: (0, j)),
            pl.BlockSpec((d, LANES), lambda i, j: (0, 0)),
            pl.BlockSpec((PST_ROWS, d), lambda i, j: (0, 0)),
        ],
        out_specs=(pl.BlockSpec((tm, tn), lambda i, j: (i, j)),
                   pl.BlockSpec((tm, LANES), lambda i, j: (i, 0)),
                   pl.BlockSpec((PST_ROWS, tm), lambda i, j: (0, i))),
        scratch_shapes=[pltpu.VMEM((tm, d), w.dtype)],
        compiler_params=_cparams(("parallel", "arbitrary"), vmem),
        name="proj",
    )(x, g, w, cs, ws, wst)


def _norm_matmul_kernel(x_ref, g_ref, w_ref, o_ref):
    h = _rmsnorm(x_ref[...], g_ref[...]).astype(w_ref.dtype)
    o_ref[...] = _dot(h, w_ref[...]).astype(o_ref.dtype)


def _norm_matmul(x, g, w, *, tm, tn):
    m, d = x.shape
    n = w.shape[1]
    wb = jnp.dtype(w.dtype).itemsize
    vmem = 3 * tm * d * 4 + 2 * d * tn * wb + 3 * tm * tn * 4 + (4 << 20)
    return pl.pallas_call(
        _norm_matmul_kernel,
        out_shape=jax.ShapeDtypeStruct((m, n), w.dtype),
        grid=(m // tm, n // tn),
        in_specs=[pl.BlockSpec((tm, d), lambda i, j: (i, 0)),
                  pl.BlockSpec((1, d), lambda i, j: (0, 0)),
                  pl.BlockSpec((d, tn), lambda i, j: (0, j))],
        out_specs=pl.BlockSpec((tm, tn), lambda i, j: (i, j)),
        compiler_params=_cparams(("parallel", "parallel"), vmem),
        name="mem_kv",
    )(x, g, w)


def _rope_kernel(pos_ref, ct_ref, st_ref):
    pos = pos_ref[...].astype(F32)
    lane = lax.broadcasted_iota(jnp.int32, (1, LANES), 1)
    half = MLA_ROPE // 2
    f = jnp.bitwise_and(lane, half - 1).astype(F32)
    inv_freq = jnp.power(jnp.float32(ROPE_THETA), -f / half)
    ang = pos * inv_freq
    valid = lane < MLA_ROPE
    ct_ref[...] = jnp.where(valid, jnp.cos(ang), 0.0)
    st_ref[...] = jnp.where(valid, jnp.sin(ang), 0.0)


def _rope_tables(pos_col, *, tm):
    m = pos_col.shape[0]
    return pl.pallas_call(
        _rope_kernel,
        out_shape=(jax.ShapeDtypeStruct((m, LANES), F32), jax.ShapeDtypeStruct((m, LANES), F32)),
        grid=(m // tm,),
        in_specs=[pl.BlockSpec((tm, 1), lambda i: (i, 0))],
        out_specs=(pl.BlockSpec((tm, LANES), lambda i: (i, 0)),
                   pl.BlockSpec((tm, LANES), lambda i: (i, 0))),
        compiler_params=_cparams(("parallel",), 16 << 20),
        name="rope_tables",
    )(pos_col)


def _mla_prep_kernel(cq_ref, ckv_ref, kr_ref, krrot_ref, ct_ref, st_ref, qn_ref, kvn_ref,
                     wa_ref, wb_ref, wkv_ref, q_ref, k_ref, v_ref):
    ct = ct_ref[...]
    st = st_ref[...]
    scale = (MLA_NOPE + MLA_ROPE) ** -0.5

    ckv = ckv_ref[...].astype(F32)
    ckvn = _rmsnorm(ckv, kvn_ref[...]).astype(MXU_DTYPE)
    kv = _dot(ckvn, wkv_ref[...])
    kpe = kr_ref[...].astype(F32) * ct + krrot_ref[...].astype(F32) * st
    kpe = kpe.astype(k_ref.dtype)
    hw = MLA_HEADS * MLA_NOPE
    for h in range(MLA_HEADS):
        base = h * MLA_DK_PAD
        k_ref[:, base:base + MLA_NOPE] = kv[:, h * MLA_NOPE:(h + 1) * MLA_NOPE].astype(k_ref.dtype)
        k_ref[:, base + MLA_NOPE:base + MLA_DK_PAD] = kpe
    v_ref[...] = kv[:, hw:].astype(v_ref.dtype)

    cq = cq_ref[...].astype(F32)[:, :MLA_Q_RANK]
    cqn = _rmsnorm(cq, qn_ref[...]).astype(MXU_DTYPE)
    qa = _dot(cqn, wa_ref[...])
    qb = _dot(cqn, wb_ref[...])
    for h in range(MLA_HEADS):
        base = h * MLA_DK_PAD
        q_ref[:, base:base + MLA_NOPE] = (qa[:, base:base + MLA_NOPE] * scale).astype(q_ref.dtype)
        pe = qa[:, base + MLA_NOPE:base + MLA_DK_PAD] * ct + qb[:, h * LANES:(h + 1) * LANES] * st
        q_ref[:, base + MLA_NOPE:base + MLA_DK_PAD] = (pe * scale).astype(q_ref.dtype)


def _mla_prep(p, ct, st, qn, kvn, wa, wb, wkv, *, tm):
    m = p.shape[0]
    qk_w = MLA_HEADS * MLA_DK_PAD
    v_w = MLA_HEADS * MLA_V
    full = lambda a: pl.BlockSpec(a.shape, lambda i: (0,) * a.ndim)
    return pl.pallas_call(
        _mla_prep_kernel,
        out_shape=(jax.ShapeDtypeStruct((m, qk_w), p.dtype),
                   jax.ShapeDtypeStruct((m, qk_w), p.dtype),
                   jax.ShapeDtypeStruct((m, v_w), p.dtype)),
        grid=(m // tm,),
        in_specs=[
            pl.BlockSpec((tm, 512), lambda i: (i, P_CQ // 512)),
            pl.BlockSpec((tm, 256), lambda i: (i, P_CKV // 256)),
            pl.BlockSpec((tm, 128), lambda i: (i, P_KR // 128)),
            pl.BlockSpec((tm, 128), lambda i: (i, P_KRROT // 128)),
            pl.BlockSpec((tm, LANES), lambda i: (i, 0)),
            pl.BlockSpec((tm, LANES), lambda i: (i, 0)),
            full(qn), full(kvn), full(wa), full(wb), full(wkv),
        ],
        out_specs=(pl.BlockSpec((tm, qk_w), lambda i: (i, 0)),
                   pl.BlockSpec((tm, qk_w), lambda i: (i, 0)),
                   pl.BlockSpec((tm, v_w), lambda i: (i, 0))),
        compiler_params=_cparams(("parallel",), 40 << 20),
        name="mla_prep",
    )(p, p, p, p, ct, st, qn, kvn, wa, wb, wkv)


def _fox_cum_kernel(ft_ref, fb_ref, c_ref, *, seq):
    lf = _log_sigmoid(ft_ref[...] + fb_ref[...])
    r = lax.broadcasted_iota(jnp.int32, (LANES, LANES), 0)
    c = lax.broadcasted_iota(jnp.int32, (LANES, LANES), 1)
    upper = r <= c
    carry = jnp.zeros((8, 1), F32)
    for blk in range(seq // LANES):
        sl = slice(blk * LANES, (blk + 1) * LANES)
        cs = _dot01_right(lf[:, sl], upper) + carry
        for h in range(FOX_HEADS):
            c_ref[h, :, sl] = cs[h:h + 1, :]
        carry = cs[:, LANES - 1:LANES]


def _fox_cum(pst, fb_col, *, batch, seq):
    return pl.pallas_call(
        functools.partial(_fox_cum_kernel, seq=seq),
        out_shape=jax.ShapeDtypeStruct((batch * FOX_HEADS, 1, seq), F32),
        grid=(batch,),
        in_specs=[pl.BlockSpec((8, seq), lambda b: (PST_F // 8, b)),
                  pl.BlockSpec((8, 1), lambda b: (0, 0))],
        out_specs=pl.BlockSpec((FOX_HEADS, 1, seq), lambda b: (b, 0, 0)),
        compiler_params=_cparams(("parallel",), 16 << 20),
        name="fox_cum",
    )(pst, fb_col)


def _flash_kernel(*refs, seq, tq, has_bias):
    if has_bias:
        q_ref, k_ref, v_ref, c_ref, o_ref = refs
    else:
        q_ref, k_ref, v_ref, o_ref = refs
        c_ref = None
    dv = v_ref.shape[-1]
    row = lax.broadcasted_iota(jnp.int32, (tq, tq), 0)
    col = lax.broadcasted_iota(jnp.int32, (tq, tq), 1)
    causal = row >= col

    for qi in range(seq // tq):
        q = q_ref[qi * tq:(qi + 1) * tq, :]
        c0 = c_ref[0, :, qi * tq:qi * tq + 1] if has_bias else None

        def scores(start):
            s = _dot_nt(q, k_ref[pl.ds(start, tq), :])
            if has_bias:
                s = s + (c0 - c_ref[0, :, pl.ds(start, tq)])
            return s

        def update(carry, s, start):
            m, l, acc = carry
            m_new = jnp.maximum(m, jnp.max(s, axis=-1, keepdims=True))
            alpha = jnp.exp(m - m_new)
            p = jnp.exp(s - m_new)
            l = alpha * l + jnp.sum(p, axis=-1, keepdims=True)
            acc = alpha * acc + _dot(p.astype(MXU_DTYPE), v_ref[pl.ds(start, tq), :])
            return m_new, l, acc

        def body(ki, carry):
            start = pl.multiple_of(ki * tq, tq)
            return update(carry, scores(start), start)

        init = (jnp.full((tq, 1), -jnp.inf, F32), jnp.zeros((tq, 1), F32), jnp.zeros((tq, dv), F32))
        carry = lax.fori_loop(0, qi, body, init)
        s = jnp.where(causal, scores(qi * tq), -jnp.inf)
        _, l, acc = update(carry, s, qi * tq)
        o_ref[qi * tq:(qi + 1) * tq, :] = (acc / l).astype(o_ref.dtype)


def _flash(q_arr, k_arr, v_arr, c_arr, *, batch, seq, heads, dk, dv, q_off, k_off, v_off, tq):
    has_bias = c_arr is not None
    in_specs = [pl.BlockSpec((seq, dk), lambda b, h: (b, q_off // dk + h)),
                pl.BlockSpec((seq, dk), lambda b, h: (b, k_off // dk + h)),
                pl.BlockSpec((seq, dv), lambda b, h: (b, v_off // dv + h))]
    args = [q_arr, k_arr, v_arr]
    if has_bias:
        in_specs.append(pl.BlockSpec((1, 1, seq), lambda b, h: (b * heads + h, 0, 0)))
        args.append(c_arr)
    vmem = 2 * seq * (2 * dk + 2 * dv) * 2 + 6 * tq * tq * 4 + (8 << 20)
    return pl.pallas_call(
        functools.partial(_flash_kernel, seq=seq, tq=tq, has_bias=has_bias),
        out_shape=jax.ShapeDtypeStruct((batch * seq, heads * dv), q_arr.dtype),
        grid=(batch, heads),
        in_specs=in_specs,
        out_specs=pl.BlockSpec((seq, dv), lambda b, h: (b, h)),
        compiler_params=_cparams(("parallel", "parallel"), vmem),
        name="flash_fox" if has_bias else "flash_mla",
    )(*args)


def _gla_kernel(q_ref, k_ref, v_ref, r_ref, ps_ref, wg2_ref, bg2_ref, nw_ref, o_ref, st_ref, *, tm):
    @pl.when(pl.program_id(1) == 0)
    def _():
        st_ref[...] = jnp.zeros_like(st_ref)

    glin = _dot(ps_ref[...].astype(MXU_DTYPE), wg2_ref[...]) + bg2_ref[...]
    gl = _log_sigmoid(glin) / GLA_GATE_NORM
    tril, _, same = _chunk_masks(tm, GLA_CHUNK)
    gcum = _dot01_left(tril, gl)
    gend = _dot01_left(same, gl)
    q = q_ref[...].astype(F32)
    k = k_ref[...].astype(F32)
    qd = (q * jnp.exp(gcum)).astype(MXU_DTYPE)
    kd = (k * jnp.exp(-gcum)).astype(MXU_DTYPE)
    ku = (k * jnp.exp(gend - gcum)).astype(MXU_DTYPE)
    dec = jnp.exp(gend)
    cr = lax.broadcasted_iota(jnp.int32, (GLA_CHUNK, GLA_CHUNK), 0)
    cc = lax.broadcasted_iota(jnp.int32, (GLA_CHUNK, GLA_CHUNK), 1)
    causal = cr >= cc
    nw = nw_ref[...]

    for c in range(tm // GLA_CHUNK):
        rows = slice(c * GLA_CHUNK, (c + 1) * GLA_CHUNK)
        for h in range(GLA_HEADS):
            kc = slice(h * GLA_DK, (h + 1) * GLA_DK)
            vc = slice(h * GLA_DV, (h + 1) * GLA_DV)
            qd_h, kd_h, ku_h = qd[rows, kc], kd[rows, kc], ku[rows, kc]
            v_h = v_ref[rows, vc]
            att = jnp.where(causal, _dot_nt(qd_h, kd_h), 0.0)
            state_t = st_ref[:, kc]
            o = _dot(att.astype(MXU_DTYPE), v_h) + _dot_nt(qd_h, state_t.astype(MXU_DTYPE))
            st_ref[:, kc] = state_t * dec[c * GLA_CHUNK:c * GLA_CHUNK + 1, kc] + _dot_tn(v_h, ku_h)
            o = _rmsnorm(o, nw)
            o_ref[rows, vc] = (o * _silu(r_ref[rows, vc].astype(F32))).astype(o_ref.dtype)


def _gla(p, ps, wg2, bg2, nw, *, batch, seq, tm):
    m = p.shape[0]
    nt = seq // tm
    full = lambda a: pl.BlockSpec(a.shape, lambda b, i: (0,) * a.ndim)
    row = lambda b, i: b * nt + i
    return pl.pallas_call(
        functools.partial(_gla_kernel, tm=tm),
        out_shape=jax.ShapeDtypeStruct((m, GLA_HEADS * GLA_DV), p.dtype),
        grid=(batch, nt),
        in_specs=[
            pl.BlockSpec((tm, 256), lambda b, i: (row(b, i), P_GLA_Q // 256)),
            pl.BlockSpec((tm, 256), lambda b, i: (row(b, i), P_GLA_K // 256)),
            pl.BlockSpec((tm, 512), lambda b, i: (row(b, i), P_GLA_V // 512)),
            pl.BlockSpec((tm, 512), lambda b, i: (row(b, i), P_GLA_R // 512)),
            pl.BlockSpec((tm, LANES), lambda b, i: (row(b, i), 0)),
            full(wg2), full(bg2), full(nw),
        ],
        out_specs=pl.BlockSpec((tm, GLA_HEADS * GLA_DV), lambda b, i: (row(b, i), 0)),
        scratch_shapes=[pltpu.VMEM((GLA_DV, GLA_HEADS * GLA_DK), F32)],
        compiler_params=_cparams(("parallel", "arbitrary"), 32 << 20),
        name="gla",
    )(p, p, p, p, ps, wg2, bg2, nw)


def _ssd_kernel(z_ref, x_ref, b_ref, c_ref, ps_ref, pst_ref, cw_ref, cb_ref, dtb_row_ref,
                dtb_col_ref, alog_row_ref, alog_col_ref, dskip_ref, nw_ref, exp_ref,
                o_ref, st_ref, tail_ref, *, tm):
    @pl.when(pl.program_id(1) == 0)
    def _():
        st_ref[...] = jnp.zeros_like(st_ref)
        tail_ref[...] = jnp.zeros_like(tail_ref)

    inner = SSD_INNER
    gw = inner // SSD_GROUPS
    xbc = jnp.concatenate([x_ref[...], b_ref[...], c_ref[...]], axis=-1).astype(F32)
    ext = jnp.concatenate([tail_ref[...], xbc], axis=0)
    tail_ref[...] = xbc[tm - 8:, :]
    conv = cb_ref[...]
    for j in range(SSD_CONV):
        shift = SSD_CONV - 1 - j
        src = ext if shift == 0 else pltpu.roll(ext, shift, axis=0)
        conv = conv + cw_ref[j:j + 1, :] * src[8:, :]
    xc = _silu(conv)
    xs = xc[:, :inner]
    bm = xc[:, inner:inner + SSD_GROUPS * SSD_STATE].astype(MXU_DTYPE)
    cm = xc[:, inner + SSD_GROUPS * SSD_STATE:].astype(MXU_DTYPE)

    dt_exp = _dot01_right(_softplus(ps_ref[...] + dtb_row_ref[...]), exp_ref[...])
    a_row = -jnp.exp(alog_row_ref[...])
    dta = dt_exp * a_row
    tril, triu, same = _chunk_masks(tm, SSD_CHUNK)
    acum = _dot01_left(tril, dta)
    aend = _dot01_left(same, dta)
    dta_t = _softplus(pst_ref[...] + dtb_col_ref[...]) * (-jnp.exp(alog_col_ref[...]))
    acum_t = _dot01_right(dta_t, triu)
    x_dt = xs * dt_exp
    xw = (x_dt * jnp.exp(aend - acum)).astype(MXU_DTYPE)
    x_dt_m = x_dt.astype(MXU_DTYPE)
    e_acum = jnp.exp(acum)
    e_aend = jnp.exp(aend)
    cr = lax.broadcasted_iota(jnp.int32, (SSD_CHUNK, SSD_CHUNK), 0)
    cc = lax.broadcasted_iota(jnp.int32, (SSD_CHUNK, SSD_CHUNK), 1)
    causal = cr >= cc
    dskip = dskip_ref[...]
    nw = nw_ref[...]
    hg = SSD_HEADS // SSD_GROUPS

    for c in range(tm // SSD_CHUNK):
        r0 = c * SSD_CHUNK
        rows = slice(r0, r0 + SSD_CHUNK)
        for g in range(SSD_GROUPS):
            gl = slice(g * gw, (g + 1) * gw)
            nl = slice(g * SSD_STATE, (g + 1) * SSD_STATE)
            b_g, c_g = bm[rows, nl], cm[rows, nl]
            cb = _dot_nt(c_g, b_g)
            parts = []
            for hh in range(hg):
                h = g * hg + hh
                lane0 = h * SSD_HEADDIM
                colv = jnp.broadcast_to(acum[rows, lane0:lane0 + 1], (SSD_CHUNK, SSD_CHUNK))
                rowv = acum_t[PST_DT + h:PST_DT + h + 1, r0:r0 + SSD_CHUNK]
                decay = jnp.where(causal, jnp.exp(colv - rowv), 0.0)
                mmat = (cb * decay).astype(MXU_DTYPE)
                parts.append(_dot(mmat, x_dt_m[rows, lane0:lane0 + SSD_HEADDIM]))
            y_diag = jnp.concatenate(parts, axis=-1)
            state_t = st_ref[:, gl]
            y_off = _dot(c_g, state_t.astype(MXU_DTYPE)) * e_acum[rows, gl]
            st_ref[:, gl] = state_t * e_aend[r0:r0 + 1, gl] + _dot_tn(b_g, xw[rows, gl])
            y = y_diag + y_off + dskip[:, gl] * xs[rows, gl]
            y = y * _silu(z_ref[rows, gl].astype(F32))
            o_ref[rows, gl] = _rmsnorm(y, nw[:, gl]).astype(o_ref.dtype)


def _ssd(p, ps, pst, cw, cb, dtb_row, dtb_col, alog_row, alog_col, dskip, nw, expand, *, batch, seq, tm):
    m = p.shape[0]
    nt = seq // tm
    full = lambda a: pl.BlockSpec(a.shape, lambda b, i: (0,) * a.ndim)
    row = lambda b, i: b * nt + i
    conv_dim = SSD_INNER + 2 * SSD_GROUPS * SSD_STATE
    return pl.pallas_call(
        functools.partial(_ssd_kernel, tm=tm),
        out_shape=jax.ShapeDtypeStruct((m, SSD_INNER), p.dtype),
        grid=(batch, nt),
        in_specs=[
            pl.BlockSpec((tm, 512), lambda b, i: (row(b, i), P_SSD_Z // 512)),
            pl.BlockSpec((tm, 512), lambda b, i: (row(b, i), P_SSD_X // 512)),
            pl.BlockSpec((tm, 256), lambda b, i: (row(b, i), P_SSD_B // 256)),
            pl.BlockSpec((tm, 256), lambda b, i: (row(b, i), P_SSD_C // 256)),
            pl.BlockSpec((tm, LANES), lambda b, i: (row(b, i), 0)),
            pl.BlockSpec((PST_ROWS, tm), lambda b, i: (0, row(b, i))),
            full(cw), full(cb), full(dtb_row), full(dtb_col), full(alog_row), full(alog_col),
            full(dskip), full(nw), full(expand),
        ],
        out_specs=pl.BlockSpec((tm, SSD_INNER), lambda b, i: (row(b, i), 0)),
        scratch_shapes=[pltpu.VMEM((SSD_STATE, SSD_INNER), F32),
                        pltpu.VMEM((8, conv_dim), F32)],
        compiler_params=_cparams(("parallel", "arbitrary"), 40 << 20),
        name="ssd",
    )(p, p, p, p, ps, pst, cw, cb, dtb_row, dtb_col, alog_row, alog_col, dskip, nw, expand)


def _merge_kernel(x_ref, g_ref, y_ref, wsq_ref, wbr_ref, o_ref, h_ref, acc_ref, *, tc):
    j = pl.program_id(1)
    d = x_ref.shape[-1]

    @pl.when(j == 0)
    def _():
        h_ref[...] = _rmsnorm(x_ref[...], g_ref[...]).astype(h_ref.dtype)
        acc_ref[...] = jnp.zeros_like(acc_ref)

    @pl.when(j < N_BRANCH)
    def _():
        h = h_ref[...]
        y = y_ref[0]
        for c in range(d // tc):
            cols = slice(c * tc, (c + 1) * tc)
            gate = jax.nn.sigmoid(_dot(h, wsq_ref[0, :, cols]))
            acc_ref[:, cols] += gate * _dot(y, wbr_ref[0, :, cols])

    @pl.when(j == N_BRANCH)
    def _():
        merged = acc_ref[...].astype(MXU_DTYPE)
        o_ref[...] = x_ref[...] + _dot(merged, wsq_ref[0])


def _merge(x, g, ys, wsq, wbr, *, tm, tc):
    m, d = x.shape
    bw = ys.shape[-1]
    wbytes = jnp.dtype(wsq.dtype).itemsize
    vmem = 4 * tm * d * 4 + tm * d * wbytes + tm * d * 4 + 2 * d * d * wbytes \
        + 2 * bw * d * wbytes + 2 * tm * bw * wbytes + 4 * tm * tc * 4 + (4 << 20)
    last = N_BRANCH - 1
    return pl.pallas_call(
        functools.partial(_merge_kernel, tc=tc),
        out_shape=jax.ShapeDtypeStruct((m, d), F32),
        grid=(m // tm, N_BRANCH + 1),
        in_specs=[
            pl.BlockSpec((tm, d), lambda i, j: (i, 0)),
            pl.BlockSpec((1, d), lambda i, j: (0, 0)),
            pl.BlockSpec((1, tm, bw), lambda i, j: (jnp.minimum(j, last), i, 0)),
            pl.BlockSpec((1, d, d), lambda i, j: (j, 0, 0)),
            pl.BlockSpec((1, bw, d), lambda i, j: (jnp.minimum(j, last), 0, 0)),
        ],
        out_specs=pl.BlockSpec((tm, d), lambda i, j: (i, 0)),
        scratch_shapes=[pltpu.VMEM((tm, d), wsq.dtype), pltpu.VMEM((tm, d), F32)],
        compiler_params=_cparams(("parallel", "arbitrary"), vmem),
        name="merge",
    )(x, g, ys, wsq, wbr)


def _xattn_kernel(x_ref, g_ref, wq_ref, k_ref, v_ref, wo_ref, o_ref, h_ref):
    j = pl.program_id(1)

    @pl.when(j == 0)
    def _():
        x = x_ref[...]
        h_ref[...] = _rmsnorm(x, g_ref[...]).astype(h_ref.dtype)
        o_ref[...] = x

    q = _dot(h_ref[...], wq_ref[...]).astype(MXU_DTYPE)
    s = _dot_nt(q, k_ref[...]) * (XATTN_DH ** -0.5)
    s = s - jnp.max(s, axis=-1, keepdims=True)
    p = jnp.exp(s)
    p = p / jnp.sum(p, axis=-1, keepdims=True)
    o = _dot(p.astype(MXU_DTYPE), v_ref[...]).astype(MXU_DTYPE)
    o_ref[...] += _dot(o, wo_ref[...])


def _xattn(x, g, wq, kv, wo, *, seq, n_mem, tm):
    m, d = x.shape
    dh = XATTN_DH
    wbytes = jnp.dtype(wq.dtype).itemsize
    vmem = 4 * tm * d * 4 + tm * d * wbytes + 4 * d * dh * wbytes + 4 * n_mem * dh * wbytes \
        + 4 * tm * dh * 4 + 3 * tm * n_mem * 4 + (4 << 20)
    tiles_per_seq = seq // tm
    return pl.pallas_call(
        _xattn_kernel,
        out_shape=jax.ShapeDtypeStruct((m, d), F32),
        grid=(m // tm, XATTN_HEADS),
        in_specs=[
            pl.BlockSpec((tm, d), lambda i, j: (i, 0)),
            pl.BlockSpec((1, d), lambda i, j: (0, 0)),
            pl.BlockSpec((d, dh), lambda i, j: (0, j)),
            pl.BlockSpec((n_mem, dh), lambda i, j: (i // tiles_per_seq, j)),
            pl.BlockSpec((n_mem, dh), lambda i, j: (i // tiles_per_seq, XATTN_HEADS + j)),
            pl.BlockSpec((dh, d), lambda i, j: (j, 0)),
        ],
        out_specs=pl.BlockSpec((tm, d), lambda i, j: (i, 0)),
        scratch_shapes=[pltpu.VMEM((tm, d), wq.dtype)],
        compiler_params=_cparams(("parallel", "arbitrary"), vmem),
        name="xattn",
    )(x, g, wq, kv, kv, wo)


_IN_SIZES = (N_BRANCH * D_MODEL, MLA_Q_RANK, MLA_KV_RANK, MLA_ROPE, 3 * FOX_HEADS * FOX_DH, FOX_HEADS,
             GLA_HEADS * GLA_DK, GLA_HEADS * GLA_DK, GLA_HEADS * GLA_DV, GLA_GATE_RANK,
             GLA_HEADS * GLA_DV, SSD_INNER, SSD_INNER + 2 * SSD_GROUPS * SSD_STATE, SSD_HEADS)


def _rot_cols(w):
    half = w.shape[-1] // 2
    return jnp.concatenate([-w[..., half:], w[..., :half]], axis=-1)


def _prep_w_in(w_in):
    d = w_in.shape[0]
    offs = np.cumsum((0,) + _IN_SIZES)
    seg = lambda i: w_in[:, offs[i]:offs[i + 1]]
    (_, cq, ckv, kr, fox, fox_f, gq, gk, gv, glr, gr, sz, sxbc, sdt) = [seg(i) for i in range(len(_IN_SIZES))]
    z = lambda n: jnp.zeros((d, n), w_in.dtype)
    w_p = jnp.concatenate([
        cq, z(512 - MLA_Q_RANK), ckv, kr, z(LANES - MLA_ROPE), _rot_cols(kr), z(LANES - MLA_ROPE),
        fox, gq, gk, gv, gr, sz, sxbc], axis=1)
    assert w_p.shape[1] == P_WIDTH
    cs = np.ones((1, P_WIDTH), np.float32)
    cs[:, P_FOX_Q:P_FOX_K] = FOX_DH ** -0.5
    cs[:, P_GLA_Q:P_GLA_K] = GLA_DK ** -0.5
    w_s = jnp.concatenate([glr, sdt, fox_f, z(LANES - GLA_GATE_RANK - SSD_HEADS - FOX_HEADS)], axis=1)
    w_st = jnp.concatenate([fox_f, z(PST_DT - FOX_HEADS), sdt], axis=1).T
    return w_p.astype(MXU_DTYPE), jnp.asarray(cs), w_s.astype(MXU_DTYPE), w_st.astype(MXU_DTYPE)


def _prep_mla(w_uq, w_ukv):
    r = w_uq.shape[0]
    wq = w_uq.reshape(r, MLA_HEADS, MLA_NOPE + MLA_ROPE)
    nope, pe = wq[..., :MLA_NOPE], wq[..., MLA_NOPE:]
    zpad = jnp.zeros((r, MLA_HEADS, MLA_DK_PAD - MLA_NOPE - MLA_ROPE), w_uq.dtype)
    wa = jnp.concatenate([nope, pe, zpad], axis=-1).reshape(r, MLA_HEADS * MLA_DK_PAD)
    wb = jnp.concatenate([_rot_cols(pe), zpad], axis=-1).reshape(r, MLA_HEADS * LANES)
    rk = w_ukv.shape[0]
    wkv = w_ukv.reshape(rk, MLA_HEADS, MLA_NOPE + MLA_V)
    wkv = jnp.concatenate([wkv[..., :MLA_NOPE].reshape(rk, -1), wkv[..., MLA_NOPE:].reshape(rk, -1)], axis=1)
    return wa.astype(MXU_DTYPE), wb.astype(MXU_DTYPE), wkv.astype(MXU_DTYPE)


def _tiles(m, seq):
    big = min(512, seq)
    return dict(ffn_tm=big, ffn_tf=512, proj_tm=big, proj_tn=1408, prep_tm=big, flash_tq=min(512, seq),
                gla_tm=min(256, seq), ssd_tm=min(256, seq), merge_tm=big, merge_tc=512, xattn_tm=big,
                rope_tm=min(1024, m))


@jax.jit
def kernel(x, mem, positions, ffn1_norm, ffn1_w_gate, ffn1_w_up, ffn1_w_down, mix_norm, w_in, mla_q_norm, mla_kv_norm, mla_w_uq, mla_w_ukv, fox_f_bias, gla_w_g2, gla_b_g2, gla_norm, ssd_conv_w, ssd_conv_b, ssd_dt_bias, ssd_a_log, ssd_d, ssd_norm, w_branch, w_out, xattn_norm, mem_norm, xattn_w_q, xattn_w_k, xattn_w_v, xattn_w_o, ffn2_norm, ffn2_w_gate, ffn2_w_up, ffn2_w_down, final_norm):
    batch, seq, d = x.shape
    n_mem = mem.shape[1]
    m = batch * seq
    t = _tiles(m, seq)
    depth = w_in.shape[0]
    cast = lambda a: a.astype(MXU_DTYPE)
    row = lambda a: a.reshape(1, -1).astype(F32)

    xs = x.reshape(m, d)
    mem2 = mem.reshape(batch * n_mem, d)
    ct, st = _rope_tables(positions.reshape(m, 1), tm=t["rope_tm"])

    expand = np.zeros((LANES, SSD_INNER), np.float32)
    for h in range(SSD_HEADS):
        expand[PS_DT + h, h * SSD_HEADDIM:(h + 1) * SSD_HEADDIM] = 1.0
    expand = jnp.asarray(expand, MXU_DTYPE)
    rep = lambda v: jnp.repeat(v.astype(F32), SSD_HEADDIM).reshape(1, SSD_INNER)

    for l in range(depth):
        xs = _ffn(xs, row(ffn1_norm[l]), cast(ffn1_w_gate[l]), cast(ffn1_w_up[l]), cast(ffn1_w_down[l]),
                  row(final_norm), final_norm=False, tm=t["ffn_tm"], tf=t["ffn_tf"])

        w_p, cs, w_s, w_st = _prep_w_in(w_in[l])
        p, ps, pst = _proj(xs, row(mix_norm[l]), w_p, cs, w_s, w_st, tm=t["proj_tm"], tn=t["proj_tn"])

        wa, wb, wkv = _prep_mla(mla_w_uq[l], mla_w_ukv[l])
        q_mla, k_mla, v_mla = _mla_prep(p, ct, st, row(mla_q_norm[l]), row(mla_kv_norm[l]), wa, wb, wkv,
                                        tm=t["prep_tm"])
        y_mla = _flash(q_mla, k_mla, v_mla, None, batch=batch, seq=seq, heads=MLA_HEADS, dk=MLA_DK_PAD,
                       dv=MLA_V, q_off=0, k_off=0, v_off=0, tq=t["flash_tq"])

        fb_col = jnp.zeros((8, 1), F32).at[:FOX_HEADS, 0].set(fox_f_bias[l].astype(F32))
        c_fox = _fox_cum(pst, fb_col, batch=batch, seq=seq)
        y_fox = _flash(p, p, p, c_fox, batch=batch, seq=seq, heads=FOX_HEADS, dk=FOX_DH, dv=FOX_DH,
                       q_off=P_FOX_Q, k_off=P_FOX_K, v_off=P_FOX_V, tq=t["flash_tq"])

        wg2 = jnp.zeros((LANES, GLA_HEADS * GLA_DK), F32).at[PS_GLR:PS_GLR + GLA_GATE_RANK].set(gla_w_g2[l])
        y_gla = _gla(p, ps, cast(wg2), row(gla_b_g2[l]), row(gla_norm[l]), batch=batch, seq=seq, tm=t["gla_tm"])

        dtb_row = jnp.zeros((1, LANES), F32).at[0, PS_DT:PS_DT + SSD_HEADS].set(ssd_dt_bias[l])
        dtb_col = jnp.zeros((PST_ROWS, 1), F32).at[PST_DT:PST_DT + SSD_HEADS, 0].set(ssd_dt_bias[l])
        alog_col = jnp.zeros((PST_ROWS, 1), F32).at[PST_DT:PST_DT + SSD_HEADS, 0].set(ssd_a_log[l])
        y_ssd = _ssd(p, ps, pst, ssd_conv_w[l].astype(F32), row(ssd_conv_b[l]), dtb_row, dtb_col,
                     rep(ssd_a_log[l]), alog_col, rep(ssd_d[l]), row(ssd_norm[l]), expand,
                     batch=batch, seq=seq, tm=t["ssd_tm"])

        ys = jnp.stack([y_mla, y_fox, y_gla, y_ssd])
        gates = w_in[l][:, :N_BRANCH * d].reshape(d, N_BRANCH, d).transpose(1, 0, 2)
        wsq = cast(jnp.concatenate([gates, w_out[l][None]], axis=0))
        xs = _merge(xs, row(mix_norm[l]), ys, wsq, cast(w_branch[l]), tm=t["merge_tm"], tc=t["merge_tc"])

        w_kv = cast(jnp.concatenate([xattn_w_k[l], xattn_w_v[l]], axis=1))
        kv = _norm_matmul(mem2, row(mem_norm[l]), w_kv, tm=min(512, batch * n_mem), tn=1024)
        xs = _xattn(xs, row(xattn_norm[l]), cast(xattn_w_q[l]), kv, cast(xattn_w_o[l]),
                    seq=seq, n_mem=n_mem, tm=t["xattn_tm"])

        xs = _ffn(xs, row(ffn2_norm[l]), cast(ffn2_w_gate[l]), cast(ffn2_w_up[l]), cast(ffn2_w_down[l]),
                  row(final_norm), final_norm=(l == depth - 1), tm=t["ffn_tm"], tf=t["ffn_tf"])

    return xs.reshape(batch, seq, d)
```

```python
import functools

import numpy as np
import jax
import jax.numpy as jnp
from jax import lax
from jax.experimental import pallas as pl
from jax.experimental.pallas import tpu as pltpu

D_MODEL = 2048
DEPTH = 4
EPS = 1e-6
N_BRANCH = 4
BRANCH_WIDTH = 512
MLA_HEADS, MLA_NOPE, MLA_ROPE, MLA_V = 4, 128, 64, 128
MLA_Q_RANK, MLA_KV_RANK = 384, 256
MLA_DK_PAD = 256
ROPE_THETA = 10000.0
FOX_HEADS, FOX_DH = 4, 128
GLA_HEADS, GLA_DK, GLA_DV = 4, 64, 128
GLA_GATE_RANK, GLA_GATE_NORM, GLA_CHUNK = 16, 16.0, 64
SSD_HEADS, SSD_HEADDIM, SSD_GROUPS, SSD_STATE, SSD_CONV, SSD_CHUNK = 8, 64, 2, 128, 4, 128
SSD_INNER = SSD_HEADS * SSD_HEADDIM
XATTN_HEADS = 4
XATTN_DH = D_MODEL // XATTN_HEADS
D_FF = 5632

LANES = 128
SUBLANES = 8
V7X_VMEM_LIMIT_BYTES = 60 * 1024 * 1024

MXU_DTYPE = jnp.bfloat16
F32 = jnp.float32
LOG2E = 1.4426950408889634

P_CQ, P_CKV, P_KR, P_KRROT = 0, 512, 768, 896
P_FOX_Q, P_FOX_K, P_FOX_V = 1024, 1536, 2048
P_GLA_Q, P_GLA_K, P_GLA_V, P_GLA_R = 2560, 2816, 3072, 3584
P_SSD_Z, P_SSD_X, P_SSD_B, P_SSD_C = 4096, 4608, 5120, 5376
P_WIDTH = 5632
PS_GLR, PS_DT, PS_F = 0, 16, 24
PST_ROWS = 32
SLOT_MLA, SLOT_FOX, SLOT_GLA, SLOT_SSD = 0, 1, 2, 3


def _cparams(sem, vmem_bytes):
    return pltpu.CompilerParams(dimension_semantics=sem,
                                vmem_limit_bytes=int(min(vmem_bytes, V7X_VMEM_LIMIT_BYTES)))


def _dot(a, b):
    return jnp.dot(a, b, preferred_element_type=F32)


def _dot_nt(a, b):
    return lax.dot_general(a, b, (((1,), (1,)), ((), ())), preferred_element_type=F32)


def _dot_tn(a, b):
    return lax.dot_general(a, b, (((0,), (0,)), ((), ())), preferred_element_type=F32)


def _split3(x):
    hi = x.astype(MXU_DTYPE)
    r1 = x - hi.astype(F32)
    mid = r1.astype(MXU_DTYPE)
    lo = (r1 - mid.astype(F32)).astype(MXU_DTYPE)
    return hi, mid, lo


def _dot01_right(x, m01):
    if MXU_DTYPE == F32:
        return _dot(x, m01.astype(F32))
    m = m01.astype(MXU_DTYPE)
    hi, mid, lo = _split3(x)
    return _dot(hi, m) + _dot(mid, m) + _dot(lo, m)


def _dot01_left(m01, x):
    if MXU_DTYPE == F32:
        return _dot(m01.astype(F32), x)
    m = m01.astype(MXU_DTYPE)
    hi, mid, lo = _split3(x)
    return _dot(m, hi) + _dot(m, mid) + _dot(m, lo)


def _rmsnorm(x, g):
    ms = jnp.mean(x * x, axis=-1, keepdims=True)
    return x * lax.rsqrt(ms + EPS) * g


def _log_sigmoid(x):
    return jnp.minimum(x, 0.0) - jnp.log1p(jnp.exp(-jnp.abs(x)))


def _softplus(x):
    return jnp.maximum(x, 0.0) + jnp.log1p(jnp.exp(-jnp.abs(x)))


def _silu(x):
    return x * jax.nn.sigmoid(x)


def _chunk_masks(n, chunk):
    shift = chunk.bit_length() - 1
    assert 1 << shift == chunk
    r = lax.broadcasted_iota(jnp.int32, (n, n), 0)
    c = lax.broadcasted_iota(jnp.int32, (n, n), 1)
    same = jnp.right_shift(r, shift) == jnp.right_shift(c, shift)
    return jnp.logical_and(same, r >= c), jnp.logical_and(same, r <= c), same


def _norm_rows(x_ref, g_ref, h_ref, copy_ref=None, *, rows=256):
    tm = x_ref.shape[0]
    rows = min(rows, tm)
    g = g_ref[...]
    for r in range(tm // rows):
        sl = slice(r * rows, (r + 1) * rows)
        x = x_ref[sl, :]
        h_ref[sl, :] = _rmsnorm(x, g).astype(h_ref.dtype)
        if copy_ref is not None:
            copy_ref[sl, :] = x


def _ffn_kernel(x_ref, g_ref, wg_ref, wu_ref, wd_ref, fg_ref, o_ref, h_ref, *, final_norm, tc):
    j = pl.program_id(1)

    @pl.when(j == 0)
    def _():
        _norm_rows(x_ref, g_ref, h_ref, o_ref)

    h = h_ref[...]
    a = _dot(h, wg_ref[...])
    u = _dot(h, wu_ref[...])
    t = (_silu(a) * (0.5 * u)).astype(MXU_DTYPE)
    d = o_ref.shape[-1]
    for c in range(d // tc):
        cols = slice(c * tc, (c + 1) * tc)
        o_ref[:, cols] += _dot(t, wd_ref[:, cols])

    if final_norm:
        @pl.when(j == pl.num_programs(1) - 1)
        def _():
            _norm_rows(o_ref, fg_ref, o_ref)


def _ffn(x, g, wg, wu, wd, fg, *, final_norm, tm, tf, tc):
    m, d = x.shape
    f = wg.shape[1]
    wb = jnp.dtype(wg.dtype).itemsize
    vmem = 3 * tm * d * 4 + tm * d * wb + 2 * 3 * d * tf * wb + 3 * tm * tf * 4 + 2 * tm * tc * 4 + (4 << 20)
    return pl.pallas_call(
        functools.partial(_ffn_kernel, final_norm=final_norm, tc=tc),
        out_shape=jax.ShapeDtypeStruct((m, d), F32),
        grid=(m // tm, f // tf),
        in_specs=[
            pl.BlockSpec((tm, d), lambda i, j: (i, 0), pipeline_mode=pl.Buffered(1)),
            pl.BlockSpec((1, d), lambda i, j: (0, 0)),
            pl.BlockSpec((d, tf), lambda i, j: (0, j)),
            pl.BlockSpec((d, tf), lambda i, j: (0, j)),
            pl.BlockSpec((tf, d), lambda i, j: (j, 0)),
            pl.BlockSpec((1, d), lambda i, j: (0, 0)),
        ],
        out_specs=pl.BlockSpec((tm, d), lambda i, j: (i, 0)),
        scratch_shapes=[pltpu.VMEM((tm, d), wg.dtype)],
        compiler_params=_cparams(("parallel", "arbitrary"), vmem),
        name="ffn",
    )(x, g, wg, wu, wd, fg)


def _proj_kernel(x_ref, g_ref, w_ref, cs_ref, ws_ref, p_ref, ps_ref, pst_ref, h_ref):
    j = pl.program_id(1)

    @pl.when(j == 0)
    def _():
        _norm_rows(x_ref, g_ref, h_ref)
        ps = _dot(h_ref[...], ws_ref[...])
        ps_ref[...] = ps
        pst_ref[...] = jnp.transpose(ps)[:PST_ROWS, :]

    p_ref[...] = (_dot(h_ref[...], w_ref[...]) * cs_ref[...]).astype(p_ref.dtype)


def _proj(x, g, w, cs, ws, *, tm, tn):
    m, d = x.shape
    n = w.shape[1]
    wb = jnp.dtype(w.dtype).itemsize
    vmem = 2 * tm * d * 4 + tm * d * wb + 2 * d * tn * wb + 2 * tm * tn * wb + 2 * tm * tn * 4 + (6 << 20)
    return pl.pallas_call(
        _proj_kernel,
        out_shape=(jax.ShapeDtypeStruct((m, n), w.dtype),
                   jax.ShapeDtypeStruct((m, LANES), F32),
                   jax.ShapeDtypeStruct((PST_ROWS, m), F32)),
        grid=(m // tm, n // tn),
        in_specs=[
            pl.BlockSpec((tm, d), lambda i, j: (i, 0)),
            pl.BlockSpec((1, d), lambda i, j: (0, 0)),
            pl.BlockSpec((d, tn), lambda i, j: (0, j)),
            pl.BlockSpec((1, tn), lambda i, j: (0, j)),
            pl.BlockSpec((d, LANES), lambda i, j: (0, 0)),
        ],
        out_specs=(pl.BlockSpec((tm, tn), lambda i, j: (i, j)),
                   pl.BlockSpec((tm, LANES), lambda i, j: (i, 0)),
                   pl.BlockSpec((PST_ROWS, tm), lambda i, j: (0, i))),
        scratch_shapes=[pltpu.VMEM((tm, d), w.dtype)],
        compiler_params=_cparams(("parallel", "arbitrary"), vmem),
        name="proj",
    )(x, g, w, cs, ws)


def _norm_matmul_kernel(x_ref, g_ref, w_ref, o_ref):
    h = _rmsnorm(x_ref[...], g_ref[...]).astype(w_ref.dtype)
    o_ref[...] = _dot(h, w_ref[...]).astype(o_ref.dtype)


def _norm_matmul(x, g, w, *, tm, tn):
    m, d = x.shape
    n = w.shape[1]
    wb = jnp.dtype(w.dtype).itemsize
    vmem = 3 * tm * d * 4 + 2 * d * tn * wb + 3 * tm * tn * 4 + (4 << 20)
    return pl.pallas_call(
        _norm_matmul_kernel,
        out_shape=jax.ShapeDtypeStruct((m, n), w.dtype),
        grid=(m // tm, n // tn),
        in_specs=[pl.BlockSpec((tm, d), lambda i, j: (i, 0)),
                  pl.BlockSpec((1, d), lambda i, j: (0, 0)),
                  pl.BlockSpec((d, tn), lambda i, j: (0, j))],
        out_specs=pl.BlockSpec((tm, tn), lambda i, j: (i, j)),
        compiler_params=_cparams(("parallel", "parallel"), vmem),
        name="mem_kv",
    )(x, g, w)


def _rope_kernel(pos_ref, ct_ref, st_ref):
    pos = pos_ref[...].astype(F32)
    lane = lax.broadcasted_iota(jnp.int32, (1, LANES), 1)
    half = MLA_ROPE // 2
    f = jnp.bitwise_and(lane, half - 1).astype(F32)
    inv_freq = jnp.power(jnp.float32(ROPE_THETA), -f / half)
    ang = pos * inv_freq
    valid = lane < MLA_ROPE
    ct_ref[...] = jnp.where(valid, jnp.cos(ang), 0.0)
    st_ref[...] = jnp.where(valid, jnp.sin(ang), 0.0)


def _rope_tables(pos_col, *, tm):
    m = pos_col.shape[0]
    return pl.pallas_call(
        _rope_kernel,
        out_shape=(jax.ShapeDtypeStruct((m, LANES), F32), jax.ShapeDtypeStruct((m, LANES), F32)),
        grid=(m // tm,),
        in_specs=[pl.BlockSpec((tm, 1), lambda i: (i, 0))],
        out_specs=(pl.BlockSpec((tm, LANES), lambda i: (i, 0)),
                   pl.BlockSpec((tm, LANES), lambda i: (i, 0))),
        compiler_params=_cparams(("parallel",), 16 << 20),
        name="rope_tables",
    )(pos_col)


def _mla_prep_kernel(cq_ref, ckv_ref, kr_ref, krrot_ref, ct_ref, st_ref, qn_ref, kvn_ref,
                     wa_ref, wb_ref, wkv_ref, q_ref, k_ref, v_ref):
    ct = ct_ref[...]
    st = st_ref[...]
    scale = (MLA_NOPE + MLA_ROPE) ** -0.5 * LOG2E

    ckv = ckv_ref[...].astype(F32)
    ckvn = _rmsnorm(ckv, kvn_ref[...]).astype(MXU_DTYPE)
    kv = _dot(ckvn, wkv_ref[...])
    kpe = kr_ref[...].astype(F32) * ct + krrot_ref[...].astype(F32) * st
    kpe = kpe.astype(k_ref.dtype)
    hw = MLA_HEADS * MLA_NOPE
    for h in range(MLA_HEADS):
        base = h * MLA_DK_PAD
        k_ref[:, base:base + MLA_NOPE] = kv[:, h * MLA_NOPE:(h + 1) * MLA_NOPE].astype(k_ref.dtype)
        k_ref[:, base + MLA_NOPE:base + MLA_DK_PAD] = kpe
    v_ref[...] = kv[:, hw:].astype(v_ref.dtype)

    cq = cq_ref[...].astype(F32)[:, :MLA_Q_RANK]
    cqn = _rmsnorm(cq, qn_ref[...]).astype(MXU_DTYPE)
    qa = _dot(cqn, wa_ref[...])
    qb = _dot(cqn, wb_ref[...])
    for h in range(MLA_HEADS):
        base = h * MLA_DK_PAD
        q_ref[:, base:base + MLA_NOPE] = (qa[:, base:base + MLA_NOPE] * scale).astype(q_ref.dtype)
        pe = qa[:, base + MLA_NOPE:base + MLA_DK_PAD] * ct + qb[:, h * LANES:(h + 1) * LANES] * st
        q_ref[:, base + MLA_NOPE:base + MLA_DK_PAD] = (pe * scale).astype(q_ref.dtype)


def _mla_prep(p, ct, st, qn, kvn, wa, wb, wkv, *, tm):
    m = p.shape[0]
    qk_w = MLA_HEADS * MLA_DK_PAD
    v_w = MLA_HEADS * MLA_V
    full = lambda a: pl.BlockSpec(a.shape, lambda i: (0,) * a.ndim)
    return pl.pallas_call(
        _mla_prep_kernel,
        out_shape=(jax.ShapeDtypeStruct((m, qk_w), p.dtype),
                   jax.ShapeDtypeStruct((m, qk_w), p.dtype),
                   jax.ShapeDtypeStruct((m, v_w), p.dtype)),
        grid=(m // tm,),
        in_specs=[
            pl.BlockSpec((tm, 512), lambda i: (i, P_CQ // 512)),
            pl.BlockSpec((tm, 256), lambda i: (i, P_CKV // 256)),
            pl.BlockSpec((tm, 128), lambda i: (i, P_KR // 128)),
            pl.BlockSpec((tm, 128), lambda i: (i, P_KRROT // 128)),
            pl.BlockSpec((tm, LANES), lambda i: (i, 0)),
            pl.BlockSpec((tm, LANES), lambda i: (i, 0)),
            full(qn), full(kvn), full(wa), full(wb), full(wkv),
        ],
        out_specs=(pl.BlockSpec((tm, qk_w), lambda i: (i, 0)),
                   pl.BlockSpec((tm, qk_w), lambda i: (i, 0)),
                   pl.BlockSpec((tm, v_w), lambda i: (i, 0))),
        compiler_params=_cparams(("parallel",), 40 << 20),
        name="mla_prep",
    )(p, p, p, p, ct, st, qn, kvn, wa, wb, wkv)


def _fox_cum_kernel(ft_ref, fb_ref, c_ref, *, seq):
    lf = _log_sigmoid(ft_ref[...] + fb_ref[...])
    r = lax.broadcasted_iota(jnp.int32, (LANES, LANES), 0)
    c = lax.broadcasted_iota(jnp.int32, (LANES, LANES), 1)
    upper = r <= c
    carry = jnp.zeros((SUBLANES, 1), F32)
    for blk in range(seq // LANES):
        sl = slice(blk * LANES, (blk + 1) * LANES)
        cs = _dot01_right(lf[:, sl], upper) + carry
        for h in range(FOX_HEADS):
            c_ref[h, :, sl] = cs[h:h + 1, :] * LOG2E
        carry = cs[:, LANES - 1:LANES]


def _fox_cum(pst, fb_col, *, batch, seq):
    return pl.pallas_call(
        functools.partial(_fox_cum_kernel, seq=seq),
        out_shape=jax.ShapeDtypeStruct((batch * FOX_HEADS, 1, seq), F32),
        grid=(batch,),
        in_specs=[pl.BlockSpec((SUBLANES, seq), lambda b: (PS_F // SUBLANES, b)),
                  pl.BlockSpec((SUBLANES, 1), lambda b: (0, 0))],
        out_specs=pl.BlockSpec((FOX_HEADS, 1, seq), lambda b: (b, 0, 0)),
        compiler_params=_cparams(("parallel",), 16 << 20),
        name="fox_cum",
    )(pst, fb_col)


def _flash_kernel(*refs, seq, tq, groups, dk, dv, has_bias, has_alias):
    refs = list(refs)
    q_ref, k_ref, v_ref = refs[:3]
    c_ref = refs[3] if has_bias else None
    o_ref = refs[3 + int(has_bias) + int(has_alias)]
    row = lax.broadcasted_iota(jnp.int32, (tq, tq), 0)
    col = lax.broadcasted_iota(jnp.int32, (tq, tq), 1)
    causal = row >= col

    def q_tile(qi, unused):
        q0 = pl.multiple_of(qi * tq, tq)
        qs = [q_ref[pl.ds(q0, tq), g * dk:(g + 1) * dk] for g in range(groups)]
        if has_bias:
            c0 = [jnp.max(c_ref[g, :, pl.ds(q0, tq)], axis=-1, keepdims=True) for g in range(groups)]

        def scores(g, start):
            s = _dot_nt(qs[g], k_ref[pl.ds(start, tq), g * dk:(g + 1) * dk])
            if has_bias:
                s = s + (c0[g] - c_ref[g, :, pl.ds(start, tq)])
            return s

        def update(state, s, g, start):
            m, l, acc = state
            m_new = jnp.maximum(m, jnp.max(s, axis=-1, keepdims=True))
            alpha = jnp.exp2(m - m_new)
            p = jnp.exp2(s - m_new)
            l = alpha * l + jnp.sum(p, axis=-1, keepdims=True)
            acc = alpha * acc + _dot(p.astype(MXU_DTYPE), v_ref[pl.ds(start, tq), g * dv:(g + 1) * dv])
            return m_new, l, acc

        def body(ki, carry):
            start = pl.multiple_of(ki * tq, tq)
            return tuple(update(carry[g], scores(g, start), g, start) for g in range(groups))

        init = tuple((jnp.full((tq, 1), -jnp.inf, F32), jnp.zeros((tq, 1), F32), jnp.zeros((tq, dv), F32))
                     for _ in range(groups))
        carry = lax.fori_loop(0, qi, body, init)
        for g in range(groups):
            s = jnp.where(causal, scores(g, q0), -jnp.inf)
            _, l, acc = update(carry[g], s, g, q0)
            o_ref[pl.ds(q0, tq), g * dv:(g + 1) * dv] = (acc / l).astype(o_ref.dtype)
        return unused

    lax.fori_loop(0, seq // tq, q_tile, 0)


def _flash(q_arr, k_arr, v_arr, c_arr, ys, *, slot, batch, seq, heads, groups, dk, dv,
           q_off, k_off, v_off, tq):
    has_bias = c_arr is not None
    has_alias = ys is not None
    gk, gv = groups * dk, groups * dv
    in_specs = [pl.BlockSpec((seq, gk), lambda b, h: (b, q_off // gk + h)),
                pl.BlockSpec((seq, gk), lambda b, h: (b, k_off // gk + h)),
                pl.BlockSpec((seq, gv), lambda b, h: (b, v_off // gv + h))]
    args = [q_arr, k_arr, v_arr]
    if has_bias:
        in_specs.append(pl.BlockSpec((groups, 1, seq), lambda b, h: (b * (heads // groups) + h, 0, 0)))
        args.append(c_arr)
    aliases = {}
    if has_alias:
        in_specs.append(pl.BlockSpec(memory_space=pl.ANY))
        aliases = {len(args): 0}
        args.append(ys)
    vmem = 2 * seq * (2 * gk + 2 * gv) * 2 + groups * (6 * tq * tq * 4 + 4 * tq * dv * 4) + (8 << 20)
    return pl.pallas_call(
        functools.partial(_flash_kernel, seq=seq, tq=tq, groups=groups, dk=dk, dv=dv,
                          has_bias=has_bias, has_alias=has_alias),
        out_shape=jax.ShapeDtypeStruct((N_BRANCH, batch * seq, heads * dv), q_arr.dtype),
        grid=(batch, heads // groups),
        in_specs=in_specs,
        out_specs=pl.BlockSpec((None, seq, gv), lambda b, h: (slot, b, h)),
        input_output_aliases=aliases,
        compiler_params=_cparams(("parallel", "parallel"), vmem),
        name="flash_fox" if has_bias else "flash_mla",
    )(*args)


def _gla_kernel(q_ref, k_ref, v_ref, r_ref, ps_ref, wg2_ref, bg2_ref, nw_ref, ys_ref, o_ref, st_ref, *, tm):
    @pl.when(pl.program_id(1) == 0)
    def _():
        st_ref[...] = jnp.zeros_like(st_ref)

    glin = _dot(ps_ref[...].astype(MXU_DTYPE), wg2_ref[...]) + bg2_ref[...]
    gl = _log_sigmoid(glin) / GLA_GATE_NORM
    tril, _, same = _chunk_masks(tm, GLA_CHUNK)
    gcum = _dot01_left(tril, gl)
    gend = _dot01_left(same, gl)
    q = q_ref[...].astype(F32)
    k = k_ref[...].astype(F32)
    qd = (q * jnp.exp(gcum)).astype(MXU_DTYPE)
    kd = (k * jnp.exp(-gcum)).astype(MXU_DTYPE)
    ku = (k * jnp.exp(gend - gcum)).astype(MXU_DTYPE)
    dec = jnp.exp(gend)
    cr = lax.broadcasted_iota(jnp.int32, (GLA_CHUNK, GLA_CHUNK), 0)
    cc = lax.broadcasted_iota(jnp.int32, (GLA_CHUNK, GLA_CHUNK), 1)
    causal = cr >= cc
    nw = nw_ref[...]

    for c in range(tm // GLA_CHUNK):
        rows = slice(c * GLA_CHUNK, (c + 1) * GLA_CHUNK)
        for h in range(GLA_HEADS):
            kc = slice(h * GLA_DK, (h + 1) * GLA_DK)
            vc = slice(h * GLA_DV, (h + 1) * GLA_DV)
            qd_h, kd_h, ku_h = qd[rows, kc], kd[rows, kc], ku[rows, kc]
            v_h = v_ref[rows, vc]
            att = jnp.where(causal, _dot_nt(qd_h, kd_h), 0.0)
            state_t = st_ref[:, kc]
            o = _dot(att.astype(MXU_DTYPE), v_h) + _dot_nt(qd_h, state_t.astype(MXU_DTYPE))
            st_ref[:, kc] = state_t * dec[c * GLA_CHUNK:c * GLA_CHUNK + 1, kc] + _dot_tn(v_h, ku_h)
            o = _rmsnorm(o, nw)
            o_ref[rows, vc] = (o * _silu(r_ref[rows, vc].astype(F32))).astype(o_ref.dtype)


def _gla(p, ps, wg2, bg2, nw, ys, *, batch, seq, tm):
    nt = seq // tm
    full = lambda a: pl.BlockSpec(a.shape, lambda b, i: (0,) * a.ndim)
    row = lambda b, i: b * nt + i
    return pl.pallas_call(
        functools.partial(_gla_kernel, tm=tm),
        out_shape=jax.ShapeDtypeStruct(ys.shape, ys.dtype),
        grid=(batch, nt),
        in_specs=[
            pl.BlockSpec((tm, 256), lambda b, i: (row(b, i), P_GLA_Q // 256)),
            pl.BlockSpec((tm, 256), lambda b, i: (row(b, i), P_GLA_K // 256)),
            pl.BlockSpec((tm, 512), lambda b, i: (row(b, i), P_GLA_V // 512)),
            pl.BlockSpec((tm, 512), lambda b, i: (row(b, i), P_GLA_R // 512)),
            pl.BlockSpec((tm, LANES), lambda b, i: (row(b, i), 0)),
            full(wg2), full(bg2), full(nw),
            pl.BlockSpec(memory_space=pl.ANY),
        ],
        out_specs=pl.BlockSpec((None, tm, GLA_HEADS * GLA_DV), lambda b, i: (SLOT_GLA, row(b, i), 0)),
        input_output_aliases={8: 0},
        scratch_shapes=[pltpu.VMEM((GLA_DV, GLA_HEADS * GLA_DK), F32)],
        compiler_params=_cparams(("parallel", "arbitrary"), 32 << 20),
        name="gla",
    )(p, p, p, p, ps, wg2, bg2, nw, ys)


def _ssd_kernel(z_ref, x_ref, b_ref, c_ref, ps_ref, pst_ref, cw_ref, cb_ref, dtb_row_ref,
                dtb_col_ref, alog_row_ref, alog_col_ref, dskip_ref, nw_ref, exp_ref, ys_ref,
                o_ref, st_ref, tail_ref, *, tm):
    @pl.when(pl.program_id(1) == 0)
    def _():
        st_ref[...] = jnp.zeros_like(st_ref)
        tail_ref[...] = jnp.zeros_like(tail_ref)

    inner = SSD_INNER
    gw = inner // SSD_GROUPS
    xbc = jnp.concatenate([x_ref[...], b_ref[...], c_ref[...]], axis=-1).astype(F32)
    ext = jnp.concatenate([tail_ref[...], xbc], axis=0)
    tail_ref[...] = xbc[tm - SUBLANES:, :]
    conv = cb_ref[...]
    for j in range(SSD_CONV):
        shift = SSD_CONV - 1 - j
        src = ext if shift == 0 else pltpu.roll(ext, shift, axis=0)
        conv = conv + cw_ref[j:j + 1, :] * src[SUBLANES:, :]
    xc = _silu(conv)
    xs = xc[:, :inner]
    bm = xc[:, inner:inner + SSD_GROUPS * SSD_STATE].astype(MXU_DTYPE)
    cm = xc[:, inner + SSD_GROUPS * SSD_STATE:].astype(MXU_DTYPE)

    dt_exp = _dot01_right(_softplus(ps_ref[...] + dtb_row_ref[...]), exp_ref[...])
    a_row = -jnp.exp(alog_row_ref[...])
    dta = dt_exp * a_row
    tril, triu, same = _chunk_masks(tm, SSD_CHUNK)
    acum = _dot01_left(tril, dta)
    aend = _dot01_left(same, dta)
    dta_t = _softplus(pst_ref[...] + dtb_col_ref[...]) * (-jnp.exp(alog_col_ref[...]))
    acum_t = _dot01_right(dta_t, triu)
    x_dt = xs * dt_exp
    xw = (x_dt * jnp.exp(aend - acum)).astype(MXU_DTYPE)
    x_dt_m = x_dt.astype(MXU_DTYPE)
    e_acum = jnp.exp(acum)
    e_aend = jnp.exp(aend)
    cr = lax.broadcasted_iota(jnp.int32, (SSD_CHUNK, SSD_CHUNK), 0)
    cc = lax.broadcasted_iota(jnp.int32, (SSD_CHUNK, SSD_CHUNK), 1)
    causal = cr >= cc
    dskip = dskip_ref[...]
    nw = nw_ref[...]
    hg = SSD_HEADS // SSD_GROUPS

    for c in range(tm // SSD_CHUNK):
        r0 = c * SSD_CHUNK
        rows = slice(r0, r0 + SSD_CHUNK)
        for g in range(SSD_GROUPS):
            gl = slice(g * gw, (g + 1) * gw)
            nl = slice(g * SSD_STATE, (g + 1) * SSD_STATE)
            b_g, c_g = bm[rows, nl], cm[rows, nl]
            cb = _dot_nt(c_g, b_g)
            parts = []
            for hh in range(hg):
                h = g * hg + hh
                lane0 = h * SSD_HEADDIM
                colv = jnp.broadcast_to(acum[rows, lane0:lane0 + 1], (SSD_CHUNK, SSD_CHUNK))
                rowv = acum_t[h:h + 1, r0:r0 + SSD_CHUNK]
                decay = jnp.where(causal, jnp.exp(colv - rowv), 0.0)
                mmat = (cb * decay).astype(MXU_DTYPE)
                parts.append(_dot(mmat, x_dt_m[rows, lane0:lane0 + SSD_HEADDIM]))
            y_diag = jnp.concatenate(parts, axis=-1)
            state_t = st_ref[:, gl]
            y_off = _dot(c_g, state_t.astype(MXU_DTYPE)) * e_acum[rows, gl]
            st_ref[:, gl] = state_t * e_aend[r0:r0 + 1, gl] + _dot_tn(b_g, xw[rows, gl])
            y = y_diag + y_off + dskip[:, gl] * xs[rows, gl]
            y = y * _silu(z_ref[rows, gl].astype(F32))
            o_ref[rows, gl] = _rmsnorm(y, nw[:, gl]).astype(o_ref.dtype)


def _ssd(p, ps, pst, cw, cb, dtb_row, dtb_col, alog_row, alog_col, dskip, nw, expand, ys, *, batch, seq, tm):
    nt = seq // tm
    full = lambda a: pl.BlockSpec(a.shape, lambda b, i: (0,) * a.ndim)
    row = lambda b, i: b * nt + i
    conv_dim = SSD_INNER + 2 * SSD_GROUPS * SSD_STATE
    return pl.pallas_call(
        functools.partial(_ssd_kernel, tm=tm),
        out_shape=jax.ShapeDtypeStruct(ys.shape, ys.dtype),
        grid=(batch, nt),
        in_specs=[
            pl.BlockSpec((tm, 512), lambda b, i: (row(b, i), P_SSD_Z // 512)),
            pl.BlockSpec((tm, 512), lambda b, i: (row(b, i), P_SSD_X // 512)),
            pl.BlockSpec((tm, 256), lambda b, i: (row(b, i), P_SSD_B // 256)),
            pl.BlockSpec((tm, 256), lambda b, i: (row(b, i), P_SSD_C // 256)),
            pl.BlockSpec((tm, LANES), lambda b, i: (row(b, i), 0)),
            pl.BlockSpec((SUBLANES, tm), lambda b, i: (PS_DT // SUBLANES, row(b, i))),
            full(cw), full(cb), full(dtb_row), full(dtb_col), full(alog_row), full(alog_col),
            full(dskip), full(nw), full(expand),
            pl.BlockSpec(memory_space=pl.ANY),
        ],
        out_specs=pl.BlockSpec((None, tm, SSD_INNER), lambda b, i: (SLOT_SSD, row(b, i), 0)),
        input_output_aliases={15: 0},
        scratch_shapes=[pltpu.VMEM((SSD_STATE, SSD_INNER), F32),
                        pltpu.VMEM((SUBLANES, conv_dim), F32)],
        compiler_params=_cparams(("parallel", "arbitrary"), 40 << 20),
        name="ssd",
    )(p, p, p, p, ps, pst, cw, cb, dtb_row, dtb_col, alog_row, alog_col, dskip, nw, expand, ys)


def _merge_kernel(x_ref, g_ref, y_ref, wsq_ref, wbr_ref, o_ref, h_ref, acc_ref, *, tc):
    j = pl.program_id(1)
    d = x_ref.shape[-1]

    @pl.when(j == 0)
    def _():
        _norm_rows(x_ref, g_ref, h_ref)
        acc_ref[...] = jnp.zeros_like(acc_ref)

    @pl.when(j < N_BRANCH)
    def _():
        h = h_ref[...]
        y = y_ref[...]
        for c in range(d // tc):
            cols = slice(c * tc, (c + 1) * tc)
            gate = jax.nn.sigmoid(_dot(h, wsq_ref[:, cols]))
            acc_ref[:, cols] += gate * _dot(y, wbr_ref[:, cols])

    @pl.when(j == N_BRANCH)
    def _():
        merged = acc_ref[...].astype(MXU_DTYPE)
        for c in range(d // tc):
            cols = slice(c * tc, (c + 1) * tc)
            o_ref[:, cols] = x_ref[:, cols] + _dot(merged, wsq_ref[:, cols])


def _merge(x, g, ys, wsq, wbr, *, tm, tc):
    m, d = x.shape
    bw = ys.shape[-1]
    wbytes = jnp.dtype(wsq.dtype).itemsize
    vmem = 4 * tm * d * 4 + tm * d * wbytes + tm * d * 4 + 2 * d * d * wbytes \
        + 2 * bw * d * wbytes + 2 * tm * bw * wbytes + 4 * tm * tc * 4 + (4 << 20)
    last = N_BRANCH - 1
    return pl.pallas_call(
        functools.partial(_merge_kernel, tc=tc),
        out_shape=jax.ShapeDtypeStruct((m, d), F32),
        grid=(m // tm, N_BRANCH + 1),
        in_specs=[
            pl.BlockSpec((tm, d), lambda i, j: (i, 0)),
            pl.BlockSpec((1, d), lambda i, j: (0, 0)),
            pl.BlockSpec((None, tm, bw), lambda i, j: (jnp.minimum(j, last), i, 0)),
            pl.BlockSpec((d, d), lambda i, j: (0, j)),
            pl.BlockSpec((None, bw, d), lambda i, j: (jnp.minimum(j, last), 0, 0)),
        ],
        out_specs=pl.BlockSpec((tm, d), lambda i, j: (i, 0)),
        scratch_shapes=[pltpu.VMEM((tm, d), wsq.dtype), pltpu.VMEM((tm, d), F32)],
        compiler_params=_cparams(("parallel", "arbitrary"), vmem),
        name="merge",
    )(x, g, ys, wsq, wbr)


def _xattn_kernel(x_ref, g_ref, wq_ref, k_ref, v_ref, wo_ref, o_ref, h_ref, oc_ref, *, tc):
    _norm_rows(x_ref, g_ref, h_ref)
    h = h_ref[...]
    dh = XATTN_DH
    for hd in range(XATTN_HEADS):
        cols = slice(hd * dh, (hd + 1) * dh)
        q = _dot(h, wq_ref[:, cols]).astype(MXU_DTYPE)
        s = _dot_nt(q, k_ref[:, cols]) * (dh ** -0.5 * LOG2E)
        p = jnp.exp2(s - jnp.max(s, axis=-1, keepdims=True))
        p = p / jnp.sum(p, axis=-1, keepdims=True)
        oc_ref[:, cols] = _dot(p.astype(MXU_DTYPE), v_ref[:, cols]).astype(oc_ref.dtype)
    oc = oc_ref[...]
    d = x_ref.shape[-1]
    for c in range(d // tc):
        cols = slice(c * tc, (c + 1) * tc)
        o_ref[:, cols] = x_ref[:, cols] + _dot(oc, wo_ref[:, cols])


def _xattn(x, g, wq, kv, wo, *, seq, n_mem, tm, tc):
    m, d = x.shape
    wbytes = jnp.dtype(wq.dtype).itemsize
    vmem = 4 * tm * d * 4 + 2 * tm * d * wbytes + 2 * d * d * wbytes + 4 * n_mem * d * wbytes \
        + 3 * tm * XATTN_DH * 4 + 3 * tm * n_mem * 4 + 2 * tm * tc * 4 + (6 << 20)
    tiles_per_seq = seq // tm
    return pl.pallas_call(
        functools.partial(_xattn_kernel, tc=tc),
        out_shape=jax.ShapeDtypeStruct((m, d), F32),
        grid=(m // tm,),
        in_specs=[
            pl.BlockSpec((tm, d), lambda i: (i, 0)),
            pl.BlockSpec((1, d), lambda i: (0, 0)),
            pl.BlockSpec((d, d), lambda i: (0, 0), pipeline_mode=pl.Buffered(1)),
            pl.BlockSpec((n_mem, d), lambda i: (i // tiles_per_seq, 0)),
            pl.BlockSpec((n_mem, d), lambda i: (i // tiles_per_seq, 1)),
            pl.BlockSpec((d, d), lambda i: (0, 0), pipeline_mode=pl.Buffered(1)),
        ],
        out_specs=pl.BlockSpec((tm, d), lambda i: (i, 0)),
        scratch_shapes=[pltpu.VMEM((tm, d), wq.dtype), pltpu.VMEM((tm, d), wq.dtype)],
        compiler_params=_cparams(("parallel",), vmem),
        name="xattn",
    )(x, g, wq, kv, kv, wo)


_IN_SIZES = (N_BRANCH * D_MODEL, MLA_Q_RANK, MLA_KV_RANK, MLA_ROPE, 3 * FOX_HEADS * FOX_DH, FOX_HEADS,
             GLA_HEADS * GLA_DK, GLA_HEADS * GLA_DK, GLA_HEADS * GLA_DV, GLA_GATE_RANK,
             GLA_HEADS * GLA_DV, SSD_INNER, SSD_INNER + 2 * SSD_GROUPS * SSD_STATE, SSD_HEADS)


def _rot_cols(w):
    half = w.shape[-1] // 2
    return jnp.concatenate([-w[..., half:], w[..., :half]], axis=-1)


def _prep_w_in(w_in):
    d = w_in.shape[0]
    offs = np.cumsum((0,) + _IN_SIZES)
    seg = lambda i: w_in[:, offs[i]:offs[i + 1]]
    (_, cq, ckv, kr, fox, fox_f, gq, gk, gv, glr, gr, sz, sxbc, sdt) = [seg(i) for i in range(len(_IN_SIZES))]
    z = lambda n: jnp.zeros((d, n), w_in.dtype)
    w_p = jnp.concatenate([
        cq, z(512 - MLA_Q_RANK), ckv, kr, z(LANES - MLA_ROPE), _rot_cols(kr), z(LANES - MLA_ROPE),
        fox, gq, gk, gv, gr, sz, sxbc], axis=1)
    assert w_p.shape[1] == P_WIDTH
    cs = np.ones((1, P_WIDTH), np.float32)
    cs[:, P_FOX_Q:P_FOX_K] = FOX_DH ** -0.5 * LOG2E
    cs[:, P_GLA_Q:P_GLA_K] = GLA_DK ** -0.5
    w_s = jnp.concatenate([glr, sdt, fox_f, z(LANES - GLA_GATE_RANK - SSD_HEADS - FOX_HEADS)], axis=1)
    return w_p, jnp.asarray(cs), w_s


def _prep_mla(w_uq, w_ukv):
    r = w_uq.shape[0]
    wq = w_uq.reshape(r, MLA_HEADS, MLA_NOPE + MLA_ROPE)
    nope, pe = wq[..., :MLA_NOPE], wq[..., MLA_NOPE:]
    zpad = jnp.zeros((r, MLA_HEADS, MLA_DK_PAD - MLA_NOPE - MLA_ROPE), w_uq.dtype)
    wa = jnp.concatenate([nope, pe, zpad], axis=-1).reshape(r, MLA_HEADS * MLA_DK_PAD)
    wb = jnp.concatenate([_rot_cols(pe), zpad], axis=-1).reshape(r, MLA_HEADS * LANES)
    rk = w_ukv.shape[0]
    wkv = w_ukv.reshape(rk, MLA_HEADS, MLA_NOPE + MLA_V)
    wkv = jnp.concatenate([wkv[..., :MLA_NOPE].reshape(rk, -1), wkv[..., MLA_NOPE:].reshape(rk, -1)], axis=1)
    return wa.astype(MXU_DTYPE), wb.astype(MXU_DTYPE), wkv.astype(MXU_DTYPE)


def _tiles(m, seq):
    mid = min(512, seq)
    return dict(ffn_tm=min(1024, seq), ffn_tf=512, ffn_tc=512, proj_tm=mid, proj_tn=2816, prep_tm=mid,
                flash_tq=mid, gla_tm=min(256, seq), ssd_tm=min(256, seq), merge_tm=mid, merge_tc=512,
                xattn_tm=mid, xattn_tc=512, rope_tm=min(1024, m))


@jax.jit
def kernel(x, mem, positions, ffn1_norm, ffn1_w_gate, ffn1_w_up, ffn1_w_down, mix_norm, w_in, mla_q_norm, mla_kv_norm, mla_w_uq, mla_w_ukv, fox_f_bias, gla_w_g2, gla_b_g2, gla_norm, ssd_conv_w, ssd_conv_b, ssd_dt_bias, ssd_a_log, ssd_d, ssd_norm, w_branch, w_out, xattn_norm, mem_norm, xattn_w_q, xattn_w_k, xattn_w_v, xattn_w_o, ffn2_norm, ffn2_w_gate, ffn2_w_up, ffn2_w_down, final_norm):
    batch, seq, d = x.shape
    n_mem = mem.shape[1]
    m = batch * seq
    t = _tiles(m, seq)
    depth = w_in.shape[0]
    cast = lambda a: a.astype(MXU_DTYPE)
    row = lambda a: a.reshape(1, -1).astype(F32)
    col8 = lambda v: jnp.zeros((SUBLANES, 1), F32).at[:v.shape[0], 0].set(v.astype(F32))

    xs = x.reshape(m, d)
    mem2 = mem.reshape(batch * n_mem, d)
    ct, st = _rope_tables(positions.reshape(m, 1), tm=t["rope_tm"])

    expand = np.zeros((LANES, SSD_INNER), np.float32)
    for h in range(SSD_HEADS):
        expand[PS_DT + h, h * SSD_HEADDIM:(h + 1) * SSD_HEADDIM] = 1.0
    expand = jnp.asarray(expand, MXU_DTYPE)
    rep = lambda v: jnp.repeat(v.astype(F32), SSD_HEADDIM).reshape(1, SSD_INNER)
    ffn = functools.partial(_ffn, tm=t["ffn_tm"], tf=t["ffn_tf"], tc=t["ffn_tc"])

    for l in range(depth):
        xs = ffn(xs, row(ffn1_norm[l]), cast(ffn1_w_gate[l]), cast(ffn1_w_up[l]), cast(ffn1_w_down[l]),
                 row(final_norm), final_norm=False)

        w_in_l = cast(w_in[l])
        w_p, cs, w_s = _prep_w_in(w_in_l)
        p, ps, pst = _proj(xs, row(mix_norm[l]), w_p, cs, w_s, tm=t["proj_tm"], tn=t["proj_tn"])

        wa, wb, wkv = _prep_mla(mla_w_uq[l], mla_w_ukv[l])
        q_mla, k_mla, v_mla = _mla_prep(p, ct, st, row(mla_q_norm[l]), row(mla_kv_norm[l]), wa, wb, wkv,
                                        tm=t["prep_tm"])
        ys = _flash(q_mla, k_mla, v_mla, None, None, slot=SLOT_MLA, batch=batch, seq=seq, heads=MLA_HEADS,
                    groups=2, dk=MLA_DK_PAD, dv=MLA_V, q_off=0, k_off=0, v_off=0, tq=t["flash_tq"])

        c_fox = _fox_cum(pst, col8(fox_f_bias[l]), batch=batch, seq=seq)
        ys = _flash(p, p, p, c_fox, ys, slot=SLOT_FOX, batch=batch, seq=seq, heads=FOX_HEADS, groups=2,
                    dk=FOX_DH, dv=FOX_DH, q_off=P_FOX_Q, k_off=P_FOX_K, v_off=P_FOX_V, tq=t["flash_tq"])

        wg2 = jnp.zeros((LANES, GLA_HEADS * GLA_DK), F32).at[PS_GLR:PS_GLR + GLA_GATE_RANK].set(gla_w_g2[l])
        ys = _gla(p, ps, cast(wg2), row(gla_b_g2[l]), row(gla_norm[l]), ys, batch=batch, seq=seq, tm=t["gla_tm"])

        dtb_row = jnp.zeros((1, LANES), F32).at[0, PS_DT:PS_DT + SSD_HEADS].set(ssd_dt_bias[l])
        ys = _ssd(p, ps, pst, ssd_conv_w[l].astype(F32), row(ssd_conv_b[l]), dtb_row, col8(ssd_dt_bias[l]),
                  rep(ssd_a_log[l]), col8(ssd_a_log[l]), rep(ssd_d[l]), row(ssd_norm[l]), expand, ys,
                  batch=batch, seq=seq, tm=t["ssd_tm"])

        wsq = jnp.concatenate([w_in_l[:, :N_BRANCH * d], cast(w_out[l])], axis=1)
        xs = _merge(xs, row(mix_norm[l]), ys, wsq, cast(w_branch[l]), tm=t["merge_tm"], tc=t["merge_tc"])

        w_kv = cast(jnp.concatenate([xattn_w_k[l], xattn_w_v[l]], axis=1))
        kv = _norm_matmul(mem2, row(mem_norm[l]), w_kv, tm=min(512, batch * n_mem), tn=1024)
        xs = _xattn(xs, row(xattn_norm[l]), cast(xattn_w_q[l]), kv, cast(xattn_w_o[l]),
                    seq=seq, n_mem=n_mem, tm=t["xattn_tm"], tc=t["xattn_tc"])

        xs = ffn(xs, row(ffn2_norm[l]), cast(ffn2_w_gate[l]), cast(ffn2_w_up[l]), cast(ffn2_w_down[l]),
                 row(final_norm), final_norm=(l == depth - 1))

    return xs.reshape(batch, seq, d)
```

```python
import functools

import numpy as np
import jax
import jax.numpy as jnp
from jax import lax
from jax.experimental import pallas as pl
from jax.experimental.pallas import tpu as pltpu

D_MODEL = 2048
DEPTH = 4
EPS = 1e-6
N_BRANCH = 4
BRANCH_WIDTH = 512
MLA_HEADS, MLA_NOPE, MLA_ROPE, MLA_V = 4, 128, 64, 128
MLA_Q_RANK, MLA_KV_RANK = 384, 256
MLA_DK_PAD = 256
ROPE_THETA = 10000.0
FOX_HEADS, FOX_DH = 4, 128
GLA_HEADS, GLA_DK, GLA_DV = 4, 64, 128
GLA_GATE_RANK, GLA_GATE_NORM, GLA_CHUNK = 16, 16.0, 64
SSD_HEADS, SSD_HEADDIM, SSD_GROUPS, SSD_STATE, SSD_CONV, SSD_CHUNK = 8, 64, 2, 128, 4, 128
SSD_INNER = SSD_HEADS * SSD_HEADDIM
XATTN_HEADS = 4
XATTN_DH = D_MODEL // XATTN_HEADS
D_FF = 5632

LANES = 128
SUBLANES = 8
V7X_VMEM_LIMIT_BYTES = 60 * 1024 * 1024

MXU_DTYPE = jnp.bfloat16
FFN_SLOTS = 3
F32 = jnp.float32
LOG2E = 1.4426950408889634

P_CQ, P_CKV, P_KR, P_KRROT = 0, 512, 768, 896
P_FOX_Q, P_FOX_K, P_FOX_V = 1024, 1536, 2048
P_GLA_Q, P_GLA_K, P_GLA_V, P_GLA_R = 2560, 2816, 3072, 3584
P_SSD_Z, P_SSD_X, P_SSD_B, P_SSD_C = 4096, 4608, 5120, 5376
P_WIDTH = 5632
PS_GLR, PS_DT, PS_F = 0, 16, 24
PST_ROWS = 32


def _cparams(sem, vmem_bytes):
    return pltpu.CompilerParams(dimension_semantics=sem,
                                vmem_limit_bytes=int(min(vmem_bytes, V7X_VMEM_LIMIT_BYTES)))


def _dot(a, b):
    return jnp.dot(a, b, preferred_element_type=F32)


def _dot_nt(a, b):
    return lax.dot_general(a, b, (((1,), (1,)), ((), ())), preferred_element_type=F32)


def _dot_tn(a, b):
    return lax.dot_general(a, b, (((0,), (0,)), ((), ())), preferred_element_type=F32)


def _split3(x):
    hi = x.astype(MXU_DTYPE)
    r1 = x - hi.astype(F32)
    mid = r1.astype(MXU_DTYPE)
    lo = (r1 - mid.astype(F32)).astype(MXU_DTYPE)
    return hi, mid, lo


def _dot01_right(x, m01):
    if MXU_DTYPE == F32:
        return _dot(x, m01.astype(F32))
    m = m01.astype(MXU_DTYPE)
    hi, mid, lo = _split3(x)
    return _dot(hi, m) + _dot(mid, m) + _dot(lo, m)


def _dot01_left(m01, x):
    if MXU_DTYPE == F32:
        return _dot(m01.astype(F32), x)
    m = m01.astype(MXU_DTYPE)
    hi, mid, lo = _split3(x)
    return _dot(m, hi) + _dot(m, mid) + _dot(m, lo)


def _rmsnorm(x, g):
    ms = jnp.mean(x * x, axis=-1, keepdims=True)
    return x * lax.rsqrt(ms + EPS) * g


def _log_sigmoid(x):
    return jnp.minimum(x, 0.0) - jnp.log1p(jnp.exp(-jnp.abs(x)))


def _softplus(x):
    return jnp.maximum(x, 0.0) + jnp.log1p(jnp.exp(-jnp.abs(x)))


def _silu(x):
    return x * jax.nn.sigmoid(x)


def _chunk_masks(n, chunk):
    shift = chunk.bit_length() - 1
    assert 1 << shift == chunk
    r = lax.broadcasted_iota(jnp.int32, (n, n), 0)
    c = lax.broadcasted_iota(jnp.int32, (n, n), 1)
    same = jnp.right_shift(r, shift) == jnp.right_shift(c, shift)
    return jnp.logical_and(same, r >= c), jnp.logical_and(same, r <= c), same


def _norm_rows(x_ref, g_ref, h_ref, copy_ref=None, *, rows=256):
    tm = x_ref.shape[0]
    rows = min(rows, tm)
    g = g_ref[...]
    for r in range(tm // rows):
        sl = slice(r * rows, (r + 1) * rows)
        x = x_ref[sl, :]
        h_ref[sl, :] = _rmsnorm(x, g).astype(h_ref.dtype)
        if copy_ref is not None:
            copy_ref[sl, :] = x


def _ffn_kernel(x_ref, g_ref, wg_hbm, wu_hbm, wd_hbm, fg_ref, o_ref, h_ref, wg_buf, wu_buf, wd_buf, sem,
                *, final_norm, tf, n_chunks):
    i = pl.program_id(0)

    def chunk_copies(j, slot):
        cols = pl.ds(j * tf, tf)
        return (pltpu.make_async_copy(wg_hbm.at[:, cols], wg_buf.at[slot], sem.at[0, slot]),
                pltpu.make_async_copy(wu_hbm.at[:, cols], wu_buf.at[slot], sem.at[1, slot]),
                pltpu.make_async_copy(wd_hbm.at[cols, :], wd_buf.at[slot], sem.at[2, slot]))

    def start(j, slot):
        for cp in chunk_copies(j, slot):
            cp.start()

    @pl.when(i == 0)
    def _():
        start(0, 0)

    _norm_rows(x_ref, g_ref, h_ref, o_ref)
    h = h_ref[...]
    for j in range(n_chunks):
        slot = j % FFN_SLOTS
        for cp in chunk_copies(j, slot):
            cp.wait()
        if j + 1 < n_chunks:
            start(j + 1, (j + 1) % FFN_SLOTS)
        else:
            @pl.when(i + 1 < pl.num_programs(0))
            def _():
                start(0, 0)
        a = _dot(h, wg_buf[slot])
        u = _dot(h, wu_buf[slot])
        t = (_silu(a) * (0.5 * u)).astype(MXU_DTYPE)
        o_ref[...] += _dot(t, wd_buf[slot])

    if final_norm:
        _norm_rows(o_ref, fg_ref, o_ref)


def _ffn(x, g, wg, wu, wd, fg, *, final_norm, tm, tf):
    m, d = x.shape
    f = wg.shape[1]
    n_chunks = f // tf
    assert n_chunks * tf == f and n_chunks > FFN_SLOTS and (n_chunks - 1) % FFN_SLOTS != 0
    wb = jnp.dtype(wg.dtype).itemsize
    vmem = 4 * tm * d * 4 + tm * d * wb + FFN_SLOTS * 3 * d * tf * wb + 4 * tm * tf * 4 + tm * d * 4 + (4 << 20)
    return pl.pallas_call(
        functools.partial(_ffn_kernel, final_norm=final_norm, tf=tf, n_chunks=n_chunks),
        out_shape=jax.ShapeDtypeStruct((m, d), F32),
        grid=(m // tm,),
        in_specs=[
            pl.BlockSpec((tm, d), lambda i: (i, 0)),
            pl.BlockSpec((1, d), lambda i: (0, 0)),
            pl.BlockSpec(memory_space=pl.ANY),
            pl.BlockSpec(memory_space=pl.ANY),
            pl.BlockSpec(memory_space=pl.ANY),
            pl.BlockSpec((1, d), lambda i: (0, 0)),
        ],
        out_specs=pl.BlockSpec((tm, d), lambda i: (i, 0)),
        scratch_shapes=[pltpu.VMEM((tm, d), wg.dtype),
                        pltpu.VMEM((FFN_SLOTS, d, tf), wg.dtype),
                        pltpu.VMEM((FFN_SLOTS, d, tf), wg.dtype),
                        pltpu.VMEM((FFN_SLOTS, tf, d), wg.dtype),
                        pltpu.SemaphoreType.DMA((3, FFN_SLOTS))],
        compiler_params=_cparams(("arbitrary",), vmem),
        name="ffn",
    )(x, g, wg, wu, wd, fg)


def _proj_kernel(x_ref, g_ref, w_ref, cs_ref, ws_ref, p_ref, ps_ref, pst_ref, h_ref):
    j = pl.program_id(1)

    @pl.when(j == 0)
    def _():
        _norm_rows(x_ref, g_ref, h_ref)
        ps = _dot(h_ref[...], ws_ref[...])
        ps_ref[...] = ps
        pst_ref[...] = jnp.transpose(ps)[:PST_ROWS, :]

    p_ref[...] = (_dot(h_ref[...], w_ref[...]) * cs_ref[...]).astype(p_ref.dtype)


def _proj(x, g, w, cs, ws, *, tm, tn):
    m, d = x.shape
    n = w.shape[1]
    wb = jnp.dtype(w.dtype).itemsize
    vmem = 2 * tm * d * 4 + tm * d * wb + 2 * d * tn * wb + 2 * tm * tn * wb + 2 * tm * tn * 4 + (6 << 20)
    return pl.pallas_call(
        _proj_kernel,
        out_shape=(jax.ShapeDtypeStruct((m, n), w.dtype),
                   jax.ShapeDtypeStruct((m, LANES), F32),
                   jax.ShapeDtypeStruct((PST_ROWS, m), F32)),
        grid=(m // tm, n // tn),
        in_specs=[
            pl.BlockSpec((tm, d), lambda i, j: (i, 0)),
            pl.BlockSpec((1, d), lambda i, j: (0, 0)),
            pl.BlockSpec((d, tn), lambda i, j: (0, j)),
            pl.BlockSpec((1, tn), lambda i, j: (0, j)),
            pl.BlockSpec((d, LANES), lambda i, j: (0, 0)),
        ],
        out_specs=(pl.BlockSpec((tm, tn), lambda i, j: (i, j)),
                   pl.BlockSpec((tm, LANES), lambda i, j: (i, 0)),
                   pl.BlockSpec((PST_ROWS, tm), lambda i, j: (0, i))),
        scratch_shapes=[pltpu.VMEM((tm, d), w.dtype)],
        compiler_params=_cparams(("parallel", "arbitrary"), vmem),
        name="proj",
    )(x, g, w, cs, ws)


def _norm_matmul_kernel(x_ref, g_ref, w_ref, o_ref):
    h = _rmsnorm(x_ref[...], g_ref[...]).astype(w_ref.dtype)
    o_ref[...] = _dot(h, w_ref[...]).astype(o_ref.dtype)


def _norm_matmul(x, g, w, *, tm, tn):
    m, d = x.shape
    n = w.shape[1]
    wb = jnp.dtype(w.dtype).itemsize
    vmem = 3 * tm * d * 4 + 2 * d * tn * wb + 3 * tm * tn * 4 + (4 << 20)
    return pl.pallas_call(
        _norm_matmul_kernel,
        out_shape=jax.ShapeDtypeStruct((m, n), w.dtype),
        grid=(m // tm, n // tn),
        in_specs=[pl.BlockSpec((tm, d), lambda i, j: (i, 0)),
                  pl.BlockSpec((1, d), lambda i, j: (0, 0)),
                  pl.BlockSpec((d, tn), lambda i, j: (0, j))],
        out_specs=pl.BlockSpec((tm, tn), lambda i, j: (i, j)),
        compiler_params=_cparams(("parallel", "parallel"), vmem),
        name="mem_kv",
    )(x, g, w)


def _rope_kernel(pos_ref, ct_ref, st_ref):
    pos = pos_ref[...].astype(F32)
    lane = lax.broadcasted_iota(jnp.int32, (1, LANES), 1)
    half = MLA_ROPE // 2
    f = jnp.bitwise_and(lane, half - 1).astype(F32)
    inv_freq = jnp.power(jnp.float32(ROPE_THETA), -f / half)
    ang = pos * inv_freq
    valid = lane < MLA_ROPE
    ct_ref[...] = jnp.where(valid, jnp.cos(ang), 0.0)
    st_ref[...] = jnp.where(valid, jnp.sin(ang), 0.0)


def _rope_tables(pos_col, *, tm):
    m = pos_col.shape[0]
    return pl.pallas_call(
        _rope_kernel,
        out_shape=(jax.ShapeDtypeStruct((m, LANES), F32), jax.ShapeDtypeStruct((m, LANES), F32)),
        grid=(m // tm,),
        in_specs=[pl.BlockSpec((tm, 1), lambda i: (i, 0))],
        out_specs=(pl.BlockSpec((tm, LANES), lambda i: (i, 0)),
                   pl.BlockSpec((tm, LANES), lambda i: (i, 0))),
        compiler_params=_cparams(("parallel",), 16 << 20),
        name="rope_tables",
    )(pos_col)


def _mla_prep_kernel(cq_ref, ckv_ref, kr_ref, krrot_ref, ct_ref, st_ref, qn_ref, kvn_ref,
                     wa_ref, wb_ref, wkv_ref, q_ref, k_ref, v_ref):
    ct = ct_ref[...]
    st = st_ref[...]
    scale = (MLA_NOPE + MLA_ROPE) ** -0.5 * LOG2E

    ckv = ckv_ref[...].astype(F32)
    ckvn = _rmsnorm(ckv, kvn_ref[...]).astype(MXU_DTYPE)
    kv = _dot(ckvn, wkv_ref[...])
    kpe = kr_ref[...].astype(F32) * ct + krrot_ref[...].astype(F32) * st
    kpe = kpe.astype(k_ref.dtype)
    hw = MLA_HEADS * MLA_NOPE
    for h in range(MLA_HEADS):
        base = h * MLA_DK_PAD
        k_ref[:, base:base + MLA_NOPE] = kv[:, h * MLA_NOPE:(h + 1) * MLA_NOPE].astype(k_ref.dtype)
        k_ref[:, base + MLA_NOPE:base + MLA_DK_PAD] = kpe
    v_ref[...] = kv[:, hw:].astype(v_ref.dtype)

    cq = cq_ref[...].astype(F32)[:, :MLA_Q_RANK]
    cqn = _rmsnorm(cq, qn_ref[...]).astype(MXU_DTYPE)
    qa = _dot(cqn, wa_ref[...])
    qb = _dot(cqn, wb_ref[...])
    for h in range(MLA_HEADS):
        base = h * MLA_DK_PAD
        q_ref[:, base:base + MLA_NOPE] = (qa[:, base:base + MLA_NOPE] * scale).astype(q_ref.dtype)
        pe = qa[:, base + MLA_NOPE:base + MLA_DK_PAD] * ct + qb[:, h * LANES:(h + 1) * LANES] * st
        q_ref[:, base + MLA_NOPE:base + MLA_DK_PAD] = (pe * scale).astype(q_ref.dtype)


def _mla_prep(p, ct, st, qn, kvn, wa, wb, wkv, *, tm):
    m = p.shape[0]
    qk_w = MLA_HEADS * MLA_DK_PAD
    v_w = MLA_HEADS * MLA_V
    full = lambda a: pl.BlockSpec(a.shape, lambda i: (0,) * a.ndim)
    return pl.pallas_call(
        _mla_prep_kernel,
        out_shape=(jax.ShapeDtypeStruct((m, qk_w), p.dtype),
                   jax.ShapeDtypeStruct((m, qk_w), p.dtype),
                   jax.ShapeDtypeStruct((m, v_w), p.dtype)),
        grid=(m // tm,),
        in_specs=[
            pl.BlockSpec((tm, 512), lambda i: (i, P_CQ // 512)),
            pl.BlockSpec((tm, 256), lambda i: (i, P_CKV // 256)),
            pl.BlockSpec((tm, 128), lambda i: (i, P_KR // 128)),
            pl.BlockSpec((tm, 128), lambda i: (i, P_KRROT // 128)),
            pl.BlockSpec((tm, LANES), lambda i: (i, 0)),
            pl.BlockSpec((tm, LANES), lambda i: (i, 0)),
            full(qn), full(kvn), full(wa), full(wb), full(wkv),
        ],
        out_specs=(pl.BlockSpec((tm, qk_w), lambda i: (i, 0)),
                   pl.BlockSpec((tm, qk_w), lambda i: (i, 0)),
                   pl.BlockSpec((tm, v_w), lambda i: (i, 0))),
        compiler_params=_cparams(("parallel",), 40 << 20),
        name="mla_prep",
    )(p, p, p, p, ct, st, qn, kvn, wa, wb, wkv)


def _fox_cum_kernel(ft_ref, fb_ref, c_ref, *, seq):
    lf = _log_sigmoid(ft_ref[...] + fb_ref[...])
    r = lax.broadcasted_iota(jnp.int32, (LANES, LANES), 0)
    c = lax.broadcasted_iota(jnp.int32, (LANES, LANES), 1)
    upper = r <= c
    carry = jnp.zeros((SUBLANES, 1), F32)
    for blk in range(seq // LANES):
        sl = slice(blk * LANES, (blk + 1) * LANES)
        cs = _dot01_right(lf[:, sl], upper) + carry
        for h in range(FOX_HEADS):
            c_ref[h, :, sl] = cs[h:h + 1, :] * LOG2E
        carry = cs[:, LANES - 1:LANES]


def _fox_cum(pst, fb_col, *, batch, seq):
    return pl.pallas_call(
        functools.partial(_fox_cum_kernel, seq=seq),
        out_shape=jax.ShapeDtypeStruct((batch * FOX_HEADS, 1, seq), F32),
        grid=(batch,),
        in_specs=[pl.BlockSpec((SUBLANES, seq), lambda b: (PS_F // SUBLANES, b)),
                  pl.BlockSpec((SUBLANES, 1), lambda b: (0, 0))],
        out_specs=pl.BlockSpec((FOX_HEADS, 1, seq), lambda b: (b, 0, 0)),
        compiler_params=_cparams(("parallel",), 16 << 20),
        name="fox_cum",
    )(pst, fb_col)


def _flash_kernel(*refs, seq, tb, rmult, groups, dk, dv, has_bias):
    refs = list(refs)
    q_ref, k_ref, v_ref = refs[:3]
    c_ref = refs[3] if has_bias else None
    o_ref = refs[3 + int(has_bias)]
    rows_st = tb * rmult
    row = lax.broadcasted_iota(jnp.int32, (tb, tb), 0)
    col = lax.broadcasted_iota(jnp.int32, (tb, tb), 1)
    causal = row >= col

    def update(state, s, v_blk):
        m, l, acc = state
        m_new = jnp.maximum(m, jnp.max(s, axis=-1, keepdims=True))
        alpha = jnp.exp2(m - m_new)
        p = jnp.exp2(s - m_new)
        l = alpha * l + jnp.sum(p, axis=-1, keepdims=True)
        acc = alpha * acc + _dot(p.astype(MXU_DTYPE), v_blk)
        return m_new, l, acc

    def super_tile(pj, unused):
        r0 = pl.multiple_of(pj * rows_st, rows_st)
        qs = [q_ref[pl.ds(r0, rows_st), g * dk:(g + 1) * dk] for g in range(groups)]
        if has_bias:
            c0 = [jnp.max(c_ref[g, :, pl.ds(r0, tb)], axis=-1, keepdims=True) for g in range(groups)]

        def scores(q, g, start, width=tb):
            s = _dot_nt(q, k_ref[pl.ds(start, width), g * dk:(g + 1) * dk])
            if has_bias:
                s = s + (c0[g] - c_ref[g, :, pl.ds(start, width)])
            return s

        def v_block(g, start, width=tb):
            return v_ref[pl.ds(start, width), g * dv:(g + 1) * dv]

        def body(ki, carry):
            start = pl.multiple_of(ki * rows_st, rows_st)
            return tuple(update(carry[g], scores(qs[g], g, start, rows_st), v_block(g, start, rows_st))
                         for g in range(groups))

        init = tuple((jnp.full((rows_st, 1), -jnp.inf, F32), jnp.zeros((rows_st, 1), F32),
                      jnp.zeros((rows_st, dv), F32)) for _ in range(groups))
        carry = lax.fori_loop(0, pj, body, init)

        for g in range(groups):
            m, l, acc = carry[g]
            for a in range(rmult):
                rs = slice(a * tb, (a + 1) * tb)
                state = (m[rs], l[rs], acc[rs])
                for kb in range(a + 1):
                    start = pl.multiple_of(r0 + kb * tb, tb)
                    s = scores(qs[g][rs], g, start)
                    if kb == a:
                        s = jnp.where(causal, s, -jnp.inf)
                    state = update(state, s, v_block(g, start))
                o_ref[pl.ds(pl.multiple_of(r0 + a * tb, tb), tb), g * dv:(g + 1) * dv] = \
                    (state[2] / state[1]).astype(o_ref.dtype)
        return unused

    lax.fori_loop(0, seq // rows_st, super_tile, 0)


def _flash(q_arr, k_arr, v_arr, c_arr, *, batch, seq, heads, groups, dk, dv,
           q_off, k_off, v_off, tb, rmult):
    rows_st = tb * rmult
    has_bias = c_arr is not None
    gk, gv = groups * dk, groups * dv
    in_specs = [pl.BlockSpec((seq, gk), lambda b, h: (b, q_off // gk + h)),
                pl.BlockSpec((seq, gk), lambda b, h: (b, k_off // gk + h)),
                pl.BlockSpec((seq, gv), lambda b, h: (b, v_off // gv + h))]
    args = [q_arr, k_arr, v_arr]
    if has_bias:
        in_specs.append(pl.BlockSpec((groups, 1, seq), lambda b, h: (b * (heads // groups) + h, 0, 0)))
        args.append(c_arr)
    vmem = 2 * seq * (2 * gk + 2 * gv) * 2 + groups * (3 * rows_st * rows_st * 4 + 4 * rows_st * dv * 4) + (8 << 20)
    return pl.pallas_call(
        functools.partial(_flash_kernel, seq=seq, tb=tb, rmult=rmult, groups=groups, dk=dk, dv=dv,
                          has_bias=has_bias),
        out_shape=jax.ShapeDtypeStruct((batch * seq, heads * dv), q_arr.dtype),
        grid=(batch, heads // groups),
        in_specs=in_specs,
        out_specs=pl.BlockSpec((seq, gv), lambda b, h: (b, h)),
        compiler_params=_cparams(("parallel", "parallel"), vmem),
        name="flash_fox" if has_bias else "flash_mla",
    )(*args)


def _gla_kernel(q_ref, k_ref, v_ref, r_ref, ps_ref, wg2_ref, bg2_ref, nw_ref, o_ref, st_ref, *, tm):
    @pl.when(pl.program_id(1) == 0)
    def _():
        st_ref[...] = jnp.zeros_like(st_ref)

    glin = _dot(ps_ref[...].astype(MXU_DTYPE), wg2_ref[...]) + bg2_ref[...]
    gl = _log_sigmoid(glin) / GLA_GATE_NORM
    tril, _, same = _chunk_masks(tm, GLA_CHUNK)
    gcum = _dot01_left(tril, gl)
    gend = _dot01_left(same, gl)
    q = q_ref[...].astype(F32)
    k = k_ref[...].astype(F32)
    qd = (q * jnp.exp(gcum)).astype(MXU_DTYPE)
    kd = (k * jnp.exp(-gcum)).astype(MXU_DTYPE)
    ku = (k * jnp.exp(gend - gcum)).astype(MXU_DTYPE)
    dec = jnp.exp(gend)
    cr = lax.broadcasted_iota(jnp.int32, (GLA_CHUNK, GLA_CHUNK), 0)
    cc = lax.broadcasted_iota(jnp.int32, (GLA_CHUNK, GLA_CHUNK), 1)
    causal = cr >= cc
    nw = nw_ref[...]

    for c in range(tm // GLA_CHUNK):
        rows = slice(c * GLA_CHUNK, (c + 1) * GLA_CHUNK)
        for h in range(GLA_HEADS):
            kc = slice(h * GLA_DK, (h + 1) * GLA_DK)
            vc = slice(h * GLA_DV, (h + 1) * GLA_DV)
            qd_h, kd_h, ku_h = qd[rows, kc], kd[rows, kc], ku[rows, kc]
            v_h = v_ref[rows, vc]
            att = jnp.where(causal, _dot_nt(qd_h, kd_h), 0.0)
            state_t = st_ref[:, kc]
            o = _dot(att.astype(MXU_DTYPE), v_h) + _dot_nt(qd_h, state_t.astype(MXU_DTYPE))
            st_ref[:, kc] = state_t * dec[c * GLA_CHUNK:c * GLA_CHUNK + 1, kc] + _dot_tn(v_h, ku_h)
            o = _rmsnorm(o, nw)
            o_ref[rows, vc] = (o * _silu(r_ref[rows, vc].astype(F32))).astype(o_ref.dtype)


def _gla(p, ps, wg2, bg2, nw, *, batch, seq, tm):
    nt = seq // tm
    full = lambda a: pl.BlockSpec(a.shape, lambda b, i: (0,) * a.ndim)
    row = lambda b, i: b * nt + i
    return pl.pallas_call(
        functools.partial(_gla_kernel, tm=tm),
        out_shape=jax.ShapeDtypeStruct((p.shape[0], GLA_HEADS * GLA_DV), p.dtype),
        grid=(batch, nt),
        in_specs=[
            pl.BlockSpec((tm, 256), lambda b, i: (row(b, i), P_GLA_Q // 256)),
            pl.BlockSpec((tm, 256), lambda b, i: (row(b, i), P_GLA_K // 256)),
            pl.BlockSpec((tm, 512), lambda b, i: (row(b, i), P_GLA_V // 512)),
            pl.BlockSpec((tm, 512), lambda b, i: (row(b, i), P_GLA_R // 512)),
            pl.BlockSpec((tm, LANES), lambda b, i: (row(b, i), 0)),
            full(wg2), full(bg2), full(nw),
        ],
        out_specs=pl.BlockSpec((tm, GLA_HEADS * GLA_DV), lambda b, i: (row(b, i), 0)),
        scratch_shapes=[pltpu.VMEM((GLA_DV, GLA_HEADS * GLA_DK), F32)],
        compiler_params=_cparams(("parallel", "arbitrary"), 32 << 20),
        name="gla",
    )(p, p, p, p, ps, wg2, bg2, nw)


def _ssd_kernel(z_ref, x_ref, b_ref, c_ref, ps_ref, pst_ref, cw_ref, cb_ref, dtb_row_ref,
                dtb_col_ref, alog_row_ref, alog_col_ref, dskip_ref, nw_ref, exp_ref,
                o_ref, st_ref, tail_ref, *, tm):
    @pl.when(pl.program_id(1) == 0)
    def _():
        st_ref[...] = jnp.zeros_like(st_ref)
        tail_ref[...] = jnp.zeros_like(tail_ref)

    inner = SSD_INNER
    gw = inner // SSD_GROUPS
    xbc = jnp.concatenate([x_ref[...], b_ref[...], c_ref[...]], axis=-1).astype(F32)
    ext = jnp.concatenate([tail_ref[...], xbc], axis=0)
    tail_ref[...] = xbc[tm - SUBLANES:, :]
    conv = cb_ref[...]
    for j in range(SSD_CONV):
        shift = SSD_CONV - 1 - j
        src = ext if shift == 0 else pltpu.roll(ext, shift, axis=0)
        conv = conv + cw_ref[j:j + 1, :] * src[SUBLANES:, :]
    xc = _silu(conv)
    xs = xc[:, :inner]
    bm = xc[:, inner:inner + SSD_GROUPS * SSD_STATE].astype(MXU_DTYPE)
    cm = xc[:, inner + SSD_GROUPS * SSD_STATE:].astype(MXU_DTYPE)

    dt_exp = _dot01_right(_softplus(ps_ref[...] + dtb_row_ref[...]), exp_ref[...])
    a_row = -jnp.exp(alog_row_ref[...])
    dta = dt_exp * a_row
    tril, triu, same = _chunk_masks(tm, SSD_CHUNK)
    acum = _dot01_left(tril, dta)
    aend = _dot01_left(same, dta)
    dta_t = _softplus(pst_ref[...] + dtb_col_ref[...]) * (-jnp.exp(alog_col_ref[...]))
    acum_t = _dot01_right(dta_t, triu)
    x_dt = xs * dt_exp
    xw = (x_dt * jnp.exp(aend - acum)).astype(MXU_DTYPE)
    x_dt_m = x_dt.astype(MXU_DTYPE)
    e_acum = jnp.exp(acum)
    e_aend = jnp.exp(aend)
    cr = lax.broadcasted_iota(jnp.int32, (SSD_CHUNK, SSD_CHUNK), 0)
    cc = lax.broadcasted_iota(jnp.int32, (SSD_CHUNK, SSD_CHUNK), 1)
    causal = cr >= cc
    dskip = dskip_ref[...]
    nw = nw_ref[...]
    hg = SSD_HEADS // SSD_GROUPS

    for c in range(tm // SSD_CHUNK):
        r0 = c * SSD_CHUNK
        rows = slice(r0, r0 + SSD_CHUNK)
        for g in range(SSD_GROUPS):
            gl = slice(g * gw, (g + 1) * gw)
            nl = slice(g * SSD_STATE, (g + 1) * SSD_STATE)
            b_g, c_g = bm[rows, nl], cm[rows, nl]
            cb = _dot_nt(c_g, b_g)
            parts = []
            for hh in range(hg):
                h = g * hg + hh
                lane0 = h * SSD_HEADDIM
                colv = jnp.broadcast_to(acum[rows, lane0:lane0 + 1], (SSD_CHUNK, SSD_CHUNK))
                rowv = acum_t[h:h + 1, r0:r0 + SSD_CHUNK]
                decay = jnp.where(causal, jnp.exp(colv - rowv), 0.0)
                mmat = (cb * decay).astype(MXU_DTYPE)
                parts.append(_dot(mmat, x_dt_m[rows, lane0:lane0 + SSD_HEADDIM]))
            y_diag = jnp.concatenate(parts, axis=-1)
            state_t = st_ref[:, gl]
            y_off = _dot(c_g, state_t.astype(MXU_DTYPE)) * e_acum[rows, gl]
            st_ref[:, gl] = state_t * e_aend[r0:r0 + 1, gl] + _dot_tn(b_g, xw[rows, gl])
            y = y_diag + y_off + dskip[:, gl] * xs[rows, gl]
            y = y * _silu(z_ref[rows, gl].astype(F32))
            o_ref[rows, gl] = _rmsnorm(y, nw[:, gl]).astype(o_ref.dtype)


def _ssd(p, ps, pst, cw, cb, dtb_row, dtb_col, alog_row, alog_col, dskip, nw, expand, *, batch, seq, tm):
    nt = seq // tm
    full = lambda a: pl.BlockSpec(a.shape, lambda b, i: (0,) * a.ndim)
    row = lambda b, i: b * nt + i
    conv_dim = SSD_INNER + 2 * SSD_GROUPS * SSD_STATE
    return pl.pallas_call(
        functools.partial(_ssd_kernel, tm=tm),
        out_shape=jax.ShapeDtypeStruct((p.shape[0], SSD_INNER), p.dtype),
        grid=(batch, nt),
        in_specs=[
            pl.BlockSpec((tm, 512), lambda b, i: (row(b, i), P_SSD_Z // 512)),
            pl.BlockSpec((tm, 512), lambda b, i: (row(b, i), P_SSD_X // 512)),
            pl.BlockSpec((tm, 256), lambda b, i: (row(b, i), P_SSD_B // 256)),
            pl.BlockSpec((tm, 256), lambda b, i: (row(b, i), P_SSD_C // 256)),
            pl.BlockSpec((tm, LANES), lambda b, i: (row(b, i), 0)),
            pl.BlockSpec((SUBLANES, tm), lambda b, i: (PS_DT // SUBLANES, row(b, i))),
            full(cw), full(cb), full(dtb_row), full(dtb_col), full(alog_row), full(alog_col),
            full(dskip), full(nw), full(expand),
        ],
        out_specs=pl.BlockSpec((tm, SSD_INNER), lambda b, i: (row(b, i), 0)),
        scratch_shapes=[pltpu.VMEM((SSD_STATE, SSD_INNER), F32),
                        pltpu.VMEM((SUBLANES, conv_dim), F32)],
        compiler_params=_cparams(("parallel", "arbitrary"), 40 << 20),
        name="ssd",
    )(p, p, p, p, ps, pst, cw, cb, dtb_row, dtb_col, alog_row, alog_col, dskip, nw, expand)


def _merge_kernel(x_ref, g_ref, y0_ref, y1_ref, y2_ref, y3_ref, wsq_ref, wbr_ref, o_ref,
                  h_ref, acc_ref, ys_ref, *, tc):
    j = pl.program_id(1)
    d = x_ref.shape[-1]

    @pl.when(j == 0)
    def _():
        _norm_rows(x_ref, g_ref, h_ref)
        acc_ref[...] = jnp.zeros_like(acc_ref)
        for b, y_ref in enumerate((y0_ref, y1_ref, y2_ref, y3_ref)):
            ys_ref[b] = y_ref[...]

    @pl.when(j < N_BRANCH)
    def _():
        h = h_ref[...]
        y = ys_ref[j]
        for c in range(d // tc):
            cols = slice(c * tc, (c + 1) * tc)
            gate = jax.nn.sigmoid(_dot(h, wsq_ref[:, cols]))
            acc_ref[:, cols] += gate * _dot(y, wbr_ref[:, cols])

    @pl.when(j == N_BRANCH)
    def _():
        merged = acc_ref[...].astype(MXU_DTYPE)
        for c in range(d // tc):
            cols = slice(c * tc, (c + 1) * tc)
            o_ref[:, cols] = x_ref[:, cols] + _dot(merged, wsq_ref[:, cols])


def _merge(x, g, ys, wsq, wbr, *, tm, tc):
    m, d = x.shape
    assert len(ys) == N_BRANCH
    bw = ys[0].shape[-1]
    wbytes = jnp.dtype(wsq.dtype).itemsize
    vmem = 4 * tm * d * 4 + tm * d * wbytes + tm * d * 4 + 2 * d * d * wbytes \
        + 2 * bw * d * wbytes + 3 * N_BRANCH * tm * bw * wbytes + 4 * tm * tc * 4 + (4 << 20)
    y_spec = pl.BlockSpec((tm, bw), lambda i, j: (i, 0))
    last = N_BRANCH - 1
    return pl.pallas_call(
        functools.partial(_merge_kernel, tc=tc),
        out_shape=jax.ShapeDtypeStruct((m, d), F32),
        grid=(m // tm, N_BRANCH + 1),
        in_specs=[
            pl.BlockSpec((tm, d), lambda i, j: (i, 0)),
            pl.BlockSpec((1, d), lambda i, j: (0, 0)),
            y_spec, y_spec, y_spec, y_spec,
            pl.BlockSpec((d, d), lambda i, j: (0, j)),
            pl.BlockSpec((None, bw, d), lambda i, j: (jnp.minimum(j, last), 0, 0)),
        ],
        out_specs=pl.BlockSpec((tm, d), lambda i, j: (i, 0)),
        scratch_shapes=[pltpu.VMEM((tm, d), wsq.dtype), pltpu.VMEM((tm, d), F32),
                        pltpu.VMEM((N_BRANCH, tm, bw), ys[0].dtype)],
        compiler_params=_cparams(("parallel", "arbitrary"), vmem),
        name="merge",
    )(x, g, *ys, wsq, wbr)


def _xattn_kernel(x_ref, g_ref, wq_ref, k_ref, v_ref, wo_ref, o_ref, h_ref, oc_ref, *, tc):
    _norm_rows(x_ref, g_ref, h_ref)
    h = h_ref[...]
    dh = XATTN_DH
    for hd in range(XATTN_HEADS):
        cols = slice(hd * dh, (hd + 1) * dh)
        q = _dot(h, wq_ref[:, cols]).astype(MXU_DTYPE)
        s = _dot_nt(q, k_ref[:, cols]) * (dh ** -0.5 * LOG2E)
        p = jnp.exp2(s - jnp.max(s, axis=-1, keepdims=True))
        p = p / jnp.sum(p, axis=-1, keepdims=True)
        oc_ref[:, cols] = _dot(p.astype(MXU_DTYPE), v_ref[:, cols]).astype(oc_ref.dtype)
    oc = oc_ref[...]
    d = x_ref.shape[-1]
    for c in range(d // tc):
        cols = slice(c * tc, (c + 1) * tc)
        o_ref[:, cols] = x_ref[:, cols] + _dot(oc, wo_ref[:, cols])


def _xattn(x, g, wq, kv, wo, *, seq, n_mem, tm, tc):
    m, d = x.shape
    wbytes = jnp.dtype(wq.dtype).itemsize
    vmem = 4 * tm * d * 4 + 2 * tm * d * wbytes + 2 * d * d * wbytes + 4 * n_mem * d * wbytes \
        + 3 * tm * XATTN_DH * 4 + 3 * tm * n_mem * 4 + 2 * tm * tc * 4 + (6 << 20)
    tiles_per_seq = seq // tm
    return pl.pallas_call(
        functools.partial(_xattn_kernel, tc=tc),
        out_shape=jax.ShapeDtypeStruct((m, d), F32),
        grid=(m // tm,),
        in_specs=[
            pl.BlockSpec((tm, d), lambda i: (i, 0)),
            pl.BlockSpec((1, d), lambda i: (0, 0)),
            pl.BlockSpec((d, d), lambda i: (0, 0), pipeline_mode=pl.Buffered(1)),
            pl.BlockSpec((n_mem, d), lambda i: (i // tiles_per_seq, 0)),
            pl.BlockSpec((n_mem, d), lambda i: (i // tiles_per_seq, 1)),
            pl.BlockSpec((d, d), lambda i: (0, 0), pipeline_mode=pl.Buffered(1)),
        ],
        out_specs=pl.BlockSpec((tm, d), lambda i: (i, 0)),
        scratch_shapes=[pltpu.VMEM((tm, d), wq.dtype), pltpu.VMEM((tm, d), wq.dtype)],
        compiler_params=_cparams(("parallel",), vmem),
        name="xattn",
    )(x, g, wq, kv, kv, wo)


_IN_SIZES = (N_BRANCH * D_MODEL, MLA_Q_RANK, MLA_KV_RANK, MLA_ROPE, 3 * FOX_HEADS * FOX_DH, FOX_HEADS,
             GLA_HEADS * GLA_DK, GLA_HEADS * GLA_DK, GLA_HEADS * GLA_DV, GLA_GATE_RANK,
             GLA_HEADS * GLA_DV, SSD_INNER, SSD_INNER + 2 * SSD_GROUPS * SSD_STATE, SSD_HEADS)


def _rot_cols(w):
    half = w.shape[-1] // 2
    return jnp.concatenate([-w[..., half:], w[..., :half]], axis=-1)


def _prep_w_in(w_in):
    d = w_in.shape[0]
    offs = np.cumsum((0,) + _IN_SIZES)
    seg = lambda i: w_in[:, offs[i]:offs[i + 1]]
    (_, cq, ckv, kr, fox, fox_f, gq, gk, gv, glr, gr, sz, sxbc, sdt) = [seg(i) for i in range(len(_IN_SIZES))]
    z = lambda n: jnp.zeros((d, n), w_in.dtype)
    w_p = jnp.concatenate([
        cq, z(512 - MLA_Q_RANK), ckv, kr, z(LANES - MLA_ROPE), _rot_cols(kr), z(LANES - MLA_ROPE),
        fox, gq, gk, gv, gr, sz, sxbc], axis=1)
    assert w_p.shape[1] == P_WIDTH
    cs = np.ones((1, P_WIDTH), np.float32)
    cs[:, P_FOX_Q:P_FOX_K] = FOX_DH ** -0.5 * LOG2E
    cs[:, P_GLA_Q:P_GLA_K] = GLA_DK ** -0.5
    w_s = jnp.concatenate([glr, sdt, fox_f, z(LANES - GLA_GATE_RANK - SSD_HEADS - FOX_HEADS)], axis=1)
    return w_p.astype(MXU_DTYPE), jnp.asarray(cs), w_s.astype(MXU_DTYPE)


def _prep_mla(w_uq, w_ukv):
    r = w_uq.shape[0]
    wq = w_uq.reshape(r, MLA_HEADS, MLA_NOPE + MLA_ROPE)
    nope, pe = wq[..., :MLA_NOPE], wq[..., MLA_NOPE:]
    zpad = jnp.zeros((r, MLA_HEADS, MLA_DK_PAD - MLA_NOPE - MLA_ROPE), w_uq.dtype)
    wa = jnp.concatenate([nope, pe, zpad], axis=-1).reshape(r, MLA_HEADS * MLA_DK_PAD)
    wb = jnp.concatenate([_rot_cols(pe), zpad], axis=-1).reshape(r, MLA_HEADS * LANES)
    rk = w_ukv.shape[0]
    wkv = w_ukv.reshape(rk, MLA_HEADS, MLA_NOPE + MLA_V)
    wkv = jnp.concatenate([wkv[..., :MLA_NOPE].reshape(rk, -1), wkv[..., MLA_NOPE:].reshape(rk, -1)], axis=1)
    return wa.astype(MXU_DTYPE), wb.astype(MXU_DTYPE), wkv.astype(MXU_DTYPE)


def _tiles(m, seq):
    mid = min(512, seq)
    return dict(ffn_tm=mid, ffn_tf=512, proj_tm=mid, proj_tn=2816, prep_tm=mid,
                flash_tb=mid, flash_rmult=min(2, seq // mid), gla_tm=min(256, seq), ssd_tm=min(256, seq), merge_tm=mid, merge_tc=512,
                xattn_tm=mid, xattn_tc=512, rope_tm=min(1024, m))


@jax.jit
def kernel(x, mem, positions, ffn1_norm, ffn1_w_gate, ffn1_w_up, ffn1_w_down, mix_norm, w_in, mla_q_norm, mla_kv_norm, mla_w_uq, mla_w_ukv, fox_f_bias, gla_w_g2, gla_b_g2, gla_norm, ssd_conv_w, ssd_conv_b, ssd_dt_bias, ssd_a_log, ssd_d, ssd_norm, w_branch, w_out, xattn_norm, mem_norm, xattn_w_q, xattn_w_k, xattn_w_v, xattn_w_o, ffn2_norm, ffn2_w_gate, ffn2_w_up, ffn2_w_down, final_norm):
    batch, seq, d = x.shape
    n_mem = mem.shape[1]
    m = batch * seq
    t = _tiles(m, seq)
    depth = w_in.shape[0]
    cast = lambda a: a.astype(MXU_DTYPE)
    row = lambda a: a.reshape(1, -1).astype(F32)
    col8 = lambda v: jnp.zeros((SUBLANES, 1), F32).at[:v.shape[0], 0].set(v.astype(F32))

    xs = x.reshape(m, d)
    mem2 = mem.reshape(batch * n_mem, d)
    ct, st = _rope_tables(positions.reshape(m, 1), tm=t["rope_tm"])

    expand = np.zeros((LANES, SSD_INNER), np.float32)
    for h in range(SSD_HEADS):
        expand[PS_DT + h, h * SSD_HEADDIM:(h + 1) * SSD_HEADDIM] = 1.0
    expand = jnp.asarray(expand, MXU_DTYPE)
    rep = lambda v: jnp.repeat(v.astype(F32), SSD_HEADDIM).reshape(1, SSD_INNER)
    ffn = functools.partial(_ffn, tm=t["ffn_tm"], tf=t["ffn_tf"])
    flash = functools.partial(_flash, batch=batch, seq=seq, groups=2, tb=t["flash_tb"], rmult=t["flash_rmult"])

    for l in range(depth):
        xs = ffn(xs, row(ffn1_norm[l]), cast(ffn1_w_gate[l]), cast(ffn1_w_up[l]), cast(ffn1_w_down[l]),
                 row(final_norm), final_norm=False)

        w_p, cs, w_s = _prep_w_in(w_in[l])
        p, ps, pst = _proj(xs, row(mix_norm[l]), w_p, cs, w_s, tm=t["proj_tm"], tn=t["proj_tn"])

        wa, wb, wkv = _prep_mla(mla_w_uq[l], mla_w_ukv[l])
        q_mla, k_mla, v_mla = _mla_prep(p, ct, st, row(mla_q_norm[l]), row(mla_kv_norm[l]), wa, wb, wkv,
                                        tm=t["prep_tm"])
        y_mla = flash(q_mla, k_mla, v_mla, None, heads=MLA_HEADS,
                      dk=MLA_DK_PAD, dv=MLA_V, q_off=0, k_off=0, v_off=0)

        c_fox = _fox_cum(pst, col8(fox_f_bias[l]), batch=batch, seq=seq)
        y_fox = flash(p, p, p, c_fox, heads=FOX_HEADS,
                      dk=FOX_DH, dv=FOX_DH, q_off=P_FOX_Q, k_off=P_FOX_K, v_off=P_FOX_V)

        wg2 = jnp.zeros((LANES, GLA_HEADS * GLA_DK), F32).at[PS_GLR:PS_GLR + GLA_GATE_RANK].set(gla_w_g2[l])
        y_gla = _gla(p, ps, cast(wg2), row(gla_b_g2[l]), row(gla_norm[l]), batch=batch, seq=seq, tm=t["gla_tm"])

        dtb_row = jnp.zeros((1, LANES), F32).at[0, PS_DT:PS_DT + SSD_HEADS].set(ssd_dt_bias[l])
        y_ssd = _ssd(p, ps, pst, ssd_conv_w[l].astype(F32), row(ssd_conv_b[l]), dtb_row, col8(ssd_dt_bias[l]),
                     rep(ssd_a_log[l]), col8(ssd_a_log[l]), rep(ssd_d[l]), row(ssd_norm[l]), expand,
                     batch=batch, seq=seq, tm=t["ssd_tm"])

        wsq = cast(jnp.concatenate([w_in[l][:, :N_BRANCH * d], w_out[l]], axis=1))
        xs = _merge(xs, row(mix_norm[l]), (y_mla, y_fox, y_gla, y_ssd), wsq, cast(w_branch[l]), tm=t["merge_tm"], tc=t["merge_tc"])

        w_kv = cast(jnp.concatenate([xattn_w_k[l], xattn_w_v[l]], axis=1))
        kv = _norm_matmul(mem2, row(mem_norm[l]), w_kv, tm=min(512, batch * n_mem), tn=1024)
        xs = _xattn(xs, row(xattn_norm[l]), cast(xattn_w_q[l]), kv, cast(xattn_w_o[l]),
                    seq=seq, n_mem=n_mem, tm=t["xattn_tm"], tc=t["xattn_tc"])

        xs = ffn(xs, row(ffn2_norm[l]), cast(ffn2_w_gate[l]), cast(ffn2_w_up[l]), cast(ffn2_w_down[l]),
                 row(final_norm), final_norm=(l == depth - 1))

    return xs.reshape(batch, seq, d)
```

```python
import functools

import numpy as np
import jax
import jax.numpy as jnp
from jax import lax
from jax.experimental import pallas as pl
from jax.experimental.pallas import tpu as pltpu

D_MODEL = 2048
DEPTH = 4
EPS = 1e-6
N_BRANCH = 4
BRANCH_WIDTH = 512
MLA_HEADS, MLA_NOPE, MLA_ROPE, MLA_V = 4, 128, 64, 128
MLA_Q_RANK, MLA_KV_RANK = 384, 256
MLA_DK_PAD = 256
ROPE_THETA = 10000.0
FOX_HEADS, FOX_DH = 4, 128
GLA_HEADS, GLA_DK, GLA_DV = 4, 64, 128
GLA_GATE_RANK, GLA_GATE_NORM, GLA_CHUNK = 16, 16.0, 64
SSD_HEADS, SSD_HEADDIM, SSD_GROUPS, SSD_STATE, SSD_CONV, SSD_CHUNK = 8, 64, 2, 128, 4, 128
SSD_INNER = SSD_HEADS * SSD_HEADDIM
XATTN_HEADS = 4
XATTN_DH = D_MODEL // XATTN_HEADS
D_FF = 5632

LANES = 128
SUBLANES = 8
V7X_VMEM_LIMIT_BYTES = 60 * 1024 * 1024

MXU_DTYPE = jnp.bfloat16
FFN_SLOTS = 3
F32 = jnp.float32
LOG2E = 1.4426950408889634

P_CQ, P_CKV, P_KR, P_KRROT = 0, 512, 768, 896
P_FOX_Q, P_FOX_K, P_FOX_V = 1024, 1536, 2048
P_GLA_Q, P_GLA_K, P_GLA_V, P_GLA_R = 2560, 2816, 3072, 3584
P_SSD_Z, P_SSD_X, P_SSD_B, P_SSD_C = 4096, 4608, 5120, 5376
P_WIDTH = 5632
PS_GLR, PS_DT, PS_F = 0, 16, 24
PST_ROWS = 32


def _cparams(sem, vmem_bytes):
    return pltpu.CompilerParams(dimension_semantics=sem,
                                vmem_limit_bytes=int(min(vmem_bytes, V7X_VMEM_LIMIT_BYTES)))


def _dot(a, b):
    return jnp.dot(a, b, preferred_element_type=F32)


def _dot_nt(a, b):
    return lax.dot_general(a, b, (((1,), (1,)), ((), ())), preferred_element_type=F32)


def _dot_tn(a, b):
    return lax.dot_general(a, b, (((0,), (0,)), ((), ())), preferred_element_type=F32)


def _split3(x):
    hi = x.astype(MXU_DTYPE)
    r1 = x - hi.astype(F32)
    mid = r1.astype(MXU_DTYPE)
    lo = (r1 - mid.astype(F32)).astype(MXU_DTYPE)
    return hi, mid, lo


def _dot01_right(x, m01):
    if MXU_DTYPE == F32:
        return _dot(x, m01.astype(F32))
    m = m01.astype(MXU_DTYPE)
    hi, mid, lo = _split3(x)
    return _dot(hi, m) + _dot(mid, m) + _dot(lo, m)


def _dot01_left(m01, x):
    if MXU_DTYPE == F32:
        return _dot(m01.astype(F32), x)
    m = m01.astype(MXU_DTYPE)
    hi, mid, lo = _split3(x)
    return _dot(m, hi) + _dot(m, mid) + _dot(m, lo)


def _rmsnorm(x, g):
    ms = jnp.mean(x * x, axis=-1, keepdims=True)
    return x * lax.rsqrt(ms + EPS) * g


def _log_sigmoid(x):
    return jnp.minimum(x, 0.0) - jnp.log1p(jnp.exp(-jnp.abs(x)))


def _softplus(x):
    return jnp.maximum(x, 0.0) + jnp.log1p(jnp.exp(-jnp.abs(x)))


def _silu(x):
    return x * jax.nn.sigmoid(x)


def _chunk_masks(n, chunk):
    shift = chunk.bit_length() - 1
    assert 1 << shift == chunk
    r = lax.broadcasted_iota(jnp.int32, (n, n), 0)
    c = lax.broadcasted_iota(jnp.int32, (n, n), 1)
    same = jnp.right_shift(r, shift) == jnp.right_shift(c, shift)
    return jnp.logical_and(same, r >= c), jnp.logical_and(same, r <= c), same


def _norm_rows(x_ref, g_ref, h_ref, copy_ref=None, *, rows=256):
    tm = x_ref.shape[0]
    rows = min(rows, tm)
    g = g_ref[...]
    for r in range(tm // rows):
        sl = slice(r * rows, (r + 1) * rows)
        x = x_ref[sl, :]
        h_ref[sl, :] = _rmsnorm(x, g).astype(h_ref.dtype)
        if copy_ref is not None:
            copy_ref[sl, :] = x


def _ffn_kernel(x_ref, g_ref, wg_hbm, wu_hbm, wd_hbm, fg_ref, o_ref, h_ref, wg_buf, wu_buf, wd_buf, sem,
                *, final_norm, layer, tf, n_chunks):
    i = pl.program_id(0)

    def chunk_copies(j, slot):
        cols = pl.ds(j * tf, tf)
        return (pltpu.make_async_copy(wg_hbm.at[layer, :, cols], wg_buf.at[slot], sem.at[0, slot]),
                pltpu.make_async_copy(wu_hbm.at[layer, :, cols], wu_buf.at[slot], sem.at[1, slot]),
                pltpu.make_async_copy(wd_hbm.at[layer, cols, :], wd_buf.at[slot], sem.at[2, slot]))

    def start(j, slot):
        for cp in chunk_copies(j, slot):
            cp.start()

    @pl.when(i == 0)
    def _():
        start(0, 0)

    _norm_rows(x_ref, g_ref, h_ref, o_ref)
    h = h_ref[...]
    for j in range(n_chunks):
        slot = j % FFN_SLOTS
        for cp in chunk_copies(j, slot):
            cp.wait()
        if j + 1 < n_chunks:
            start(j + 1, (j + 1) % FFN_SLOTS)
        else:
            @pl.when(i + 1 < pl.num_programs(0))
            def _():
                start(0, 0)
        a = _dot(h, wg_buf[slot])
        u = _dot(h, wu_buf[slot])
        t = (_silu(a) * (0.5 * u)).astype(MXU_DTYPE)
        o_ref[...] += _dot(t, wd_buf[slot])

    if final_norm:
        _norm_rows(o_ref, fg_ref, o_ref)


def _ffn(x, g, wg, wu, wd, fg, *, layer, final_norm, tm, tf):
    m, d = x.shape
    f = wg.shape[-1]
    n_chunks = f // tf
    assert n_chunks * tf == f and n_chunks > FFN_SLOTS and (n_chunks - 1) % FFN_SLOTS != 0
    wb = jnp.dtype(wg.dtype).itemsize
    vmem = 4 * tm * d * 4 + tm * d * wb + FFN_SLOTS * 3 * d * tf * wb + 4 * tm * tf * 4 + tm * d * 4 + (4 << 20)
    return pl.pallas_call(
        functools.partial(_ffn_kernel, final_norm=final_norm, layer=layer, tf=tf, n_chunks=n_chunks),
        out_shape=jax.ShapeDtypeStruct((m, d), F32),
        grid=(m // tm,),
        in_specs=[
            pl.BlockSpec((tm, d), lambda i: (i, 0)),
            pl.BlockSpec((1, d), lambda i: (0, 0)),
            pl.BlockSpec(memory_space=pl.ANY),
            pl.BlockSpec(memory_space=pl.ANY),
            pl.BlockSpec(memory_space=pl.ANY),
            pl.BlockSpec((1, d), lambda i: (0, 0)),
        ],
        out_specs=pl.BlockSpec((tm, d), lambda i: (i, 0)),
        scratch_shapes=[pltpu.VMEM((tm, d), wg.dtype),
                        pltpu.VMEM((FFN_SLOTS, d, tf), wg.dtype),
                        pltpu.VMEM((FFN_SLOTS, d, tf), wg.dtype),
                        pltpu.VMEM((FFN_SLOTS, tf, d), wg.dtype),
                        pltpu.SemaphoreType.DMA((3, FFN_SLOTS))],
        compiler_params=_cparams(("arbitrary",), vmem),
        name="ffn",
    )(x, g, wg, wu, wd, fg)


def _proj_kernel(x_ref, g_ref, w_ref, cs_ref, ws_ref, p_ref, ps_ref, pst_ref, h_ref):
    j = pl.program_id(1)

    @pl.when(j == 0)
    def _():
        _norm_rows(x_ref, g_ref, h_ref)
        ps = _dot(h_ref[...], ws_ref[...])
        ps_ref[...] = ps
        pst_ref[...] = jnp.transpose(ps)[:PST_ROWS, :]

    p_ref[...] = (_dot(h_ref[...], w_ref[...]) * cs_ref[...]).astype(p_ref.dtype)


def _proj(x, g, w, cs, ws, *, layer, tm, tn):
    m, d = x.shape
    n = w.shape[-1]
    wb = jnp.dtype(w.dtype).itemsize
    vmem = 2 * tm * d * 4 + tm * d * wb + 2 * d * tn * wb + 2 * tm * tn * wb + 2 * tm * tn * 4 + (6 << 20)
    return pl.pallas_call(
        _proj_kernel,
        out_shape=(jax.ShapeDtypeStruct((m, n), w.dtype),
                   jax.ShapeDtypeStruct((m, LANES), F32),
                   jax.ShapeDtypeStruct((PST_ROWS, m), F32)),
        grid=(m // tm, n // tn),
        in_specs=[
            pl.BlockSpec((tm, d), lambda i, j: (i, 0)),
            pl.BlockSpec((1, d), lambda i, j: (0, 0)),
            pl.BlockSpec((None, d, tn), lambda i, j: (layer, 0, j)),
            pl.BlockSpec((1, tn), lambda i, j: (0, j)),
            pl.BlockSpec((None, d, LANES), lambda i, j: (layer, 0, 0)),
        ],
        out_specs=(pl.BlockSpec((tm, tn), lambda i, j: (i, j)),
                   pl.BlockSpec((tm, LANES), lambda i, j: (i, 0)),
                   pl.BlockSpec((PST_ROWS, tm), lambda i, j: (0, i))),
        scratch_shapes=[pltpu.VMEM((tm, d), w.dtype)],
        compiler_params=_cparams(("parallel", "arbitrary"), vmem),
        name="proj",
    )(x, g, w, cs, ws)


def _norm_matmul_kernel(x_ref, g_ref, w_ref, o_ref):
    h = _rmsnorm(x_ref[...], g_ref[...]).astype(w_ref.dtype)
    o_ref[...] = _dot(h, w_ref[...]).astype(o_ref.dtype)


def _norm_matmul(x, g, w, *, layer, tm, tn):
    m, d = x.shape
    n = w.shape[-1]
    wb = jnp.dtype(w.dtype).itemsize
    vmem = 3 * tm * d * 4 + 2 * d * tn * wb + 3 * tm * tn * 4 + (4 << 20)
    return pl.pallas_call(
        _norm_matmul_kernel,
        out_shape=jax.ShapeDtypeStruct((m, n), w.dtype),
        grid=(m // tm, n // tn),
        in_specs=[pl.BlockSpec((tm, d), lambda i, j: (i, 0)),
                  pl.BlockSpec((1, d), lambda i, j: (0, 0)),
                  pl.BlockSpec((None, d, tn), lambda i, j: (layer, 0, j))],
        out_specs=pl.BlockSpec((tm, tn), lambda i, j: (i, j)),
        compiler_params=_cparams(("parallel", "parallel"), vmem),
        name="mem_kv",
    )(x, g, w)


def _rope_kernel(pos_ref, ct_ref, st_ref):
    pos = pos_ref[...].astype(F32)
    lane = lax.broadcasted_iota(jnp.int32, (1, LANES), 1)
    half = MLA_ROPE // 2
    f = jnp.bitwise_and(lane, half - 1).astype(F32)
    inv_freq = jnp.power(jnp.float32(ROPE_THETA), -f / half)
    ang = pos * inv_freq
    valid = lane < MLA_ROPE
    ct_ref[...] = jnp.where(valid, jnp.cos(ang), 0.0)
    st_ref[...] = jnp.where(valid, jnp.sin(ang), 0.0)


def _rope_tables(pos_col, *, tm):
    m = pos_col.shape[0]
    return pl.pallas_call(
        _rope_kernel,
        out_shape=(jax.ShapeDtypeStruct((m, LANES), F32), jax.ShapeDtypeStruct((m, LANES), F32)),
        grid=(m // tm,),
        in_specs=[pl.BlockSpec((tm, 1), lambda i: (i, 0))],
        out_specs=(pl.BlockSpec((tm, LANES), lambda i: (i, 0)),
                   pl.BlockSpec((tm, LANES), lambda i: (i, 0))),
        compiler_params=_cparams(("parallel",), 16 << 20),
        name="rope_tables",
    )(pos_col)


def _mla_prep_kernel(cq_ref, ckv_ref, kr_ref, krrot_ref, ct_ref, st_ref, qn_ref, kvn_ref,
                     wa_ref, wb_ref, wkv_ref, q_ref, k_ref, v_ref):
    ct = ct_ref[...]
    st = st_ref[...]
    scale = (MLA_NOPE + MLA_ROPE) ** -0.5 * LOG2E

    ckv = ckv_ref[...].astype(F32)
    ckvn = _rmsnorm(ckv, kvn_ref[...]).astype(MXU_DTYPE)
    kv = _dot(ckvn, wkv_ref[...])
    kpe = kr_ref[...].astype(F32) * ct + krrot_ref[...].astype(F32) * st
    kpe = kpe.astype(k_ref.dtype)
    hw = MLA_HEADS * MLA_NOPE
    for h in range(MLA_HEADS):
        base = h * MLA_DK_PAD
        k_ref[:, base:base + MLA_NOPE] = kv[:, h * MLA_NOPE:(h + 1) * MLA_NOPE].astype(k_ref.dtype)
        k_ref[:, base + MLA_NOPE:base + MLA_DK_PAD] = kpe
    v_ref[...] = kv[:, hw:].astype(v_ref.dtype)

    cq = cq_ref[...].astype(F32)[:, :MLA_Q_RANK]
    cqn = _rmsnorm(cq, qn_ref[...]).astype(MXU_DTYPE)
    qa = _dot(cqn, wa_ref[...])
    qb = _dot(cqn, wb_ref[...])
    for h in range(MLA_HEADS):
        base = h * MLA_DK_PAD
        q_ref[:, base:base + MLA_NOPE] = (qa[:, base:base + MLA_NOPE] * scale).astype(q_ref.dtype)
        pe = qa[:, base + MLA_NOPE:base + MLA_DK_PAD] * ct + qb[:, h * LANES:(h + 1) * LANES] * st
        q_ref[:, base + MLA_NOPE:base + MLA_DK_PAD] = (pe * scale).astype(q_ref.dtype)


def _mla_prep(p, ct, st, qn, kvn, wa, wb, wkv, *, layer, tm):
    m = p.shape[0]
    qk_w = MLA_HEADS * MLA_DK_PAD
    v_w = MLA_HEADS * MLA_V
    full = lambda a: pl.BlockSpec(a.shape, lambda i: (0,) * a.ndim)
    of_layer = lambda a: pl.BlockSpec((None,) + a.shape[1:], lambda i: (layer,) + (0,) * (a.ndim - 1))
    return pl.pallas_call(
        _mla_prep_kernel,
        out_shape=(jax.ShapeDtypeStruct((m, qk_w), p.dtype),
                   jax.ShapeDtypeStruct((m, qk_w), p.dtype),
                   jax.ShapeDtypeStruct((m, v_w), p.dtype)),
        grid=(m // tm,),
        in_specs=[
            pl.BlockSpec((tm, 512), lambda i: (i, P_CQ // 512)),
            pl.BlockSpec((tm, 256), lambda i: (i, P_CKV // 256)),
            pl.BlockSpec((tm, 128), lambda i: (i, P_KR // 128)),
            pl.BlockSpec((tm, 128), lambda i: (i, P_KRROT // 128)),
            pl.BlockSpec((tm, LANES), lambda i: (i, 0)),
            pl.BlockSpec((tm, LANES), lambda i: (i, 0)),
            full(qn), full(kvn), of_layer(wa), of_layer(wb), of_layer(wkv),
        ],
        out_specs=(pl.BlockSpec((tm, qk_w), lambda i: (i, 0)),
                   pl.BlockSpec((tm, qk_w), lambda i: (i, 0)),
                   pl.BlockSpec((tm, v_w), lambda i: (i, 0))),
        compiler_params=_cparams(("parallel",), 40 << 20),
        name="mla_prep",
    )(p, p, p, p, ct, st, qn, kvn, wa, wb, wkv)


def _fox_cum_kernel(ft_ref, fb_ref, c_ref, *, seq):
    lf = _log_sigmoid(ft_ref[...] + fb_ref[...])
    r = lax.broadcasted_iota(jnp.int32, (LANES, LANES), 0)
    c = lax.broadcasted_iota(jnp.int32, (LANES, LANES), 1)
    upper = r <= c
    carry = jnp.zeros((SUBLANES, 1), F32)
    for blk in range(seq // LANES):
        sl = slice(blk * LANES, (blk + 1) * LANES)
        cs = _dot01_right(lf[:, sl], upper) + carry
        for h in range(FOX_HEADS):
            c_ref[h, :, sl] = cs[h:h + 1, :] * LOG2E
        carry = cs[:, LANES - 1:LANES]


def _fox_cum(pst, fb_col, *, batch, seq):
    return pl.pallas_call(
        functools.partial(_fox_cum_kernel, seq=seq),
        out_shape=jax.ShapeDtypeStruct((batch * FOX_HEADS, 1, seq), F32),
        grid=(batch,),
        in_specs=[pl.BlockSpec((SUBLANES, seq), lambda b: (PS_F // SUBLANES, b)),
                  pl.BlockSpec((SUBLANES, 1), lambda b: (0, 0))],
        out_specs=pl.BlockSpec((FOX_HEADS, 1, seq), lambda b: (b, 0, 0)),
        compiler_params=_cparams(("parallel",), 16 << 20),
        name="fox_cum",
    )(pst, fb_col)


def _flash_kernel(*refs, seq, tb, rmult, groups, dk, dv, has_bias):
    refs = list(refs)
    q_ref, k_ref, v_ref = refs[:3]
    c_ref = refs[3] if has_bias else None
    o_ref = refs[3 + int(has_bias)]
    rows_st = tb * rmult
    row = lax.broadcasted_iota(jnp.int32, (tb, tb), 0)
    col = lax.broadcasted_iota(jnp.int32, (tb, tb), 1)
    causal = row >= col

    def update(state, s, v_blk):
        m, l, acc = state
        m_new = jnp.maximum(m, jnp.max(s, axis=-1, keepdims=True))
        alpha = jnp.exp2(m - m_new)
        p = jnp.exp2(s - m_new)
        l = alpha * l + jnp.sum(p, axis=-1, keepdims=True)
        acc = alpha * acc + _dot(p.astype(MXU_DTYPE), v_blk)
        return m_new, l, acc

    def super_tile(pj, unused):
        r0 = pl.multiple_of(pj * rows_st, rows_st)
        qs = [q_ref[pl.ds(r0, rows_st), g * dk:(g + 1) * dk] for g in range(groups)]
        if has_bias:
            c0 = [jnp.max(c_ref[g, :, pl.ds(r0, tb)], axis=-1, keepdims=True) for g in range(groups)]

        def scores(q, g, start, width=tb):
            s = _dot_nt(q, k_ref[pl.ds(start, width), g * dk:(g + 1) * dk])
            if has_bias:
                s = s + (c0[g] - c_ref[g, :, pl.ds(start, width)])
            return s

        def v_block(g, start, width=tb):
            return v_ref[pl.ds(start, width), g * dv:(g + 1) * dv]

        def body(ki, carry):
            start = pl.multiple_of(ki * rows_st, rows_st)
            return tuple(update(carry[g], scores(qs[g], g, start, rows_st), v_block(g, start, rows_st))
                         for g in range(groups))

        init = tuple((jnp.full((rows_st, 1), -jnp.inf, F32), jnp.zeros((rows_st, 1), F32),
                      jnp.zeros((rows_st, dv), F32)) for _ in range(groups))
        carry = lax.fori_loop(0, pj, body, init)

        for g in range(groups):
            m, l, acc = carry[g]
            for a in range(rmult):
                rs = slice(a * tb, (a + 1) * tb)
                state = (m[rs], l[rs], acc[rs])
                for kb in range(a + 1):
                    start = pl.multiple_of(r0 + kb * tb, tb)
                    s = scores(qs[g][rs], g, start)
                    if kb == a:
                        s = jnp.where(causal, s, -jnp.inf)
                    state = update(state, s, v_block(g, start))
                o_ref[pl.ds(pl.multiple_of(r0 + a * tb, tb), tb), g * dv:(g + 1) * dv] = \
                    (state[2] / state[1]).astype(o_ref.dtype)
        return unused

    lax.fori_loop(0, seq // rows_st, super_tile, 0)


def _flash(q_arr, k_arr, v_arr, c_arr, *, batch, seq, heads, groups, dk, dv,
           q_off, k_off, v_off, tb, rmult):
    rows_st = tb * rmult
    has_bias = c_arr is not None
    gk, gv = groups * dk, groups * dv
    in_specs = [pl.BlockSpec((seq, gk), lambda b, h: (b, q_off // gk + h)),
                pl.BlockSpec((seq, gk), lambda b, h: (b, k_off // gk + h)),
                pl.BlockSpec((seq, gv), lambda b, h: (b, v_off // gv + h))]
    args = [q_arr, k_arr, v_arr]
    if has_bias:
        in_specs.append(pl.BlockSpec((groups, 1, seq), lambda b, h: (b * (heads // groups) + h, 0, 0)))
        args.append(c_arr)
    vmem = 2 * seq * (2 * gk + 2 * gv) * 2 + groups * (3 * rows_st * rows_st * 4 + 4 * rows_st * dv * 4) + (8 << 20)
    return pl.pallas_call(
        functools.partial(_flash_kernel, seq=seq, tb=tb, rmult=rmult, groups=groups, dk=dk, dv=dv,
                          has_bias=has_bias),
        out_shape=jax.ShapeDtypeStruct((batch * seq, heads * dv), q_arr.dtype),
        grid=(batch, heads // groups),
        in_specs=in_specs,
        out_specs=pl.BlockSpec((seq, gv), lambda b, h: (b, h)),
        compiler_params=_cparams(("parallel", "parallel"), vmem),
        name="flash_fox" if has_bias else "flash_mla",
    )(*args)


def _gla_kernel(q_ref, k_ref, v_ref, r_ref, ps_ref, wg2_ref, bg2_ref, nw_ref, o_ref, st_ref, *, tm):
    @pl.when(pl.program_id(1) == 0)
    def _():
        st_ref[...] = jnp.zeros_like(st_ref)

    glin = _dot(ps_ref[...].astype(MXU_DTYPE), wg2_ref[...]) + bg2_ref[...]
    gl = _log_sigmoid(glin) / GLA_GATE_NORM
    tril, _, same = _chunk_masks(tm, GLA_CHUNK)
    gcum = _dot01_left(tril, gl)
    gend = _dot01_left(same, gl)
    q = q_ref[...].astype(F32)
    k = k_ref[...].astype(F32)
    qd = (q * jnp.exp(gcum)).astype(MXU_DTYPE)
    kd = (k * jnp.exp(-gcum)).astype(MXU_DTYPE)
    ku = (k * jnp.exp(gend - gcum)).astype(MXU_DTYPE)
    dec = jnp.exp(gend)
    cr = lax.broadcasted_iota(jnp.int32, (GLA_CHUNK, GLA_CHUNK), 0)
    cc = lax.broadcasted_iota(jnp.int32, (GLA_CHUNK, GLA_CHUNK), 1)
    causal = cr >= cc
    nw = nw_ref[...]

    for c in range(tm // GLA_CHUNK):
        rows = slice(c * GLA_CHUNK, (c + 1) * GLA_CHUNK)
        for h in range(GLA_HEADS):
            kc = slice(h * GLA_DK, (h + 1) * GLA_DK)
            vc = slice(h * GLA_DV, (h + 1) * GLA_DV)
            qd_h, kd_h, ku_h = qd[rows, kc], kd[rows, kc], ku[rows, kc]
            v_h = v_ref[rows, vc]
            att = jnp.where(causal, _dot_nt(qd_h, kd_h), 0.0)
            state_t = st_ref[:, kc]
            o = _dot(att.astype(MXU_DTYPE), v_h) + _dot_nt(qd_h, state_t.astype(MXU_DTYPE))
            st_ref[:, kc] = state_t * dec[c * GLA_CHUNK:c * GLA_CHUNK + 1, kc] + _dot_tn(v_h, ku_h)
            o = _rmsnorm(o, nw)
            o_ref[rows, vc] = (o * _silu(r_ref[rows, vc].astype(F32))).astype(o_ref.dtype)


def _gla(p, ps, wg2, bg2, nw, *, batch, seq, tm):
    nt = seq // tm
    full = lambda a: pl.BlockSpec(a.shape, lambda b, i: (0,) * a.ndim)
    row = lambda b, i: b * nt + i
    return pl.pallas_call(
        functools.partial(_gla_kernel, tm=tm),
        out_shape=jax.ShapeDtypeStruct((p.shape[0], GLA_HEADS * GLA_DV), p.dtype),
        grid=(batch, nt),
        in_specs=[
            pl.BlockSpec((tm, 256), lambda b, i: (row(b, i), P_GLA_Q // 256)),
            pl.BlockSpec((tm, 256), lambda b, i: (row(b, i), P_GLA_K // 256)),
            pl.BlockSpec((tm, 512), lambda b, i: (row(b, i), P_GLA_V // 512)),
            pl.BlockSpec((tm, 512), lambda b, i: (row(b, i), P_GLA_R // 512)),
            pl.BlockSpec((tm, LANES), lambda b, i: (row(b, i), 0)),
            full(wg2), full(bg2), full(nw),
        ],
        out_specs=pl.BlockSpec((tm, GLA_HEADS * GLA_DV), lambda b, i: (row(b, i), 0)),
        scratch_shapes=[pltpu.VMEM((GLA_DV, GLA_HEADS * GLA_DK), F32)],
        compiler_params=_cparams(("parallel", "arbitrary"), 32 << 20),
        name="gla",
    )(p, p, p, p, ps, wg2, bg2, nw)


def _ssd_kernel(z_ref, x_ref, b_ref, c_ref, ps_ref, pst_ref, cw_ref, cb_ref, dtb_row_ref,
                dtb_col_ref, alog_row_ref, alog_col_ref, dskip_ref, nw_ref, exp_ref,
                o_ref, st_ref, tail_ref, *, tm):
    @pl.when(pl.program_id(1) == 0)
    def _():
        st_ref[...] = jnp.zeros_like(st_ref)
        tail_ref[...] = jnp.zeros_like(tail_ref)

    inner = SSD_INNER
    gw = inner // SSD_GROUPS
    xbc = jnp.concatenate([x_ref[...], b_ref[...], c_ref[...]], axis=-1).astype(F32)
    ext = jnp.concatenate([tail_ref[...], xbc], axis=0)
    tail_ref[...] = xbc[tm - SUBLANES:, :]
    conv = cb_ref[...]
    for j in range(SSD_CONV):
        shift = SSD_CONV - 1 - j
        src = ext if shift == 0 else pltpu.roll(ext, shift, axis=0)
        conv = conv + cw_ref[j:j + 1, :] * src[SUBLANES:, :]
    xc = _silu(conv)
    xs = xc[:, :inner]
    bm = xc[:, inner:inner + SSD_GROUPS * SSD_STATE].astype(MXU_DTYPE)
    cm = xc[:, inner + SSD_GROUPS * SSD_STATE:].astype(MXU_DTYPE)

    dt_exp = _dot01_right(_softplus(ps_ref[...] + dtb_row_ref[...]), exp_ref[...])
    a_row = -jnp.exp(alog_row_ref[...])
    dta = dt_exp * a_row
    tril, triu, same = _chunk_masks(tm, SSD_CHUNK)
    acum = _dot01_left(tril, dta)
    aend = _dot01_left(same, dta)
    dta_t = _softplus(pst_ref[...] + dtb_col_ref[...]) * (-jnp.exp(alog_col_ref[...]))
    acum_t = _dot01_right(dta_t, triu)
    x_dt = xs * dt_exp
    xw = (x_dt * jnp.exp(aend - acum)).astype(MXU_DTYPE)
    x_dt_m = x_dt.astype(MXU_DTYPE)
    e_acum = jnp.exp(acum)
    e_aend = jnp.exp(aend)
    cr = lax.broadcasted_iota(jnp.int32, (SSD_CHUNK, SSD_CHUNK), 0)
    cc = lax.broadcasted_iota(jnp.int32, (SSD_CHUNK, SSD_CHUNK), 1)
    causal = cr >= cc
    dskip = dskip_ref[...]
    nw = nw_ref[...]
    hg = SSD_HEADS // SSD_GROUPS

    for c in range(tm // SSD_CHUNK):
        r0 = c * SSD_CHUNK
        rows = slice(r0, r0 + SSD_CHUNK)
        for g in range(SSD_GROUPS):
            gl = slice(g * gw, (g + 1) * gw)
            nl = slice(g * SSD_STATE, (g + 1) * SSD_STATE)
            b_g, c_g = bm[rows, nl], cm[rows, nl]
            cb = _dot_nt(c_g, b_g)
            parts = []
            for hh in range(hg):
                h = g * hg + hh
                lane0 = h * SSD_HEADDIM
                colv = jnp.broadcast_to(acum[rows, lane0:lane0 + 1], (SSD_CHUNK, SSD_CHUNK))
                rowv = acum_t[h:h + 1, r0:r0 + SSD_CHUNK]
                decay = jnp.where(causal, jnp.exp(colv - rowv), 0.0)
                mmat = (cb * decay).astype(MXU_DTYPE)
                parts.append(_dot(mmat, x_dt_m[rows, lane0:lane0 + SSD_HEADDIM]))
            y_diag = jnp.concatenate(parts, axis=-1)
            state_t = st_ref[:, gl]
            y_off = _dot(c_g, state_t.astype(MXU_DTYPE)) * e_acum[rows, gl]
            st_ref[:, gl] = state_t * e_aend[r0:r0 + 1, gl] + _dot_tn(b_g, xw[rows, gl])
            y = y_diag + y_off + dskip[:, gl] * xs[rows, gl]
            y = y * _silu(z_ref[rows, gl].astype(F32))
            o_ref[rows, gl] = _rmsnorm(y, nw[:, gl]).astype(o_ref.dtype)


def _ssd(p, ps, pst, cw, cb, dtb_row, dtb_col, alog_row, alog_col, dskip, nw, expand, *, batch, seq, tm):
    nt = seq // tm
    full = lambda a: pl.BlockSpec(a.shape, lambda b, i: (0,) * a.ndim)
    row = lambda b, i: b * nt + i
    conv_dim = SSD_INNER + 2 * SSD_GROUPS * SSD_STATE
    return pl.pallas_call(
        functools.partial(_ssd_kernel, tm=tm),
        out_shape=jax.ShapeDtypeStruct((p.shape[0], SSD_INNER), p.dtype),
        grid=(batch, nt),
        in_specs=[
            pl.BlockSpec((tm, 512), lambda b, i: (row(b, i), P_SSD_Z // 512)),
            pl.BlockSpec((tm, 512), lambda b, i: (row(b, i), P_SSD_X // 512)),
            pl.BlockSpec((tm, 256), lambda b, i: (row(b, i), P_SSD_B // 256)),
            pl.BlockSpec((tm, 256), lambda b, i: (row(b, i), P_SSD_C // 256)),
            pl.BlockSpec((tm, LANES), lambda b, i: (row(b, i), 0)),
            pl.BlockSpec((SUBLANES, tm), lambda b, i: (PS_DT // SUBLANES, row(b, i))),
            full(cw), full(cb), full(dtb_row), full(dtb_col), full(alog_row), full(alog_col),
            full(dskip), full(nw), full(expand),
        ],
        out_specs=pl.BlockSpec((tm, SSD_INNER), lambda b, i: (row(b, i), 0)),
        scratch_shapes=[pltpu.VMEM((SSD_STATE, SSD_INNER), F32),
                        pltpu.VMEM((SUBLANES, conv_dim), F32)],
        compiler_params=_cparams(("parallel", "arbitrary"), 40 << 20),
        name="ssd",
    )(p, p, p, p, ps, pst, cw, cb, dtb_row, dtb_col, alog_row, alog_col, dskip, nw, expand)


def _merge_kernel(x_ref, g_ref, y_ref, wsq_ref, wbr_ref, o_ref, h_ref, acc_ref, *, tc):
    j = pl.program_id(1)
    d = x_ref.shape[-1]

    @pl.when(j == 0)
    def _():
        _norm_rows(x_ref, g_ref, h_ref)
        acc_ref[...] = jnp.zeros_like(acc_ref)

    @pl.when(j < N_BRANCH)
    def _():
        h = h_ref[...]
        y = y_ref[...]
        for c in range(d // tc):
            cols = slice(c * tc, (c + 1) * tc)
            gate = jax.nn.sigmoid(_dot(h, wsq_ref[:, cols]))
            acc_ref[:, cols] += gate * _dot(y, wbr_ref[:, cols])

    @pl.when(j == N_BRANCH)
    def _():
        merged = acc_ref[...].astype(MXU_DTYPE)
        for c in range(d // tc):
            cols = slice(c * tc, (c + 1) * tc)
            o_ref[:, cols] = x_ref[:, cols] + _dot(merged, wsq_ref[:, cols])


def _merge(x, g, ys, wsq, wbr, *, layer, tm, tc):
    m, d = x.shape
    bw = ys.shape[-1]
    wbytes = jnp.dtype(wsq.dtype).itemsize
    vmem = 4 * tm * d * 4 + tm * d * wbytes + tm * d * 4 + 2 * d * d * wbytes \
        + 2 * bw * d * wbytes + 2 * tm * bw * wbytes + 4 * tm * tc * 4 + (4 << 20)
    last = N_BRANCH - 1
    return pl.pallas_call(
        functools.partial(_merge_kernel, tc=tc),
        out_shape=jax.ShapeDtypeStruct((m, d), F32),
        grid=(m // tm, N_BRANCH + 1),
        in_specs=[
            pl.BlockSpec((tm, d), lambda i, j: (i, 0)),
            pl.BlockSpec((1, d), lambda i, j: (0, 0)),
            pl.BlockSpec((None, tm, bw), lambda i, j: (jnp.minimum(j, last), i, 0)),
            pl.BlockSpec((None, d, d), lambda i, j: (layer, 0, j)),
            pl.BlockSpec((None, None, bw, d), lambda i, j: (layer, jnp.minimum(j, last), 0, 0)),
        ],
        out_specs=pl.BlockSpec((tm, d), lambda i, j: (i, 0)),
        scratch_shapes=[pltpu.VMEM((tm, d), wsq.dtype), pltpu.VMEM((tm, d), F32)],
        compiler_params=_cparams(("parallel", "arbitrary"), vmem),
        name="merge",
    )(x, g, ys, wsq, wbr)


def _xattn_kernel(x_ref, g_ref, wq_ref, k_ref, v_ref, wo_ref, o_ref, h_ref, oc_ref, *, tc):
    _norm_rows(x_ref, g_ref, h_ref)
    h = h_ref[...]
    dh = XATTN_DH
    for hd in range(XATTN_HEADS):
        cols = slice(hd * dh, (hd + 1) * dh)
        q = _dot(h, wq_ref[:, cols]).astype(MXU_DTYPE)
        s = _dot_nt(q, k_ref[:, cols]) * (dh ** -0.5 * LOG2E)
        p = jnp.exp2(s - jnp.max(s, axis=-1, keepdims=True))
        p = p / jnp.sum(p, axis=-1, keepdims=True)
        oc_ref[:, cols] = _dot(p.astype(MXU_DTYPE), v_ref[:, cols]).astype(oc_ref.dtype)
    oc = oc_ref[...]
    d = x_ref.shape[-1]
    for c in range(d // tc):
        cols = slice(c * tc, (c + 1) * tc)
        o_ref[:, cols] = x_ref[:, cols] + _dot(oc, wo_ref[:, cols])


def _xattn(x, g, wq, kv, wo, *, layer, seq, n_mem, tm, tc):
    m, d = x.shape
    wbytes = jnp.dtype(wq.dtype).itemsize
    vmem = 4 * tm * d * 4 + 2 * tm * d * wbytes + 2 * d * d * wbytes + 4 * n_mem * d * wbytes \
        + 3 * tm * XATTN_DH * 4 + 3 * tm * n_mem * 4 + 2 * tm * tc * 4 + (6 << 20)
    tiles_per_seq = seq // tm
    return pl.pallas_call(
        functools.partial(_xattn_kernel, tc=tc),
        out_shape=jax.ShapeDtypeStruct((m, d), F32),
        grid=(m // tm,),
        in_specs=[
            pl.BlockSpec((tm, d), lambda i: (i, 0)),
            pl.BlockSpec((1, d), lambda i: (0, 0)),
            pl.BlockSpec((None, d, d), lambda i: (layer, 0, 0), pipeline_mode=pl.Buffered(1)),
            pl.BlockSpec((n_mem, d), lambda i: (i // tiles_per_seq, 0)),
            pl.BlockSpec((n_mem, d), lambda i: (i // tiles_per_seq, 1)),
            pl.BlockSpec((None, d, d), lambda i: (layer, 0, 0), pipeline_mode=pl.Buffered(1)),
        ],
        out_specs=pl.BlockSpec((tm, d), lambda i: (i, 0)),
        scratch_shapes=[pltpu.VMEM((tm, d), wq.dtype), pltpu.VMEM((tm, d), wq.dtype)],
        compiler_params=_cparams(("parallel",), vmem),
        name="xattn",
    )(x, g, wq, kv, kv, wo)


_IN_SIZES = (N_BRANCH * D_MODEL, MLA_Q_RANK, MLA_KV_RANK, MLA_ROPE, 3 * FOX_HEADS * FOX_DH, FOX_HEADS,
             GLA_HEADS * GLA_DK, GLA_HEADS * GLA_DK, GLA_HEADS * GLA_DV, GLA_GATE_RANK,
             GLA_HEADS * GLA_DV, SSD_INNER, SSD_INNER + 2 * SSD_GROUPS * SSD_STATE, SSD_HEADS)


def _rot_cols(w):
    half = w.shape[-1] // 2
    return jnp.concatenate([-w[..., half:], w[..., :half]], axis=-1)


def _prep_w_in(w_in):
    nl, d = w_in.shape[:2]
    offs = np.cumsum((0,) + _IN_SIZES)
    seg = lambda i: w_in[:, :, offs[i]:offs[i + 1]]
    (_, cq, ckv, kr, fox, fox_f, gq, gk, gv, glr, gr, sz, sxbc, sdt) = [seg(i) for i in range(len(_IN_SIZES))]
    z = lambda n: jnp.zeros((nl, d, n), w_in.dtype)
    w_p = jnp.concatenate([
        cq, z(512 - MLA_Q_RANK), ckv, kr, z(LANES - MLA_ROPE), _rot_cols(kr), z(LANES - MLA_ROPE),
        fox, gq, gk, gv, gr, sz, sxbc], axis=2)
    assert w_p.shape[2] == P_WIDTH
    cs = np.ones((1, P_WIDTH), np.float32)
    cs[:, P_FOX_Q:P_FOX_K] = FOX_DH ** -0.5 * LOG2E
    cs[:, P_GLA_Q:P_GLA_K] = GLA_DK ** -0.5
    w_s = jnp.concatenate([glr, sdt, fox_f, z(LANES - GLA_GATE_RANK - SSD_HEADS - FOX_HEADS)], axis=2)
    return w_p.astype(MXU_DTYPE), jnp.asarray(cs), w_s.astype(MXU_DTYPE)


def _prep_mla(w_uq, w_ukv):
    nl, r = w_uq.shape[:2]
    wq = w_uq.reshape(nl, r, MLA_HEADS, MLA_NOPE + MLA_ROPE)
    nope, pe = wq[..., :MLA_NOPE], wq[..., MLA_NOPE:]
    zpad = jnp.zeros((nl, r, MLA_HEADS, MLA_DK_PAD - MLA_NOPE - MLA_ROPE), w_uq.dtype)
    wa = jnp.concatenate([nope, pe, zpad], axis=-1).reshape(nl, r, MLA_HEADS * MLA_DK_PAD)
    wb = jnp.concatenate([_rot_cols(pe), zpad], axis=-1).reshape(nl, r, MLA_HEADS * LANES)
    rk = w_ukv.shape[1]
    wkv = w_ukv.reshape(nl, rk, MLA_HEADS, MLA_NOPE + MLA_V)
    wkv = jnp.concatenate([wkv[..., :MLA_NOPE].reshape(nl, rk, -1), wkv[..., MLA_NOPE:].reshape(nl, rk, -1)],
                          axis=2)
    return wa.astype(MXU_DTYPE), wb.astype(MXU_DTYPE), wkv.astype(MXU_DTYPE)


def _tiles(m, seq):
    mid = min(512, seq)
    return dict(ffn_tm=mid, ffn_tf=512, proj_tm=mid, proj_tn=2816, prep_tm=mid,
                flash_tb=mid, flash_rmult=min(2, seq // mid), gla_tm=min(256, seq), ssd_tm=min(256, seq), merge_tm=mid, merge_tc=512,
                xattn_tm=mid, xattn_tc=512, rope_tm=min(1024, m))


@jax.jit
def kernel(x, mem, positions, ffn1_norm, ffn1_w_gate, ffn1_w_up, ffn1_w_down, mix_norm, w_in, mla_q_norm, mla_kv_norm, mla_w_uq, mla_w_ukv, fox_f_bias, gla_w_g2, gla_b_g2, gla_norm, ssd_conv_w, ssd_conv_b, ssd_dt_bias, ssd_a_log, ssd_d, ssd_norm, w_branch, w_out, xattn_norm, mem_norm, xattn_w_q, xattn_w_k, xattn_w_v, xattn_w_o, ffn2_norm, ffn2_w_gate, ffn2_w_up, ffn2_w_down, final_norm):
    batch, seq, d = x.shape
    n_mem = mem.shape[1]
    m = batch * seq
    t = _tiles(m, seq)
    depth = w_in.shape[0]
    cast = lambda a: a.astype(MXU_DTYPE)
    row = lambda a: a.reshape(1, -1).astype(F32)
    col8 = lambda v: jnp.zeros((SUBLANES, 1), F32).at[:v.shape[0], 0].set(v.astype(F32))

    xs = x.reshape(m, d)
    mem2 = mem.reshape(batch * n_mem, d)
    ct, st = _rope_tables(positions.reshape(m, 1), tm=t["rope_tm"])

    expand = np.zeros((LANES, SSD_INNER), np.float32)
    for h in range(SSD_HEADS):
        expand[PS_DT + h, h * SSD_HEADDIM:(h + 1) * SSD_HEADDIM] = 1.0
    expand = jnp.asarray(expand, MXU_DTYPE)
    rep = lambda v: jnp.repeat(v.astype(F32), SSD_HEADDIM).reshape(1, SSD_INNER)
    ffn = functools.partial(_ffn, tm=t["ffn_tm"], tf=t["ffn_tf"])
    flash = functools.partial(_flash, batch=batch, seq=seq, groups=2, tb=t["flash_tb"], rmult=t["flash_rmult"])

    f1g, f1u, f1d = cast(ffn1_w_gate), cast(ffn1_w_up), cast(ffn1_w_down)
    f2g, f2u, f2d = cast(ffn2_w_gate), cast(ffn2_w_up), cast(ffn2_w_down)
    w_p, cs, w_s = _prep_w_in(w_in)
    wa, wb, wkv = _prep_mla(mla_w_uq, mla_w_ukv)
    wsq = cast(jnp.concatenate([w_in[:, :, :N_BRANCH * d], w_out], axis=2))
    wbr = cast(w_branch)
    w_kv = cast(jnp.concatenate([xattn_w_k, xattn_w_v], axis=2))
    wq_x, wo_x = cast(xattn_w_q), cast(xattn_w_o)

    for l in range(depth):
        xs = ffn(xs, row(ffn1_norm[l]), f1g, f1u, f1d, row(final_norm), layer=l, final_norm=False)

        p, ps, pst = _proj(xs, row(mix_norm[l]), w_p, cs, w_s, layer=l, tm=t["proj_tm"], tn=t["proj_tn"])

        q_mla, k_mla, v_mla = _mla_prep(p, ct, st, row(mla_q_norm[l]), row(mla_kv_norm[l]), wa, wb, wkv,
                                        layer=l, tm=t["prep_tm"])
        y_mla = flash(q_mla, k_mla, v_mla, None, heads=MLA_HEADS,
                      dk=MLA_DK_PAD, dv=MLA_V, q_off=0, k_off=0, v_off=0)

        c_fox = _fox_cum(pst, col8(fox_f_bias[l]), batch=batch, seq=seq)
        y_fox = flash(p, p, p, c_fox, heads=FOX_HEADS,
                      dk=FOX_DH, dv=FOX_DH, q_off=P_FOX_Q, k_off=P_FOX_K, v_off=P_FOX_V)

        wg2 = jnp.zeros((LANES, GLA_HEADS * GLA_DK), F32).at[PS_GLR:PS_GLR + GLA_GATE_RANK].set(gla_w_g2[l])
        y_gla = _gla(p, ps, cast(wg2), row(gla_b_g2[l]), row(gla_norm[l]), batch=batch, seq=seq, tm=t["gla_tm"])

        dtb_row = jnp.zeros((1, LANES), F32).at[0, PS_DT:PS_DT + SSD_HEADS].set(ssd_dt_bias[l])
        y_ssd = _ssd(p, ps, pst, ssd_conv_w[l].astype(F32), row(ssd_conv_b[l]), dtb_row, col8(ssd_dt_bias[l]),
                     rep(ssd_a_log[l]), col8(ssd_a_log[l]), rep(ssd_d[l]), row(ssd_norm[l]), expand,
                     batch=batch, seq=seq, tm=t["ssd_tm"])

        ys = jnp.stack([y_mla, y_fox, y_gla, y_ssd])
        xs = _merge(xs, row(mix_norm[l]), ys, wsq, wbr, layer=l, tm=t["merge_tm"], tc=t["merge_tc"])

        kv = _norm_matmul(mem2, row(mem_norm[l]), w_kv, layer=l, tm=min(512, batch * n_mem), tn=1024)
        xs = _xattn(xs, row(xattn_norm[l]), wq_x, kv, wo_x, layer=l,
                    seq=seq, n_mem=n_mem, tm=t["xattn_tm"], tc=t["xattn_tc"])

        xs = ffn(xs, row(ffn2_norm[l]), f2g, f2u, f2d, row(final_norm), layer=l, final_norm=(l == depth - 1))

    return xs.reshape(batch, seq, d)
```

```python
import functools

import numpy as np
import jax
import jax.numpy as jnp
from jax import lax
from jax.experimental import pallas as pl
from jax.experimental.pallas import tpu as pltpu

D_MODEL = 2048
DEPTH = 4
EPS = 1e-6
N_BRANCH = 4
BRANCH_WIDTH = 512
MLA_HEADS, MLA_NOPE, MLA_ROPE, MLA_V = 4, 128, 64, 128
MLA_Q_RANK, MLA_KV_RANK = 384, 256
MLA_DK_PAD = 256
ROPE_THETA = 10000.0
FOX_HEADS, FOX_DH = 4, 128
GLA_HEADS, GLA_DK, GLA_DV = 4, 64, 128
GLA_GATE_RANK, GLA_GATE_NORM, GLA_CHUNK = 16, 16.0, 64
SSD_HEADS, SSD_HEADDIM, SSD_GROUPS, SSD_STATE, SSD_CONV, SSD_CHUNK = 8, 64, 2, 128, 4, 128
SSD_INNER = SSD_HEADS * SSD_HEADDIM
XATTN_HEADS = 4
XATTN_DH = D_MODEL // XATTN_HEADS
D_FF = 5632

LANES = 128
SUBLANES = 8
V7X_VMEM_LIMIT_BYTES = 60 * 1024 * 1024

MXU_DTYPE = jnp.bfloat16
FFN_SLOTS = 3
F32 = jnp.float32
LOG2E = 1.4426950408889634

P_CQ, P_CKV, P_KR, P_KRROT = 0, 512, 768, 896
P_FOX_Q, P_FOX_K, P_FOX_V = 1024, 1536, 2048
P_GLA_Q, P_GLA_K, P_GLA_V, P_GLA_R = 2560, 2816, 3072, 3584
P_SSD_Z, P_SSD_X, P_SSD_B, P_SSD_C = 4096, 4608, 5120, 5376
P_WIDTH = 5632
PS_GLR, PS_DT, PS_F = 0, 16, 24
PST_ROWS = 32


def _cparams(sem, vmem_bytes):
    return pltpu.CompilerParams(dimension_semantics=sem,
                                vmem_limit_bytes=int(min(vmem_bytes, V7X_VMEM_LIMIT_BYTES)))


def _dot(a, b):
    return jnp.dot(a, b, preferred_element_type=F32)


def _dot_nt(a, b):
    return lax.dot_general(a, b, (((1,), (1,)), ((), ())), preferred_element_type=F32)


def _dot_tn(a, b):
    return lax.dot_general(a, b, (((0,), (0,)), ((), ())), preferred_element_type=F32)


def _split3(x):
    hi = x.astype(MXU_DTYPE)
    r1 = x - hi.astype(F32)
    mid = r1.astype(MXU_DTYPE)
    lo = (r1 - mid.astype(F32)).astype(MXU_DTYPE)
    return hi, mid, lo


def _dot01_right(x, m01):
    if MXU_DTYPE == F32:
        return _dot(x, m01.astype(F32))
    m = m01.astype(MXU_DTYPE)
    hi, mid, lo = _split3(x)
    return _dot(hi, m) + _dot(mid, m) + _dot(lo, m)


def _dot01_left(m01, x):
    if MXU_DTYPE == F32:
        return _dot(m01.astype(F32), x)
    m = m01.astype(MXU_DTYPE)
    hi, mid, lo = _split3(x)
    return _dot(m, hi) + _dot(m, mid) + _dot(m, lo)


def _rmsnorm(x, g):
    ms = jnp.mean(x * x, axis=-1, keepdims=True)
    return x * lax.rsqrt(ms + EPS) * g


def _log_sigmoid(x):
    return jnp.minimum(x, 0.0) - jnp.log1p(jnp.exp(-jnp.abs(x)))


def _softplus(x):
    return jnp.maximum(x, 0.0) + jnp.log1p(jnp.exp(-jnp.abs(x)))


def _silu(x):
    return x * jax.nn.sigmoid(x)


def _chunk_masks(n, chunk):
    shift = chunk.bit_length() - 1
    assert 1 << shift == chunk
    r = lax.broadcasted_iota(jnp.int32, (n, n), 0)
    c = lax.broadcasted_iota(jnp.int32, (n, n), 1)
    same = jnp.right_shift(r, shift) == jnp.right_shift(c, shift)
    return jnp.logical_and(same, r >= c), jnp.logical_and(same, r <= c), same


def _norm_rows(x_ref, g_ref, h_ref, copy_ref=None, *, rows=256):
    tm = x_ref.shape[0]
    rows = min(rows, tm)
    g = g_ref[...]
    for r in range(tm // rows):
        sl = slice(r * rows, (r + 1) * rows)
        x = x_ref[sl, :]
        h_ref[sl, :] = _rmsnorm(x, g).astype(h_ref.dtype)
        if copy_ref is not None:
            copy_ref[sl, :] = x


def _ffn_kernel(x_ref, g_ref, wg_hbm, wu_hbm, wd_hbm, fg_ref, o_ref, h_ref, wg_buf, wu_buf, wd_buf, sem,
                *, final_norm, layer, tf, n_chunks):
    i = pl.program_id(0)

    def chunk_copies(j, slot):
        cols = pl.ds(j * tf, tf)
        return (pltpu.make_async_copy(wg_hbm.at[layer, :, cols], wg_buf.at[slot], sem.at[0, slot]),
                pltpu.make_async_copy(wu_hbm.at[layer, :, cols], wu_buf.at[slot], sem.at[1, slot]),
                pltpu.make_async_copy(wd_hbm.at[layer, cols, :], wd_buf.at[slot], sem.at[2, slot]))

    def start(j, slot):
        for cp in chunk_copies(j, slot):
            cp.start()

    @pl.when(i == 0)
    def _():
        start(0, 0)

    _norm_rows(x_ref, g_ref, h_ref, o_ref)
    h = h_ref[...]
    for j in range(n_chunks):
        slot = j % FFN_SLOTS
        for cp in chunk_copies(j, slot):
            cp.wait()
        if j + 1 < n_chunks:
            start(j + 1, (j + 1) % FFN_SLOTS)
        else:
            @pl.when(i + 1 < pl.num_programs(0))
            def _():
                start(0, 0)
        a = _dot(h, wg_buf[slot])
        u = _dot(h, wu_buf[slot])
        t = (_silu(a) * (0.5 * u)).astype(MXU_DTYPE)
        o_ref[...] += _dot(t, wd_buf[slot])

    if final_norm:
        _norm_rows(o_ref, fg_ref, o_ref)


def _ffn(x, g, wg, wu, wd, fg, *, layer, final_norm, tm, tf):
    m, d = x.shape
    f = wg.shape[-1]
    n_chunks = f // tf
    assert n_chunks * tf == f and n_chunks > FFN_SLOTS and (n_chunks - 1) % FFN_SLOTS != 0
    wb = jnp.dtype(wg.dtype).itemsize
    vmem = 4 * tm * d * 4 + tm * d * wb + FFN_SLOTS * 3 * d * tf * wb + 4 * tm * tf * 4 + tm * d * 4 + (4 << 20)
    return pl.pallas_call(
        functools.partial(_ffn_kernel, final_norm=final_norm, layer=layer, tf=tf, n_chunks=n_chunks),
        out_shape=jax.ShapeDtypeStruct((m, d), F32),
        grid=(m // tm,),
        in_specs=[
            pl.BlockSpec((tm, d), lambda i: (i, 0)),
            pl.BlockSpec((1, d), lambda i: (0, 0)),
            pl.BlockSpec(memory_space=pl.ANY),
            pl.BlockSpec(memory_space=pl.ANY),
            pl.BlockSpec(memory_space=pl.ANY),
            pl.BlockSpec((1, d), lambda i: (0, 0)),
        ],
        out_specs=pl.BlockSpec((tm, d), lambda i: (i, 0)),
        scratch_shapes=[pltpu.VMEM((tm, d), wg.dtype),
                        pltpu.VMEM((FFN_SLOTS, d, tf), wg.dtype),
                        pltpu.VMEM((FFN_SLOTS, d, tf), wg.dtype),
                        pltpu.VMEM((FFN_SLOTS, tf, d), wg.dtype),
                        pltpu.SemaphoreType.DMA((3, FFN_SLOTS))],
        compiler_params=_cparams(("arbitrary",), vmem),
        name="ffn",
    )(x, g, wg, wu, wd, fg)


def _proj_kernel(x_ref, g_ref, w_ref, cs_ref, ws_ref, p_ref, ps_ref, pst_ref, h_ref):
    j = pl.program_id(1)

    @pl.when(j == 0)
    def _():
        _norm_rows(x_ref, g_ref, h_ref)
        ps = _dot(h_ref[...], ws_ref[...])
        ps_ref[...] = ps
        pst_ref[...] = jnp.transpose(ps)[:PST_ROWS, :]

    p_ref[...] = (_dot(h_ref[...], w_ref[...]) * cs_ref[...]).astype(p_ref.dtype)


def _proj(x, g, w, cs, ws, *, layer, tm, tn):
    m, d = x.shape
    n = w.shape[-1]
    wb = jnp.dtype(w.dtype).itemsize
    vmem = 2 * tm * d * 4 + tm * d * wb + 2 * d * tn * wb + 2 * tm * tn * wb + 2 * tm * tn * 4 + (6 << 20)
    return pl.pallas_call(
        _proj_kernel,
        out_shape=(jax.ShapeDtypeStruct((m, n), w.dtype),
                   jax.ShapeDtypeStruct((m, LANES), F32),
                   jax.ShapeDtypeStruct((PST_ROWS, m), F32)),
        grid=(m // tm, n // tn),
        in_specs=[
            pl.BlockSpec((tm, d), lambda i, j: (i, 0)),
            pl.BlockSpec((1, d), lambda i, j: (0, 0)),
            pl.BlockSpec((None, d, tn), lambda i, j: (layer, 0, j)),
            pl.BlockSpec((1, tn), lambda i, j: (0, j)),
            pl.BlockSpec((None, d, LANES), lambda i, j: (layer, 0, 0)),
        ],
        out_specs=(pl.BlockSpec((tm, tn), lambda i, j: (i, j)),
                   pl.BlockSpec((tm, LANES), lambda i, j: (i, 0)),
                   pl.BlockSpec((PST_ROWS, tm), lambda i, j: (0, i))),
        scratch_shapes=[pltpu.VMEM((tm, d), w.dtype)],
        compiler_params=_cparams(("parallel", "arbitrary"), vmem),
        name="proj",
    )(x, g, w, cs, ws)


def _norm_matmul_kernel(x_ref, g_ref, w_ref, o_ref):
    h = _rmsnorm(x_ref[...], g_ref[...]).astype(w_ref.dtype)
    o_ref[...] = _dot(h, w_ref[...]).astype(o_ref.dtype)


def _norm_matmul(x, g, w, *, layer, tm, tn):
    m, d = x.shape
    n = w.shape[-1]
    wb = jnp.dtype(w.dtype).itemsize
    vmem = 3 * tm * d * 4 + 2 * d * tn * wb + 3 * tm * tn * 4 + (4 << 20)
    return pl.pallas_call(
        _norm_matmul_kernel,
        out_shape=jax.ShapeDtypeStruct((m, n), w.dtype),
        grid=(m // tm, n // tn),
        in_specs=[pl.BlockSpec((tm, d), lambda i, j: (i, 0)),
                  pl.BlockSpec((1, d), lambda i, j: (0, 0)),
                  pl.BlockSpec((None, d, tn), lambda i, j: (layer, 0, j))],
        out_specs=pl.BlockSpec((tm, tn), lambda i, j: (i, j)),
        compiler_params=_cparams(("parallel", "parallel"), vmem),
        name="mem_kv",
    )(x, g, w)


def _rope_kernel(pos_ref, ct_ref, st_ref):
    pos = pos_ref[...].astype(F32)
    lane = lax.broadcasted_iota(jnp.int32, (1, LANES), 1)
    half = MLA_ROPE // 2
    f = jnp.bitwise_and(lane, half - 1).astype(F32)
    inv_freq = jnp.power(jnp.float32(ROPE_THETA), -f / half)
    ang = pos * inv_freq
    valid = lane < MLA_ROPE
    ct_ref[...] = jnp.where(valid, jnp.cos(ang), 0.0)
    st_ref[...] = jnp.where(valid, jnp.sin(ang), 0.0)


def _rope_tables(pos_col, *, tm):
    m = pos_col.shape[0]
    return pl.pallas_call(
        _rope_kernel,
        out_shape=(jax.ShapeDtypeStruct((m, LANES), F32), jax.ShapeDtypeStruct((m, LANES), F32)),
        grid=(m // tm,),
        in_specs=[pl.BlockSpec((tm, 1), lambda i: (i, 0))],
        out_specs=(pl.BlockSpec((tm, LANES), lambda i: (i, 0)),
                   pl.BlockSpec((tm, LANES), lambda i: (i, 0))),
        compiler_params=_cparams(("parallel",), 16 << 20),
        name="rope_tables",
    )(pos_col)


def _mla_prep_kernel(cq_ref, ckv_ref, kr_ref, krrot_ref, ct_ref, st_ref, qn_ref, kvn_ref,
                     wa_ref, wb_ref, wkv_ref, q_ref, k_ref, v_ref):
    ct = ct_ref[...]
    st = st_ref[...]
    scale = (MLA_NOPE + MLA_ROPE) ** -0.5 * LOG2E

    ckv = ckv_ref[...].astype(F32)
    ckvn = _rmsnorm(ckv, kvn_ref[...]).astype(MXU_DTYPE)
    kv = _dot(ckvn, wkv_ref[...])
    kpe = kr_ref[...].astype(F32) * ct + krrot_ref[...].astype(F32) * st
    kpe = kpe.astype(k_ref.dtype)
    hw = MLA_HEADS * MLA_NOPE
    for h in range(MLA_HEADS):
        base = h * MLA_DK_PAD
        k_ref[:, base:base + MLA_NOPE] = kv[:, h * MLA_NOPE:(h + 1) * MLA_NOPE].astype(k_ref.dtype)
        k_ref[:, base + MLA_NOPE:base + MLA_DK_PAD] = kpe
    v_ref[...] = kv[:, hw:].astype(v_ref.dtype)

    cq = cq_ref[...].astype(F32)[:, :MLA_Q_RANK]
    cqn = _rmsnorm(cq, qn_ref[...]).astype(MXU_DTYPE)
    qa = _dot(cqn, wa_ref[...])
    qb = _dot(cqn, wb_ref[...])
    for h in range(MLA_HEADS):
        base = h * MLA_DK_PAD
        q_ref[:, base:base + MLA_NOPE] = (qa[:, base:base + MLA_NOPE] * scale).astype(q_ref.dtype)
        pe = qa[:, base + MLA_NOPE:base + MLA_DK_PAD] * ct + qb[:, h * LANES:(h + 1) * LANES] * st
        q_ref[:, base + MLA_NOPE:base + MLA_DK_PAD] = (pe * scale).astype(q_ref.dtype)


def _mla_prep(p, ct, st, qn, kvn, wa, wb, wkv, *, layer, tm):
    m = p.shape[0]
    qk_w = MLA_HEADS * MLA_DK_PAD
    v_w = MLA_HEADS * MLA_V
    full = lambda a: pl.BlockSpec(a.shape, lambda i: (0,) * a.ndim)
    of_layer = lambda a: pl.BlockSpec((None,) + a.shape[1:], lambda i: (layer,) + (0,) * (a.ndim - 1))
    return pl.pallas_call(
        _mla_prep_kernel,
        out_shape=(jax.ShapeDtypeStruct((m, qk_w), p.dtype),
                   jax.ShapeDtypeStruct((m, qk_w), p.dtype),
                   jax.ShapeDtypeStruct((m, v_w), p.dtype)),
        grid=(m // tm,),
        in_specs=[
            pl.BlockSpec((tm, 512), lambda i: (i, P_CQ // 512)),
            pl.BlockSpec((tm, 256), lambda i: (i, P_CKV // 256)),
            pl.BlockSpec((tm, 128), lambda i: (i, P_KR // 128)),
            pl.BlockSpec((tm, 128), lambda i: (i, P_KRROT // 128)),
            pl.BlockSpec((tm, LANES), lambda i: (i, 0)),
            pl.BlockSpec((tm, LANES), lambda i: (i, 0)),
            full(qn), full(kvn), of_layer(wa), of_layer(wb), of_layer(wkv),
        ],
        out_specs=(pl.BlockSpec((tm, qk_w), lambda i: (i, 0)),
                   pl.BlockSpec((tm, qk_w), lambda i: (i, 0)),
                   pl.BlockSpec((tm, v_w), lambda i: (i, 0))),
        compiler_params=_cparams(("parallel",), 40 << 20),
        name="mla_prep",
    )(p, p, p, p, ct, st, qn, kvn, wa, wb, wkv)


def _fox_cum_kernel(ft_ref, fb_ref, c_ref, *, seq):
    lf = _log_sigmoid(ft_ref[...] + fb_ref[...])
    r = lax.broadcasted_iota(jnp.int32, (LANES, LANES), 0)
    c = lax.broadcasted_iota(jnp.int32, (LANES, LANES), 1)
    upper = r <= c
    carry = jnp.zeros((SUBLANES, 1), F32)
    for blk in range(seq // LANES):
        sl = slice(blk * LANES, (blk + 1) * LANES)
        cs = _dot01_right(lf[:, sl], upper) + carry
        for h in range(FOX_HEADS):
            c_ref[h, :, sl] = cs[h:h + 1, :] * LOG2E
        carry = cs[:, LANES - 1:LANES]


def _fox_cum(pst, fb_col, *, batch, seq):
    return pl.pallas_call(
        functools.partial(_fox_cum_kernel, seq=seq),
        out_shape=jax.ShapeDtypeStruct((batch * FOX_HEADS, 1, seq), F32),
        grid=(batch,),
        in_specs=[pl.BlockSpec((SUBLANES, seq), lambda b: (PS_F // SUBLANES, b)),
                  pl.BlockSpec((SUBLANES, 1), lambda b: (0, 0))],
        out_specs=pl.BlockSpec((FOX_HEADS, 1, seq), lambda b: (b, 0, 0)),
        compiler_params=_cparams(("parallel",), 16 << 20),
        name="fox_cum",
    )(pst, fb_col)


def _flash_kernel(*refs, seq, tb, rmult, groups, dk, dv, has_bias):
    refs = list(refs)
    q_ref, k_ref, v_ref = refs[:3]
    c_ref = refs[3] if has_bias else None
    o_ref = refs[3 + int(has_bias)]
    rows_st = tb * rmult
    row = lax.broadcasted_iota(jnp.int32, (tb, tb), 0)
    col = lax.broadcasted_iota(jnp.int32, (tb, tb), 1)
    causal = row >= col

    def update(state, s, v_blk):
        m, l, acc = state
        m_new = jnp.maximum(m, jnp.max(s, axis=-1, keepdims=True))
        alpha = jnp.exp2(m - m_new)
        p = jnp.exp2(s - m_new)
        l = alpha * l + jnp.sum(p, axis=-1, keepdims=True)
        acc = alpha * acc + _dot(p.astype(MXU_DTYPE), v_blk)
        return m_new, l, acc

    def super_tile(pj, unused):
        r0 = pl.multiple_of(pj * rows_st, rows_st)
        qs = [q_ref[pl.ds(r0, rows_st), g * dk:(g + 1) * dk] for g in range(groups)]
        if has_bias:
            c0 = [jnp.max(c_ref[g, :, pl.ds(r0, tb)], axis=-1, keepdims=True) for g in range(groups)]

        def scores(q, g, start, width=tb):
            s = _dot_nt(q, k_ref[pl.ds(start, width), g * dk:(g + 1) * dk])
            if has_bias:
                s = s + (c0[g] - c_ref[g, :, pl.ds(start, width)])
            return s

        def v_block(g, start, width=tb):
            return v_ref[pl.ds(start, width), g * dv:(g + 1) * dv]

        def body(ki, carry):
            start = pl.multiple_of(ki * rows_st, rows_st)
            return tuple(update(carry[g], scores(qs[g], g, start, rows_st), v_block(g, start, rows_st))
                         for g in range(groups))

        init = tuple((jnp.full((rows_st, 1), -jnp.inf, F32), jnp.zeros((rows_st, 1), F32),
                      jnp.zeros((rows_st, dv), F32)) for _ in range(groups))
        carry = lax.fori_loop(0, pj, body, init)

        for g in range(groups):
            m, l, acc = carry[g]
            for a in range(rmult):
                rs = slice(a * tb, (a + 1) * tb)
                state = (m[rs], l[rs], acc[rs])
                for kb in range(a + 1):
                    start = pl.multiple_of(r0 + kb * tb, tb)
                    s = scores(qs[g][rs], g, start)
                    if kb == a:
                        s = jnp.where(causal, s, -jnp.inf)
                    state = update(state, s, v_block(g, start))
                o_ref[pl.ds(pl.multiple_of(r0 + a * tb, tb), tb), g * dv:(g + 1) * dv] = \
                    (state[2] / state[1]).astype(o_ref.dtype)
        return unused

    lax.fori_loop(0, seq // rows_st, super_tile, 0)


def _flash(q_arr, k_arr, v_arr, c_arr, *, batch, seq, heads, groups, dk, dv,
           q_off, k_off, v_off, tb, rmult):
    rows_st = tb * rmult
    has_bias = c_arr is not None
    gk, gv = groups * dk, groups * dv
    in_specs = [pl.BlockSpec((seq, gk), lambda b, h: (b, q_off // gk + h)),
                pl.BlockSpec((seq, gk), lambda b, h: (b, k_off // gk + h)),
                pl.BlockSpec((seq, gv), lambda b, h: (b, v_off // gv + h))]
    args = [q_arr, k_arr, v_arr]
    if has_bias:
        in_specs.append(pl.BlockSpec((groups, 1, seq), lambda b, h: (b * (heads // groups) + h, 0, 0)))
        args.append(c_arr)
    vmem = 2 * seq * (2 * gk + 2 * gv) * 2 + groups * (3 * rows_st * rows_st * 4 + 4 * rows_st * dv * 4) + (8 << 20)
    return pl.pallas_call(
        functools.partial(_flash_kernel, seq=seq, tb=tb, rmult=rmult, groups=groups, dk=dk, dv=dv,
                          has_bias=has_bias),
        out_shape=jax.ShapeDtypeStruct((batch * seq, heads * dv), q_arr.dtype),
        grid=(batch, heads // groups),
        in_specs=in_specs,
        out_specs=pl.BlockSpec((seq, gv), lambda b, h: (b, h)),
        compiler_params=_cparams(("parallel", "parallel"), vmem),
        name="flash_fox" if has_bias else "flash_mla",
    )(*args)


def _gla_kernel(q_ref, k_ref, v_ref, r_ref, ps_ref, wg2_ref, bg2_ref, nw_ref, o_ref, st_ref, *, tm):
    @pl.when(pl.program_id(1) == 0)
    def _():
        st_ref[...] = jnp.zeros_like(st_ref)

    glin = _dot(ps_ref[...].astype(MXU_DTYPE), wg2_ref[...]) + bg2_ref[...]
    gl = _log_sigmoid(glin) / GLA_GATE_NORM
    tril, _, same = _chunk_masks(tm, GLA_CHUNK)
    gcum = _dot01_left(tril, gl)
    gend = _dot01_left(same, gl)
    q = q_ref[...].astype(F32)
    k = k_ref[...].astype(F32)
    qd = (q * jnp.exp(gcum)).astype(MXU_DTYPE)
    kd = (k * jnp.exp(-gcum)).astype(MXU_DTYPE)
    ku = (k * jnp.exp(gend - gcum)).astype(MXU_DTYPE)
    dec = jnp.exp(gend)
    cr = lax.broadcasted_iota(jnp.int32, (GLA_CHUNK, GLA_CHUNK), 0)
    cc = lax.broadcasted_iota(jnp.int32, (GLA_CHUNK, GLA_CHUNK), 1)
    causal = cr >= cc
    nw = nw_ref[...]

    for c in range(tm // GLA_CHUNK):
        rows = slice(c * GLA_CHUNK, (c + 1) * GLA_CHUNK)
        for h in range(GLA_HEADS):
            kc = slice(h * GLA_DK, (h + 1) * GLA_DK)
            vc = slice(h * GLA_DV, (h + 1) * GLA_DV)
            qd_h, kd_h, ku_h = qd[rows, kc], kd[rows, kc], ku[rows, kc]
            v_h = v_ref[rows, vc]
            att = jnp.where(causal, _dot_nt(qd_h, kd_h), 0.0)
            state_t = st_ref[:, kc]
            o = _dot(att.astype(MXU_DTYPE), v_h) + _dot_nt(qd_h, state_t.astype(MXU_DTYPE))
            st_ref[:, kc] = state_t * dec[c * GLA_CHUNK:c * GLA_CHUNK + 1, kc] + _dot_tn(v_h, ku_h)
            o = _rmsnorm(o, nw)
            o_ref[rows, vc] = (o * _silu(r_ref[rows, vc].astype(F32))).astype(o_ref.dtype)


def _gla(p, ps, wg2, bg2, nw, *, batch, seq, tm):
    nt = seq // tm
    full = lambda a: pl.BlockSpec(a.shape, lambda b, i: (0,) * a.ndim)
    row = lambda b, i: b * nt + i
    return pl.pallas_call(
        functools.partial(_gla_kernel, tm=tm),
        out_shape=jax.ShapeDtypeStruct((p.shape[0], GLA_HEADS * GLA_DV), p.dtype),
        grid=(batch, nt),
        in_specs=[
            pl.BlockSpec((tm, 256), lambda b, i: (row(b, i), P_GLA_Q // 256)),
            pl.BlockSpec((tm, 256), lambda b, i: (row(b, i), P_GLA_K // 256)),
            pl.BlockSpec((tm, 512), lambda b, i: (row(b, i), P_GLA_V // 512)),
            pl.BlockSpec((tm, 512), lambda b, i: (row(b, i), P_GLA_R // 512)),
            pl.BlockSpec((tm, LANES), lambda b, i: (row(b, i), 0)),
            full(wg2), full(bg2), full(nw),
        ],
        out_specs=pl.BlockSpec((tm, GLA_HEADS * GLA_DV), lambda b, i: (row(b, i), 0)),
        scratch_shapes=[pltpu.VMEM((GLA_DV, GLA_HEADS * GLA_DK), F32)],
        compiler_params=_cparams(("parallel", "arbitrary"), 32 << 20),
        name="gla",
    )(p, p, p, p, ps, wg2, bg2, nw)


def _ssd_kernel(z_ref, x_ref, b_ref, c_ref, ps_ref, pst_ref, cw_ref, cb_ref, dtb_row_ref,
                dtb_col_ref, alog_row_ref, alog_col_ref, dskip_ref, nw_ref, exp_ref,
                o_ref, st_ref, tail_ref, *, tm):
    @pl.when(pl.program_id(1) == 0)
    def _():
        st_ref[...] = jnp.zeros_like(st_ref)
        tail_ref[...] = jnp.zeros_like(tail_ref)

    inner = SSD_INNER
    gw = inner // SSD_GROUPS
    xbc = jnp.concatenate([x_ref[...], b_ref[...], c_ref[...]], axis=-1).astype(F32)
    ext = jnp.concatenate([tail_ref[...], xbc], axis=0)
    tail_ref[...] = xbc[tm - SUBLANES:, :]
    conv = cb_ref[...]
    for j in range(SSD_CONV):
        shift = SSD_CONV - 1 - j
        src = ext if shift == 0 else pltpu.roll(ext, shift, axis=0)
        conv = conv + cw_ref[j:j + 1, :] * src[SUBLANES:, :]
    xc = _silu(conv)
    xs = xc[:, :inner]
    bm = xc[:, inner:inner + SSD_GROUPS * SSD_STATE].astype(MXU_DTYPE)
    cm = xc[:, inner + SSD_GROUPS * SSD_STATE:].astype(MXU_DTYPE)

    dt_exp = _dot01_right(_softplus(ps_ref[...] + dtb_row_ref[...]), exp_ref[...])
    a_row = -jnp.exp(alog_row_ref[...])
    dta = dt_exp * a_row
    tril, triu, same = _chunk_masks(tm, SSD_CHUNK)
    acum = _dot01_left(tril, dta)
    aend = _dot01_left(same, dta)
    dta_t = _softplus(pst_ref[...] + dtb_col_ref[...]) * (-jnp.exp(alog_col_ref[...]))
    acum_t = _dot01_right(dta_t, triu)
    x_dt = xs * dt_exp
    xw = (x_dt * jnp.exp(aend - acum)).astype(MXU_DTYPE)
    x_dt_m = x_dt.astype(MXU_DTYPE)
    e_acum = jnp.exp(acum)
    e_aend = jnp.exp(aend)
    cr = lax.broadcasted_iota(jnp.int32, (SSD_CHUNK, SSD_CHUNK), 0)
    cc = lax.broadcasted_iota(jnp.int32, (SSD_CHUNK, SSD_CHUNK), 1)
    causal = cr >= cc
    dskip = dskip_ref[...]
    nw = nw_ref[...]
    hg = SSD_HEADS // SSD_GROUPS

    for c in range(tm // SSD_CHUNK):
        r0 = c * SSD_CHUNK
        rows = slice(r0, r0 + SSD_CHUNK)
        for g in range(SSD_GROUPS):
            gl = slice(g * gw, (g + 1) * gw)
            nl = slice(g * SSD_STATE, (g + 1) * SSD_STATE)
            b_g, c_g = bm[rows, nl], cm[rows, nl]
            cb = _dot_nt(c_g, b_g)
            parts = []
            for hh in range(hg):
                h = g * hg + hh
                lane0 = h * SSD_HEADDIM
                colv = jnp.broadcast_to(acum[rows, lane0:lane0 + 1], (SSD_CHUNK, SSD_CHUNK))
                rowv = acum_t[h:h + 1, r0:r0 + SSD_CHUNK]
                decay = jnp.where(causal, jnp.exp(colv - rowv), 0.0)
                mmat = (cb * decay).astype(MXU_DTYPE)
                parts.append(_dot(mmat, x_dt_m[rows, lane0:lane0 + SSD_HEADDIM]))
            y_diag = jnp.concatenate(parts, axis=-1)
            state_t = st_ref[:, gl]
            y_off = _dot(c_g, state_t.astype(MXU_DTYPE)) * e_acum[rows, gl]
            st_ref[:, gl] = state_t * e_aend[r0:r0 + 1, gl] + _dot_tn(b_g, xw[rows, gl])
            y = y_diag + y_off + dskip[:, gl] * xs[rows, gl]
            y = y * _silu(z_ref[rows, gl].astype(F32))
            o_ref[rows, gl] = _rmsnorm(y, nw[:, gl]).astype(o_ref.dtype)


def _ssd(p, ps, pst, cw, cb, dtb_row, dtb_col, alog_row, alog_col, dskip, nw, expand, *, batch, seq, tm):
    nt = seq // tm
    full = lambda a: pl.BlockSpec(a.shape, lambda b, i: (0,) * a.ndim)
    row = lambda b, i: b * nt + i
    conv_dim = SSD_INNER + 2 * SSD_GROUPS * SSD_STATE
    return pl.pallas_call(
        functools.partial(_ssd_kernel, tm=tm),
        out_shape=jax.ShapeDtypeStruct((p.shape[0], SSD_INNER), p.dtype),
        grid=(batch, nt),
        in_specs=[
            pl.BlockSpec((tm, 512), lambda b, i: (row(b, i), P_SSD_Z // 512)),
            pl.BlockSpec((tm, 512), lambda b, i: (row(b, i), P_SSD_X // 512)),
            pl.BlockSpec((tm, 256), lambda b, i: (row(b, i), P_SSD_B // 256)),
            pl.BlockSpec((tm, 256), lambda b, i: (row(b, i), P_SSD_C // 256)),
            pl.BlockSpec((tm, LANES), lambda b, i: (row(b, i), 0)),
            pl.BlockSpec((SUBLANES, tm), lambda b, i: (PS_DT // SUBLANES, row(b, i))),
            full(cw), full(cb), full(dtb_row), full(dtb_col), full(alog_row), full(alog_col),
            full(dskip), full(nw), full(expand),
        ],
        out_specs=pl.BlockSpec((tm, SSD_INNER), lambda b, i: (row(b, i), 0)),
        scratch_shapes=[pltpu.VMEM((SSD_STATE, SSD_INNER), F32),
                        pltpu.VMEM((SUBLANES, conv_dim), F32)],
        compiler_params=_cparams(("parallel", "arbitrary"), 40 << 20),
        name="ssd",
    )(p, p, p, p, ps, pst, cw, cb, dtb_row, dtb_col, alog_row, alog_col, dskip, nw, expand)


def _merge_kernel(x_ref, g_ref, y0_ref, y1_ref, y2_ref, y3_ref, wsq_ref, wbr_ref, o_ref, h_ref, acc_ref,
                  *, tc):
    j = pl.program_id(1)
    d = x_ref.shape[-1]

    @pl.when(j == 0)
    def _():
        _norm_rows(x_ref, g_ref, h_ref)
        acc_ref[...] = jnp.zeros_like(acc_ref)

    @pl.when(j < N_BRANCH)
    def _():
        h = h_ref[...]
        y = jnp.where(j == 0, y0_ref[...], jnp.where(j == 1, y1_ref[...],
                                                      jnp.where(j == 2, y2_ref[...], y3_ref[...])))
        for c in range(d // tc):
            cols = slice(c * tc, (c + 1) * tc)
            gate = jax.nn.sigmoid(_dot(h, wsq_ref[:, cols]))
            acc_ref[:, cols] += gate * _dot(y, wbr_ref[:, cols])

    @pl.when(j == N_BRANCH)
    def _():
        merged = acc_ref[...].astype(MXU_DTYPE)
        for c in range(d // tc):
            cols = slice(c * tc, (c + 1) * tc)
            o_ref[:, cols] = x_ref[:, cols] + _dot(merged, wsq_ref[:, cols])


def _merge(x, g, ys, wsq, wbr, *, layer, tm, tc):
    m, d = x.shape
    assert len(ys) == N_BRANCH
    bw = ys[0].shape[-1]
    wbytes = jnp.dtype(wsq.dtype).itemsize
    vmem = 4 * tm * d * 4 + tm * d * wbytes + tm * d * 4 + 2 * d * d * wbytes \
        + 2 * bw * d * wbytes + 3 * N_BRANCH * tm * bw * wbytes + 4 * tm * tc * 4 + (4 << 20)
    last = N_BRANCH - 1
    y_spec = pl.BlockSpec((tm, bw), lambda i, j: (i, 0))
    return pl.pallas_call(
        functools.partial(_merge_kernel, tc=tc),
        out_shape=jax.ShapeDtypeStruct((m, d), F32),
        grid=(m // tm, N_BRANCH + 1),
        in_specs=[
            pl.BlockSpec((tm, d), lambda i, j: (i, 0)),
            pl.BlockSpec((1, d), lambda i, j: (0, 0)),
            y_spec, y_spec, y_spec, y_spec,
            pl.BlockSpec((None, d, d), lambda i, j: (layer, 0, j)),
            pl.BlockSpec((None, None, bw, d), lambda i, j: (layer, jnp.minimum(j, last), 0, 0)),
        ],
        out_specs=pl.BlockSpec((tm, d), lambda i, j: (i, 0)),
        scratch_shapes=[pltpu.VMEM((tm, d), wsq.dtype), pltpu.VMEM((tm, d), F32)],
        compiler_params=_cparams(("parallel", "arbitrary"), vmem),
        name="merge",
    )(x, g, *ys, wsq, wbr)


def _xattn_kernel(x_ref, g_ref, wq_ref, k_ref, v_ref, wo_ref, o_ref, h_ref, oc_ref, *, tc):
    _norm_rows(x_ref, g_ref, h_ref)
    h = h_ref[...]
    dh = XATTN_DH
    for hd in range(XATTN_HEADS):
        cols = slice(hd * dh, (hd + 1) * dh)
        q = _dot(h, wq_ref[:, cols]).astype(MXU_DTYPE)
        s = _dot_nt(q, k_ref[:, cols]) * (dh ** -0.5 * LOG2E)
        p = jnp.exp2(s - jnp.max(s, axis=-1, keepdims=True))
        p = p / jnp.sum(p, axis=-1, keepdims=True)
        oc_ref[:, cols] = _dot(p.astype(MXU_DTYPE), v_ref[:, cols]).astype(oc_ref.dtype)
    oc = oc_ref[...]
    d = x_ref.shape[-1]
    for c in range(d // tc):
        cols = slice(c * tc, (c + 1) * tc)
        o_ref[:, cols] = x_ref[:, cols] + _dot(oc, wo_ref[:, cols])


def _xattn(x, g, wq, kv, wo, *, layer, seq, n_mem, tm, tc):
    m, d = x.shape
    wbytes = jnp.dtype(wq.dtype).itemsize
    vmem = 4 * tm * d * 4 + 2 * tm * d * wbytes + 2 * d * d * wbytes + 4 * n_mem * d * wbytes \
        + 3 * tm * XATTN_DH * 4 + 3 * tm * n_mem * 4 + 2 * tm * tc * 4 + (6 << 20)
    tiles_per_seq = seq // tm
    return pl.pallas_call(
        functools.partial(_xattn_kernel, tc=tc),
        out_shape=jax.ShapeDtypeStruct((m, d), F32),
        grid=(m // tm,),
        in_specs=[
            pl.BlockSpec((tm, d), lambda i: (i, 0)),
            pl.BlockSpec((1, d), lambda i: (0, 0)),
            pl.BlockSpec((None, d, d), lambda i: (layer, 0, 0), pipeline_mode=pl.Buffered(1)),
            pl.BlockSpec((n_mem, d), lambda i: (i // tiles_per_seq, 0)),
            pl.BlockSpec((n_mem, d), lambda i: (i // tiles_per_seq, 1)),
            pl.BlockSpec((None, d, d), lambda i: (layer, 0, 0), pipeline_mode=pl.Buffered(1)),
        ],
        out_specs=pl.BlockSpec((tm, d), lambda i: (i, 0)),
        scratch_shapes=[pltpu.VMEM((tm, d), wq.dtype), pltpu.VMEM((tm, d), wq.dtype)],
        compiler_params=_cparams(("parallel",), vmem),
        name="xattn",
    )(x, g, wq, kv, kv, wo)


_IN_SIZES = (N_BRANCH * D_MODEL, MLA_Q_RANK, MLA_KV_RANK, MLA_ROPE, 3 * FOX_HEADS * FOX_DH, FOX_HEADS,
             GLA_HEADS * GLA_DK, GLA_HEADS * GLA_DK, GLA_HEADS * GLA_DV, GLA_GATE_RANK,
             GLA_HEADS * GLA_DV, SSD_INNER, SSD_INNER + 2 * SSD_GROUPS * SSD_STATE, SSD_HEADS)


def _rot_cols(w):
    half = w.shape[-1] // 2
    return jnp.concatenate([-w[..., half:], w[..., :half]], axis=-1)


def _prep_w_in(w_mix):
    w_in = w_mix.astype(MXU_DTYPE)
    nl, d = w_in.shape[:2]
    offs = np.cumsum((0,) + _IN_SIZES[1:])
    assert offs[-1] == w_in.shape[2]
    seg = lambda i: w_in[:, :, offs[i]:offs[i + 1]]
    (cq, ckv, kr, fox, fox_f, gq, gk, gv, glr, gr, sz, sxbc, sdt) = [seg(i) for i in range(len(_IN_SIZES) - 1)]
    z = lambda n: jnp.zeros((nl, d, n), w_in.dtype)
    w_p = jnp.concatenate([
        cq, z(512 - MLA_Q_RANK), ckv, kr, z(LANES - MLA_ROPE), _rot_cols(kr), z(LANES - MLA_ROPE),
        fox, gq, gk, gv, gr, sz, sxbc], axis=2)
    assert w_p.shape[2] == P_WIDTH
    cs = np.ones((1, P_WIDTH), np.float32)
    cs[:, P_FOX_Q:P_FOX_K] = FOX_DH ** -0.5 * LOG2E
    cs[:, P_GLA_Q:P_GLA_K] = GLA_DK ** -0.5
    w_s = jnp.concatenate([glr, sdt, fox_f, z(LANES - GLA_GATE_RANK - SSD_HEADS - FOX_HEADS)], axis=2)
    return w_p.astype(MXU_DTYPE), jnp.asarray(cs), w_s.astype(MXU_DTYPE)


def _prep_mla(w_uq, w_ukv):
    nl, r = w_uq.shape[:2]
    wq = w_uq.reshape(nl, r, MLA_HEADS, MLA_NOPE + MLA_ROPE)
    nope, pe = wq[..., :MLA_NOPE], wq[..., MLA_NOPE:]
    zpad = jnp.zeros((nl, r, MLA_HEADS, MLA_DK_PAD - MLA_NOPE - MLA_ROPE), w_uq.dtype)
    wa = jnp.concatenate([nope, pe, zpad], axis=-1).reshape(nl, r, MLA_HEADS * MLA_DK_PAD)
    wb = jnp.concatenate([_rot_cols(pe), zpad], axis=-1).reshape(nl, r, MLA_HEADS * LANES)
    rk = w_ukv.shape[1]
    wkv = w_ukv.reshape(nl, rk, MLA_HEADS, MLA_NOPE + MLA_V)
    wkv = jnp.concatenate([wkv[..., :MLA_NOPE].reshape(nl, rk, -1), wkv[..., MLA_NOPE:].reshape(nl, rk, -1)],
                          axis=2)
    return wa.astype(MXU_DTYPE), wb.astype(MXU_DTYPE), wkv.astype(MXU_DTYPE)


def _tiles(m, seq):
    mid = min(512, seq)
    return dict(ffn_tm=mid, ffn_tf=512, proj_tm=mid, proj_tn=2816, prep_tm=mid,
                flash_tb=mid, flash_rmult=min(2, seq // mid), gla_tm=min(256, seq), ssd_tm=min(256, seq), merge_tm=mid, merge_tc=512,
                xattn_tm=mid, xattn_tc=512, rope_tm=min(1024, m))


@jax.jit
def kernel(x, mem, positions, ffn1_norm, ffn1_w_gate, ffn1_w_up, ffn1_w_down, mix_norm, w_in, mla_q_norm, mla_kv_norm, mla_w_uq, mla_w_ukv, fox_f_bias, gla_w_g2, gla_b_g2, gla_norm, ssd_conv_w, ssd_conv_b, ssd_dt_bias, ssd_a_log, ssd_d, ssd_norm, w_branch, w_out, xattn_norm, mem_norm, xattn_w_q, xattn_w_k, xattn_w_v, xattn_w_o, ffn2_norm, ffn2_w_gate, ffn2_w_up, ffn2_w_down, final_norm):
    batch, seq, d = x.shape
    n_mem = mem.shape[1]
    m = batch * seq
    t = _tiles(m, seq)
    depth = w_in.shape[0]
    cast = lambda a: a.astype(MXU_DTYPE)
    row = lambda a: a.reshape(1, -1).astype(F32)
    col8 = lambda v: jnp.zeros((SUBLANES, 1), F32).at[:v.shape[0], 0].set(v.astype(F32))

    xs = x.reshape(m, d)
    mem2 = mem.reshape(batch * n_mem, d)
    ct, st = _rope_tables(positions.reshape(m, 1), tm=t["rope_tm"])

    expand = np.zeros((LANES, SSD_INNER), np.float32)
    for h in range(SSD_HEADS):
        expand[PS_DT + h, h * SSD_HEADDIM:(h + 1) * SSD_HEADDIM] = 1.0
    expand = jnp.asarray(expand, MXU_DTYPE)
    rep = lambda v: jnp.repeat(v.astype(F32), SSD_HEADDIM).reshape(1, SSD_INNER)
    ffn = functools.partial(_ffn, tm=t["ffn_tm"], tf=t["ffn_tf"])
    flash = functools.partial(_flash, batch=batch, seq=seq, groups=2, tb=t["flash_tb"], rmult=t["flash_rmult"])

    f1g, f1u, f1d = cast(ffn1_w_gate), cast(ffn1_w_up), cast(ffn1_w_down)
    f2g, f2u, f2d = cast(ffn2_w_gate), cast(ffn2_w_up), cast(ffn2_w_down)
    w_p, cs, w_s = _prep_w_in(w_in[:, :, N_BRANCH * d:])
    wa, wb, wkv = _prep_mla(mla_w_uq, mla_w_ukv)
    wsq = jnp.concatenate([cast(w_in[:, :, :N_BRANCH * d]), cast(w_out)], axis=2)
    wbr = cast(w_branch)
    w_kv = cast(jnp.concatenate([xattn_w_k, xattn_w_v], axis=2))
    wq_x, wo_x = cast(xattn_w_q), cast(xattn_w_o)

    for l in range(depth):
        xs = ffn(xs, row(ffn1_norm[l]), f1g, f1u, f1d, row(final_norm), layer=l, final_norm=False)

        p, ps, pst = _proj(xs, row(mix_norm[l]), w_p, cs, w_s, layer=l, tm=t["proj_tm"], tn=t["proj_tn"])

        q_mla, k_mla, v_mla = _mla_prep(p, ct, st, row(mla_q_norm[l]), row(mla_kv_norm[l]), wa, wb, wkv,
                                        layer=l, tm=t["prep_tm"])
        y_mla = flash(q_mla, k_mla, v_mla, None, heads=MLA_HEADS,
                      dk=MLA_DK_PAD, dv=MLA_V, q_off=0, k_off=0, v_off=0)

        c_fox = _fox_cum(pst, col8(fox_f_bias[l]), batch=batch, seq=seq)
        y_fox = flash(p, p, p, c_fox, heads=FOX_HEADS,
                      dk=FOX_DH, dv=FOX_DH, q_off=P_FOX_Q, k_off=P_FOX_K, v_off=P_FOX_V)

        wg2 = jnp.zeros((LANES, GLA_HEADS * GLA_DK), F32).at[PS_GLR:PS_GLR + GLA_GATE_RANK].set(gla_w_g2[l])
        y_gla = _gla(p, ps, cast(wg2), row(gla_b_g2[l]), row(gla_norm[l]), batch=batch, seq=seq, tm=t["gla_tm"])

        dtb_row = jnp.zeros((1, LANES), F32).at[0, PS_DT:PS_DT + SSD_HEADS].set(ssd_dt_bias[l])
        y_ssd = _ssd(p, ps, pst, ssd_conv_w[l].astype(F32), row(ssd_conv_b[l]), dtb_row, col8(ssd_dt_bias[l]),
                     rep(ssd_a_log[l]), col8(ssd_a_log[l]), rep(ssd_d[l]), row(ssd_norm[l]), expand,
                     batch=batch, seq=seq, tm=t["ssd_tm"])

        xs = _merge(xs, row(mix_norm[l]), (y_mla, y_fox, y_gla, y_ssd), wsq, wbr, layer=l, tm=t["merge_tm"], tc=t["merge_tc"])

        kv = _norm_matmul(mem2, row(mem_norm[l]), w_kv, layer=l, tm=min(512, batch * n_mem), tn=1024)
        xs = _xattn(xs, row(xattn_norm[l]), wq_x, kv, wo_x, layer=l,
                    seq=seq, n_mem=n_mem, tm=t["xattn_tm"], tc=t["xattn_tc"])

        xs = ffn(xs, row(ffn2_norm[l]), f2g, f2u, f2d, row(final_norm), layer=l, final_norm=(l == depth - 1))

    return xs.reshape(batch, seq, d)
```

```python
import functools

import numpy as np
import jax
import jax.numpy as jnp
from jax import lax
from jax.experimental import pallas as pl
from jax.experimental.pallas import tpu as pltpu

D_MODEL = 2048
DEPTH = 4
EPS = 1e-6
N_BRANCH = 4
BRANCH_WIDTH = 512
MLA_HEADS, MLA_NOPE, MLA_ROPE, MLA_V = 4, 128, 64, 128
MLA_Q_RANK, MLA_KV_RANK = 384, 256
MLA_DK_PAD = 256
ROPE_THETA = 10000.0
FOX_HEADS, FOX_DH = 4, 128
GLA_HEADS, GLA_DK, GLA_DV = 4, 64, 128
GLA_GATE_RANK, GLA_GATE_NORM, GLA_CHUNK = 16, 16.0, 64
SSD_HEADS, SSD_HEADDIM, SSD_GROUPS, SSD_STATE, SSD_CONV, SSD_CHUNK = 8, 64, 2, 128, 4, 128
SSD_INNER = SSD_HEADS * SSD_HEADDIM
XATTN_HEADS = 4
XATTN_DH = D_MODEL // XATTN_HEADS
D_FF = 5632

LANES = 128
SUBLANES = 8
V7X_VMEM_LIMIT_BYTES = 60 * 1024 * 1024

MXU_DTYPE = jnp.bfloat16
FFN_SLOTS = 3
F32 = jnp.float32
LOG2E = 1.4426950408889634

P_CQ, P_CKV, P_KR, P_KRROT = 0, 512, 768, 896
P_FOX_Q, P_FOX_K, P_FOX_V = 1024, 1536, 2048
P_GLA_Q, P_GLA_K, P_GLA_V, P_GLA_R = 2560, 2816, 3072, 3584
P_SSD_Z, P_SSD_X, P_SSD_B, P_SSD_C = 4096, 4608, 5120, 5376
P_WIDTH = 5632
PS_GLR, PS_DT, PS_F = 0, 16, 24
PST_ROWS = 32


def _cparams(sem, vmem_bytes):
    return pltpu.CompilerParams(dimension_semantics=sem,
                                vmem_limit_bytes=int(min(vmem_bytes, V7X_VMEM_LIMIT_BYTES)))


def _dot(a, b):
    return jnp.dot(a, b, preferred_element_type=F32)


def _dot_nt(a, b):
    return lax.dot_general(a, b, (((1,), (1,)), ((), ())), preferred_element_type=F32)


def _dot_tn(a, b):
    return lax.dot_general(a, b, (((0,), (0,)), ((), ())), preferred_element_type=F32)


def _split3(x):
    hi = x.astype(MXU_DTYPE)
    r1 = x - hi.astype(F32)
    mid = r1.astype(MXU_DTYPE)
    lo = (r1 - mid.astype(F32)).astype(MXU_DTYPE)
    return hi, mid, lo


def _dot01_right(x, m01):
    if MXU_DTYPE == F32:
        return _dot(x, m01.astype(F32))
    m = m01.astype(MXU_DTYPE)
    hi, mid, lo = _split3(x)
    return _dot(hi, m) + _dot(mid, m) + _dot(lo, m)


def _dot01_left(m01, x):
    if MXU_DTYPE == F32:
        return _dot(m01.astype(F32), x)
    m = m01.astype(MXU_DTYPE)
    hi, mid, lo = _split3(x)
    return _dot(m, hi) + _dot(m, mid) + _dot(m, lo)


def _rmsnorm(x, g):
    ms = jnp.mean(x * x, axis=-1, keepdims=True)
    return x * lax.rsqrt(ms + EPS) * g


def _log_sigmoid(x):
    return jnp.minimum(x, 0.0) - jnp.log1p(jnp.exp(-jnp.abs(x)))


def _softplus(x):
    return jnp.maximum(x, 0.0) + jnp.log1p(jnp.exp(-jnp.abs(x)))


def _silu(x):
    return x * jax.nn.sigmoid(x)


def _chunk_masks(n, chunk):
    shift = chunk.bit_length() - 1
    assert 1 << shift == chunk
    r = lax.broadcasted_iota(jnp.int32, (n, n), 0)
    c = lax.broadcasted_iota(jnp.int32, (n, n), 1)
    same = jnp.right_shift(r, shift) == jnp.right_shift(c, shift)
    return jnp.logical_and(same, r >= c), jnp.logical_and(same, r <= c), same


def _norm_rows(x_ref, g_ref, h_ref, copy_ref=None, *, rows=256):
    tm = x_ref.shape[0]
    rows = min(rows, tm)
    g = g_ref[...]
    for r in range(tm // rows):
        sl = slice(r * rows, (r + 1) * rows)
        x = x_ref[sl, :]
        h_ref[sl, :] = _rmsnorm(x, g).astype(h_ref.dtype)
        if copy_ref is not None:
            copy_ref[sl, :] = x


def _ffn_kernel(x_ref, g_ref, wg_hbm, wu_hbm, wd_hbm, fg_ref, o_ref, h_ref, wg_buf, wu_buf, wd_buf, sem,
                *, final_norm, layer, tf, n_chunks):
    i = pl.program_id(0)

    def chunk_copies(j, slot):
        cols = pl.ds(j * tf, tf)
        return (pltpu.make_async_copy(wg_hbm.at[layer, :, cols], wg_buf.at[slot], sem.at[0, slot]),
                pltpu.make_async_copy(wu_hbm.at[layer, :, cols], wu_buf.at[slot], sem.at[1, slot]),
                pltpu.make_async_copy(wd_hbm.at[layer, cols, :], wd_buf.at[slot], sem.at[2, slot]))

    def start(j, slot):
        for cp in chunk_copies(j, slot):
            cp.start()

    @pl.when(i == 0)
    def _():
        start(0, 0)

    _norm_rows(x_ref, g_ref, h_ref, o_ref)
    h = h_ref[...]
    for j in range(n_chunks):
        slot = j % FFN_SLOTS
        for cp in chunk_copies(j, slot):
            cp.wait()
        if j + 1 < n_chunks:
            start(j + 1, (j + 1) % FFN_SLOTS)
        else:
            @pl.when(i + 1 < pl.num_programs(0))
            def _():
                start(0, 0)
        a = _dot(h, wg_buf[slot])
        u = _dot(h, wu_buf[slot])
        t = (_silu(a) * (0.5 * u)).astype(MXU_DTYPE)
        o_ref[...] += _dot(t, wd_buf[slot])

    if final_norm:
        _norm_rows(o_ref, fg_ref, o_ref)


def _ffn(x, g, wg, wu, wd, fg, *, layer, final_norm, tm, tf):
    m, d = x.shape
    f = wg.shape[-1]
    n_chunks = f // tf
    assert n_chunks * tf == f and n_chunks > FFN_SLOTS and (n_chunks - 1) % FFN_SLOTS != 0
    wb = jnp.dtype(wg.dtype).itemsize
    vmem = 4 * tm * d * 4 + tm * d * wb + FFN_SLOTS * 3 * d * tf * wb + 4 * tm * tf * 4 + tm * d * 4 + (4 << 20)
    return pl.pallas_call(
        functools.partial(_ffn_kernel, final_norm=final_norm, layer=layer, tf=tf, n_chunks=n_chunks),
        out_shape=jax.ShapeDtypeStruct((m, d), F32),
        grid=(m // tm,),
        in_specs=[
            pl.BlockSpec((tm, d), lambda i: (i, 0)),
            pl.BlockSpec((1, d), lambda i: (0, 0)),
            pl.BlockSpec(memory_space=pl.ANY),
            pl.BlockSpec(memory_space=pl.ANY),
            pl.BlockSpec(memory_space=pl.ANY),
            pl.BlockSpec((1, d), lambda i: (0, 0)),
        ],
        out_specs=pl.BlockSpec((tm, d), lambda i: (i, 0)),
        scratch_shapes=[pltpu.VMEM((tm, d), wg.dtype),
                        pltpu.VMEM((FFN_SLOTS, d, tf), wg.dtype),
                        pltpu.VMEM((FFN_SLOTS, d, tf), wg.dtype),
                        pltpu.VMEM((FFN_SLOTS, tf, d), wg.dtype),
                        pltpu.SemaphoreType.DMA((3, FFN_SLOTS))],
        compiler_params=_cparams(("arbitrary",), vmem),
        name="ffn",
    )(x, g, wg, wu, wd, fg)


def _proj_kernel(x_ref, g_ref, w_ref, cs_ref, ws_ref, p_ref, ps_ref, pst_ref, h_ref):
    j = pl.program_id(1)

    @pl.when(j == 0)
    def _():
        _norm_rows(x_ref, g_ref, h_ref)
        ps = _dot(h_ref[...], ws_ref[...])
        ps_ref[...] = ps
        pst_ref[...] = jnp.transpose(ps)[:PST_ROWS, :]

    p_ref[...] = (_dot(h_ref[...], w_ref[...]) * cs_ref[...]).astype(p_ref.dtype)


def _proj(x, g, w, cs, ws, *, layer, tm, tn):
    m, d = x.shape
    n = w.shape[-1]
    wb = jnp.dtype(w.dtype).itemsize
    vmem = 2 * tm * d * 4 + tm * d * wb + 2 * d * tn * wb + 2 * tm * tn * wb + 2 * tm * tn * 4 + (6 << 20)
    return pl.pallas_call(
        _proj_kernel,
        out_shape=(jax.ShapeDtypeStruct((m, n), w.dtype),
                   jax.ShapeDtypeStruct((m, LANES), F32),
                   jax.ShapeDtypeStruct((PST_ROWS, m), F32)),
        grid=(m // tm, n // tn),
        in_specs=[
            pl.BlockSpec((tm, d), lambda i, j: (i, 0)),
            pl.BlockSpec((1, d), lambda i, j: (0, 0)),
            pl.BlockSpec((None, d, tn), lambda i, j: (layer, 0, j)),
            pl.BlockSpec((1, tn), lambda i, j: (0, j)),
            pl.BlockSpec((None, d, LANES), lambda i, j: (layer, 0, 0)),
        ],
        out_specs=(pl.BlockSpec((tm, tn), lambda i, j: (i, j)),
                   pl.BlockSpec((tm, LANES), lambda i, j: (i, 0)),
                   pl.BlockSpec((PST_ROWS, tm), lambda i, j: (0, i))),
        scratch_shapes=[pltpu.VMEM((tm, d), w.dtype)],
        compiler_params=_cparams(("parallel", "arbitrary"), vmem),
        name="proj",
    )(x, g, w, cs, ws)


def _norm_matmul_kernel(x_ref, g_ref, w_ref, o_ref):
    h = _rmsnorm(x_ref[...], g_ref[...]).astype(w_ref.dtype)
    o_ref[...] = _dot(h, w_ref[...]).astype(o_ref.dtype)


def _norm_matmul(x, g, w, *, layer, tm, tn):
    m, d = x.shape
    n = w.shape[-1]
    wb = jnp.dtype(w.dtype).itemsize
    vmem = 3 * tm * d * 4 + 2 * d * tn * wb + 3 * tm * tn * 4 + (4 << 20)
    return pl.pallas_call(
        _norm_matmul_kernel,
        out_shape=jax.ShapeDtypeStruct((m, n), w.dtype),
        grid=(m // tm, n // tn),
        in_specs=[pl.BlockSpec((tm, d), lambda i, j: (i, 0)),
                  pl.BlockSpec((1, d), lambda i, j: (0, 0)),
                  pl.BlockSpec((None, d, tn), lambda i, j: (layer, 0, j))],
        out_specs=pl.BlockSpec((tm, tn), lambda i, j: (i, j)),
        compiler_params=_cparams(("parallel", "parallel"), vmem),
        name="mem_kv",
    )(x, g, w)


def _rope_kernel(pos_ref, ct_ref, st_ref):
    pos = pos_ref[...].astype(F32)
    lane = lax.broadcasted_iota(jnp.int32, (1, LANES), 1)
    half = MLA_ROPE // 2
    f = jnp.bitwise_and(lane, half - 1).astype(F32)
    inv_freq = jnp.power(jnp.float32(ROPE_THETA), -f / half)
    ang = pos * inv_freq
    valid = lane < MLA_ROPE
    ct_ref[...] = jnp.where(valid, jnp.cos(ang), 0.0)
    st_ref[...] = jnp.where(valid, jnp.sin(ang), 0.0)


def _rope_tables(pos_col, *, tm):
    m = pos_col.shape[0]
    return pl.pallas_call(
        _rope_kernel,
        out_shape=(jax.ShapeDtypeStruct((m, LANES), F32), jax.ShapeDtypeStruct((m, LANES), F32)),
        grid=(m // tm,),
        in_specs=[pl.BlockSpec((tm, 1), lambda i: (i, 0))],
        out_specs=(pl.BlockSpec((tm, LANES), lambda i: (i, 0)),
                   pl.BlockSpec((tm, LANES), lambda i: (i, 0))),
        compiler_params=_cparams(("parallel",), 16 << 20),
        name="rope_tables",
    )(pos_col)


def _mla_prep_kernel(cq_ref, ckv_ref, kr_ref, krrot_ref, ct_ref, st_ref, qn_ref, kvn_ref,
                     wa_ref, wb_ref, wkv_ref, q_ref, k_ref, v_ref):
    ct = ct_ref[...]
    st = st_ref[...]
    scale = (MLA_NOPE + MLA_ROPE) ** -0.5 * LOG2E

    ckv = ckv_ref[...].astype(F32)
    ckvn = _rmsnorm(ckv, kvn_ref[...]).astype(MXU_DTYPE)
    kv = _dot(ckvn, wkv_ref[...])
    kpe = kr_ref[...].astype(F32) * ct + krrot_ref[...].astype(F32) * st
    kpe = kpe.astype(k_ref.dtype)
    hw = MLA_HEADS * MLA_NOPE
    for h in range(MLA_HEADS):
        base = h * MLA_DK_PAD
        k_ref[:, base:base + MLA_NOPE] = kv[:, h * MLA_NOPE:(h + 1) * MLA_NOPE].astype(k_ref.dtype)
        k_ref[:, base + MLA_NOPE:base + MLA_DK_PAD] = kpe
    v_ref[...] = kv[:, hw:].astype(v_ref.dtype)

    cq = cq_ref[...].astype(F32)[:, :MLA_Q_RANK]
    cqn = _rmsnorm(cq, qn_ref[...]).astype(MXU_DTYPE)
    qa = _dot(cqn, wa_ref[...])
    qb = _dot(cqn, wb_ref[...])
    for h in range(MLA_HEADS):
        base = h * MLA_DK_PAD
        q_ref[:, base:base + MLA_NOPE] = (qa[:, base:base + MLA_NOPE] * scale).astype(q_ref.dtype)
        pe = qa[:, base + MLA_NOPE:base + MLA_DK_PAD] * ct + qb[:, h * LANES:(h + 1) * LANES] * st
        q_ref[:, base + MLA_NOPE:base + MLA_DK_PAD] = (pe * scale).astype(q_ref.dtype)


def _mla_prep(p, ct, st, qn, kvn, wa, wb, wkv, *, layer, tm):
    m = p.shape[0]
    qk_w = MLA_HEADS * MLA_DK_PAD
    v_w = MLA_HEADS * MLA_V
    full = lambda a: pl.BlockSpec(a.shape, lambda i: (0,) * a.ndim)
    of_layer = lambda a: pl.BlockSpec((None,) + a.shape[1:], lambda i: (layer,) + (0,) * (a.ndim - 1))
    return pl.pallas_call(
        _mla_prep_kernel,
        out_shape=(jax.ShapeDtypeStruct((m, qk_w), p.dtype),
                   jax.ShapeDtypeStruct((m, qk_w), p.dtype),
                   jax.ShapeDtypeStruct((m, v_w), p.dtype)),
        grid=(m // tm,),
        in_specs=[
            pl.BlockSpec((tm, 512), lambda i: (i, P_CQ // 512)),
            pl.BlockSpec((tm, 256), lambda i: (i, P_CKV // 256)),
            pl.BlockSpec((tm, 128), lambda i: (i, P_KR // 128)),
            pl.BlockSpec((tm, 128), lambda i: (i, P_KRROT // 128)),
            pl.BlockSpec((tm, LANES), lambda i: (i, 0)),
            pl.BlockSpec((tm, LANES), lambda i: (i, 0)),
            full(qn), full(kvn), of_layer(wa), of_layer(wb), of_layer(wkv),
        ],
        out_specs=(pl.BlockSpec((tm, qk_w), lambda i: (i, 0)),
                   pl.BlockSpec((tm, qk_w), lambda i: (i, 0)),
                   pl.BlockSpec((tm, v_w), lambda i: (i, 0))),
        compiler_params=_cparams(("parallel",), 40 << 20),
        name="mla_prep",
    )(p, p, p, p, ct, st, qn, kvn, wa, wb, wkv)


def _fox_cum_kernel(ft_ref, fb_ref, c_ref, *, seq):
    lf = _log_sigmoid(ft_ref[...] + fb_ref[...])
    r = lax.broadcasted_iota(jnp.int32, (LANES, LANES), 0)
    c = lax.broadcasted_iota(jnp.int32, (LANES, LANES), 1)
    upper = r <= c
    carry = jnp.zeros((SUBLANES, 1), F32)
    for blk in range(seq // LANES):
        sl = slice(blk * LANES, (blk + 1) * LANES)
        cs = _dot01_right(lf[:, sl], upper) + carry
        for h in range(FOX_HEADS):
            c_ref[h, :, sl] = cs[h:h + 1, :] * LOG2E
        carry = cs[:, LANES - 1:LANES]


def _fox_cum(pst, fb_col, *, batch, seq):
    return pl.pallas_call(
        functools.partial(_fox_cum_kernel, seq=seq),
        out_shape=jax.ShapeDtypeStruct((batch * FOX_HEADS, 1, seq), F32),
        grid=(batch,),
        in_specs=[pl.BlockSpec((SUBLANES, seq), lambda b: (PS_F // SUBLANES, b)),
                  pl.BlockSpec((SUBLANES, 1), lambda b: (0, 0))],
        out_specs=pl.BlockSpec((FOX_HEADS, 1, seq), lambda b: (b, 0, 0)),
        compiler_params=_cparams(("parallel",), 16 << 20),
        name="fox_cum",
    )(pst, fb_col)


def _flash_kernel(*refs, seq, tb, rmult, groups, dk, dv, has_bias):
    refs = list(refs)
    q_ref, k_ref, v_ref = refs[:3]
    c_ref = refs[3] if has_bias else None
    o_ref = refs[3 + int(has_bias)]
    rows_st = tb * rmult
    row = lax.broadcasted_iota(jnp.int32, (tb, tb), 0)
    col = lax.broadcasted_iota(jnp.int32, (tb, tb), 1)
    causal = row >= col

    def update(state, s, v_blk):
        m, l, acc = state
        m_new = jnp.maximum(m, jnp.max(s, axis=-1, keepdims=True))
        alpha = jnp.exp2(m - m_new)
        p = jnp.exp2(s - m_new)
        l = alpha * l + jnp.sum(p, axis=-1, keepdims=True)
        acc = alpha * acc + _dot(p.astype(MXU_DTYPE), v_blk)
        return m_new, l, acc

    def super_tile(pj, unused):
        r0 = pl.multiple_of(pj * rows_st, rows_st)
        qs = [q_ref[pl.ds(r0, rows_st), g * dk:(g + 1) * dk] for g in range(groups)]
        if has_bias:
            c0 = [jnp.max(c_ref[g, :, pl.ds(r0, tb)], axis=-1, keepdims=True) for g in range(groups)]

        def scores(q, g, start, width=tb):
            s = _dot_nt(q, k_ref[pl.ds(start, width), g * dk:(g + 1) * dk])
            if has_bias:
                s = s + (c0[g] - c_ref[g, :, pl.ds(start, width)])
            return s

        def v_block(g, start, width=tb):
            return v_ref[pl.ds(start, width), g * dv:(g + 1) * dv]

        def body(ki, carry):
            start = pl.multiple_of(ki * rows_st, rows_st)
            return tuple(update(carry[g], scores(qs[g], g, start, rows_st), v_block(g, start, rows_st))
                         for g in range(groups))

        init = tuple((jnp.full((rows_st, 1), -jnp.inf, F32), jnp.zeros((rows_st, 1), F32),
                      jnp.zeros((rows_st, dv), F32)) for _ in range(groups))
        carry = lax.fori_loop(0, pj, body, init)

        for g in range(groups):
            m, l, acc = carry[g]
            for a in range(rmult):
                rs = slice(a * tb, (a + 1) * tb)
                state = (m[rs], l[rs], acc[rs])
                for kb in range(a + 1):
                    start = pl.multiple_of(r0 + kb * tb, tb)
                    s = scores(qs[g][rs], g, start)
                    if kb == a:
                        s = jnp.where(causal, s, -jnp.inf)
                    state = update(state, s, v_block(g, start))
                o_ref[pl.ds(pl.multiple_of(r0 + a * tb, tb), tb), g * dv:(g + 1) * dv] = \
                    (state[2] / state[1]).astype(o_ref.dtype)
        return unused

    lax.fori_loop(0, seq // rows_st, super_tile, 0)


def _flash(q_arr, k_arr, v_arr, c_arr, *, batch, seq, heads, groups, dk, dv,
           q_off, k_off, v_off, tb, rmult):
    rows_st = tb * rmult
    has_bias = c_arr is not None
    gk, gv = groups * dk, groups * dv
    in_specs = [pl.BlockSpec((seq, gk), lambda b, h: (b, q_off // gk + h)),
                pl.BlockSpec((seq, gk), lambda b, h: (b, k_off // gk + h)),
                pl.BlockSpec((seq, gv), lambda b, h: (b, v_off // gv + h))]
    args = [q_arr, k_arr, v_arr]
    if has_bias:
        in_specs.append(pl.BlockSpec((groups, 1, seq), lambda b, h: (b * (heads // groups) + h, 0, 0)))
        args.append(c_arr)
    vmem = 2 * seq * (2 * gk + 2 * gv) * 2 + groups * (3 * rows_st * rows_st * 4 + 4 * rows_st * dv * 4) + (8 << 20)
    return pl.pallas_call(
        functools.partial(_flash_kernel, seq=seq, tb=tb, rmult=rmult, groups=groups, dk=dk, dv=dv,
                          has_bias=has_bias),
        out_shape=jax.ShapeDtypeStruct((batch * seq, heads * dv), q_arr.dtype),
        grid=(batch, heads // groups),
        in_specs=in_specs,
        out_specs=pl.BlockSpec((seq, gv), lambda b, h: (b, h)),
        compiler_params=_cparams(("parallel", "parallel"), vmem),
        name="flash_fox" if has_bias else "flash_mla",
    )(*args)


def _gla_kernel(q_ref, k_ref, v_ref, r_ref, ps_ref, wg2_ref, bg2_ref, nw_ref, o_ref, st_ref, *, tm):
    @pl.when(pl.program_id(1) == 0)
    def _():
        st_ref[...] = jnp.zeros_like(st_ref)

    nc = tm // GLA_CHUNK
    glin = _dot(ps_ref[...].astype(MXU_DTYPE), wg2_ref[...]) + bg2_ref[...]
    gl = _log_sigmoid(glin) / GLA_GATE_NORM
    tril, _, same = _chunk_masks(tm, GLA_CHUNK)
    gcum = _dot01_left(tril, gl)
    gend = _dot01_left(same, gl)
    q = q_ref[...].astype(F32)
    k = k_ref[...].astype(F32)
    qd = (q * jnp.exp(gcum)).astype(MXU_DTYPE)
    kd = (k * jnp.exp(-gcum)).astype(MXU_DTYPE)
    ku = (k * jnp.exp(gend - gcum)).astype(MXU_DTYPE)
    dec = jnp.exp(gend)
    nw = nw_ref[...]
    er = lax.broadcasted_iota(jnp.int32, (tm, nc * GLA_DK), 0)
    ec = lax.broadcasted_iota(jnp.int32, (tm, nc * GLA_DK), 1)
    own = jnp.right_shift(er, GLA_CHUNK.bit_length() - 1) == jnp.right_shift(ec, GLA_DK.bit_length() - 1)
    zero = jnp.zeros((), MXU_DTYPE)

    for h in range(GLA_HEADS):
        kc = slice(h * GLA_DK, (h + 1) * GLA_DK)
        vc = slice(h * GLA_DV, (h + 1) * GLA_DV)
        qd_h, kd_h, ku_h = qd[:, kc], kd[:, kc], ku[:, kc]
        v_h = v_ref[:, vc]
        att = jnp.where(tril, _dot_nt(qd_h, kd_h), 0.0)
        o = _dot(att.astype(MXU_DTYPE), v_h)
        qd_x = jnp.where(own, jnp.concatenate([qd_h] * nc, axis=1), zero)
        ku_x = jnp.where(own, jnp.concatenate([ku_h] * nc, axis=1), zero)
        upd = _dot_tn(v_h, ku_x)
        state = st_ref[:, kc]
        entering = []
        for c in range(nc):
            entering.append(state)
            state = state * dec[c * GLA_CHUNK:c * GLA_CHUNK + 1, kc] + upd[:, c * GLA_DK:(c + 1) * GLA_DK]
        st_ref[:, kc] = state
        s_all = jnp.concatenate(entering, axis=1).astype(MXU_DTYPE)
        o = o + _dot_nt(qd_x, s_all)
        o = _rmsnorm(o, nw)
        o_ref[:, vc] = (o * _silu(r_ref[:, vc].astype(F32))).astype(o_ref.dtype)


def _gla(p, ps, wg2, bg2, nw, *, batch, seq, tm):
    nt = seq // tm
    full = lambda a: pl.BlockSpec(a.shape, lambda b, i: (0,) * a.ndim)
    row = lambda b, i: b * nt + i
    return pl.pallas_call(
        functools.partial(_gla_kernel, tm=tm),
        out_shape=jax.ShapeDtypeStruct((p.shape[0], GLA_HEADS * GLA_DV), p.dtype),
        grid=(batch, nt),
        in_specs=[
            pl.BlockSpec((tm, 256), lambda b, i: (row(b, i), P_GLA_Q // 256)),
            pl.BlockSpec((tm, 256), lambda b, i: (row(b, i), P_GLA_K // 256)),
            pl.BlockSpec((tm, 512), lambda b, i: (row(b, i), P_GLA_V // 512)),
            pl.BlockSpec((tm, 512), lambda b, i: (row(b, i), P_GLA_R // 512)),
            pl.BlockSpec((tm, LANES), lambda b, i: (row(b, i), 0)),
            full(wg2), full(bg2), full(nw),
        ],
        out_specs=pl.BlockSpec((tm, GLA_HEADS * GLA_DV), lambda b, i: (row(b, i), 0)),
        scratch_shapes=[pltpu.VMEM((GLA_DV, GLA_HEADS * GLA_DK), F32)],
        compiler_params=_cparams(("parallel", "arbitrary"), 32 << 20),
        name="gla",
    )(p, p, p, p, ps, wg2, bg2, nw)


def _ssd_kernel(z_ref, x_ref, b_ref, c_ref, ps_ref, pst_ref, cw_ref, cb_ref, dtb_row_ref,
                dtb_col_ref, alog_row_ref, alog_col_ref, dskip_ref, nw_ref, exp_ref,
                o_ref, st_ref, tail_ref, *, tm):
    @pl.when(pl.program_id(1) == 0)
    def _():
        st_ref[...] = jnp.zeros_like(st_ref)
        tail_ref[...] = jnp.zeros_like(tail_ref)

    inner = SSD_INNER
    gw = inner // SSD_GROUPS
    xbc = jnp.concatenate([x_ref[...], b_ref[...], c_ref[...]], axis=-1).astype(F32)
    ext = jnp.concatenate([tail_ref[...], xbc], axis=0)
    tail_ref[...] = xbc[tm - SUBLANES:, :]
    conv = cb_ref[...]
    for j in range(SSD_CONV):
        shift = SSD_CONV - 1 - j
        src = ext if shift == 0 else pltpu.roll(ext, shift, axis=0)
        conv = conv + cw_ref[j:j + 1, :] * src[SUBLANES:, :]
    xc = _silu(conv)
    xs = xc[:, :inner]
    bm = xc[:, inner:inner + SSD_GROUPS * SSD_STATE].astype(MXU_DTYPE)
    cm = xc[:, inner + SSD_GROUPS * SSD_STATE:].astype(MXU_DTYPE)

    dt_exp = _dot01_right(_softplus(ps_ref[...] + dtb_row_ref[...]), exp_ref[...])
    a_row = -jnp.exp(alog_row_ref[...])
    dta = dt_exp * a_row
    tril, triu, same = _chunk_masks(tm, SSD_CHUNK)
    acum = _dot01_left(tril, dta)
    aend = _dot01_left(same, dta)
    dta_t = _softplus(pst_ref[...] + dtb_col_ref[...]) * (-jnp.exp(alog_col_ref[...]))
    acum_t = _dot01_right(dta_t, triu)
    x_dt = xs * dt_exp
    xw = (x_dt * jnp.exp(aend - acum)).astype(MXU_DTYPE)
    x_dt_m = x_dt.astype(MXU_DTYPE)
    e_acum = jnp.exp(acum)
    e_aend = jnp.exp(aend)
    cr = lax.broadcasted_iota(jnp.int32, (SSD_CHUNK, SSD_CHUNK), 0)
    cc = lax.broadcasted_iota(jnp.int32, (SSD_CHUNK, SSD_CHUNK), 1)
    causal = cr >= cc
    dskip = dskip_ref[...]
    nw = nw_ref[...]
    hg = SSD_HEADS // SSD_GROUPS

    for c in range(tm // SSD_CHUNK):
        r0 = c * SSD_CHUNK
        rows = slice(r0, r0 + SSD_CHUNK)
        for g in range(SSD_GROUPS):
            gl = slice(g * gw, (g + 1) * gw)
            nl = slice(g * SSD_STATE, (g + 1) * SSD_STATE)
            b_g, c_g = bm[rows, nl], cm[rows, nl]
            cb = _dot_nt(c_g, b_g)
            parts = []
            for hh in range(hg):
                h = g * hg + hh
                lane0 = h * SSD_HEADDIM
                colv = jnp.broadcast_to(acum[rows, lane0:lane0 + 1], (SSD_CHUNK, SSD_CHUNK))
                rowv = acum_t[h:h + 1, r0:r0 + SSD_CHUNK]
                decay = jnp.where(causal, jnp.exp(colv - rowv), 0.0)
                mmat = (cb * decay).astype(MXU_DTYPE)
                parts.append(_dot(mmat, x_dt_m[rows, lane0:lane0 + SSD_HEADDIM]))
            y_diag = jnp.concatenate(parts, axis=-1)
            state_t = st_ref[:, gl]
            y_off = _dot(c_g, state_t.astype(MXU_DTYPE)) * e_acum[rows, gl]
            st_ref[:, gl] = state_t * e_aend[r0:r0 + 1, gl] + _dot_tn(b_g, xw[rows, gl])
            y = y_diag + y_off + dskip[:, gl] * xs[rows, gl]
            y = y * _silu(z_ref[rows, gl].astype(F32))
            o_ref[rows, gl] = _rmsnorm(y, nw[:, gl]).astype(o_ref.dtype)


def _ssd(p, ps, pst, cw, cb, dtb_row, dtb_col, alog_row, alog_col, dskip, nw, expand, *, batch, seq, tm):
    nt = seq // tm
    full = lambda a: pl.BlockSpec(a.shape, lambda b, i: (0,) * a.ndim)
    row = lambda b, i: b * nt + i
    conv_dim = SSD_INNER + 2 * SSD_GROUPS * SSD_STATE
    return pl.pallas_call(
        functools.partial(_ssd_kernel, tm=tm),
        out_shape=jax.ShapeDtypeStruct((p.shape[0], SSD_INNER), p.dtype),
        grid=(batch, nt),
        in_specs=[
            pl.BlockSpec((tm, 512), lambda b, i: (row(b, i), P_SSD_Z // 512)),
            pl.BlockSpec((tm, 512), lambda b, i: (row(b, i), P_SSD_X // 512)),
            pl.BlockSpec((tm, 256), lambda b, i: (row(b, i), P_SSD_B // 256)),
            pl.BlockSpec((tm, 256), lambda b, i: (row(b, i), P_SSD_C // 256)),
            pl.BlockSpec((tm, LANES), lambda b, i: (row(b, i), 0)),
            pl.BlockSpec((SUBLANES, tm), lambda b, i: (PS_DT // SUBLANES, row(b, i))),
            full(cw), full(cb), full(dtb_row), full(dtb_col), full(alog_row), full(alog_col),
            full(dskip), full(nw), full(expand),
        ],
        out_specs=pl.BlockSpec((tm, SSD_INNER), lambda b, i: (row(b, i), 0)),
        scratch_shapes=[pltpu.VMEM((SSD_STATE, SSD_INNER), F32),
                        pltpu.VMEM((SUBLANES, conv_dim), F32)],
        compiler_params=_cparams(("parallel", "arbitrary"), 40 << 20),
        name="ssd",
    )(p, p, p, p, ps, pst, cw, cb, dtb_row, dtb_col, alog_row, alog_col, dskip, nw, expand)


def _merge_kernel(x_ref, g_ref, y0_ref, y1_ref, y2_ref, y3_ref, wsq_ref, wbr_ref, o_ref, h_ref, acc_ref,
                  *, tc):
    j = pl.program_id(1)
    d = x_ref.shape[-1]

    @pl.when(j == 0)
    def _():
        _norm_rows(x_ref, g_ref, h_ref)
        acc_ref[...] = jnp.zeros_like(acc_ref)

    @pl.when(j < N_BRANCH)
    def _():
        h = h_ref[...]
        y = jnp.where(j == 0, y0_ref[...], jnp.where(j == 1, y1_ref[...],
                                                      jnp.where(j == 2, y2_ref[...], y3_ref[...])))
        for c in range(d // tc):
            cols = slice(c * tc, (c + 1) * tc)
            gate = jax.nn.sigmoid(_dot(h, wsq_ref[:, cols]))
            acc_ref[:, cols] += gate * _dot(y, wbr_ref[:, cols])

    @pl.when(j == N_BRANCH)
    def _():
        merged = acc_ref[...].astype(MXU_DTYPE)
        for c in range(d // tc):
            cols = slice(c * tc, (c + 1) * tc)
            o_ref[:, cols] = x_ref[:, cols] + _dot(merged, wsq_ref[:, cols])


def _merge(x, g, ys, wsq, wbr, *, layer, tm, tc):
    m, d = x.shape
    assert len(ys) == N_BRANCH
    bw = ys[0].shape[-1]
    wbytes = jnp.dtype(wsq.dtype).itemsize
    vmem = 4 * tm * d * 4 + tm * d * wbytes + tm * d * 4 + 2 * d * d * wbytes \
        + 2 * bw * d * wbytes + 3 * N_BRANCH * tm * bw * wbytes + 4 * tm * tc * 4 + (4 << 20)
    last = N_BRANCH - 1
    y_spec = pl.BlockSpec((tm, bw), lambda i, j: (i, 0))
    return pl.pallas_call(
        functools.partial(_merge_kernel, tc=tc),
        out_shape=jax.ShapeDtypeStruct((m, d), F32),
        grid=(m // tm, N_BRANCH + 1),
        in_specs=[
            pl.BlockSpec((tm, d), lambda i, j: (i, 0)),
            pl.BlockSpec((1, d), lambda i, j: (0, 0)),
            y_spec, y_spec, y_spec, y_spec,
            pl.BlockSpec((None, d, d), lambda i, j: (layer, 0, j)),
            pl.BlockSpec((None, None, bw, d), lambda i, j: (layer, jnp.minimum(j, last), 0, 0)),
        ],
        out_specs=pl.BlockSpec((tm, d), lambda i, j: (i, 0)),
        scratch_shapes=[pltpu.VMEM((tm, d), wsq.dtype), pltpu.VMEM((tm, d), F32)],
        compiler_params=_cparams(("parallel", "arbitrary"), vmem),
        name="merge",
    )(x, g, *ys, wsq, wbr)


def _xattn_kernel(x_ref, g_ref, wq_ref, k_ref, v_ref, wo_ref, o_ref, h_ref, oc_ref, *, tc):
    _norm_rows(x_ref, g_ref, h_ref)
    h = h_ref[...]
    dh = XATTN_DH
    for hd in range(XATTN_HEADS):
        cols = slice(hd * dh, (hd + 1) * dh)
        q = _dot(h, wq_ref[:, cols]).astype(MXU_DTYPE)
        s = _dot_nt(q, k_ref[:, cols]) * (dh ** -0.5 * LOG2E)
        p = jnp.exp2(s - jnp.max(s, axis=-1, keepdims=True))
        p = p / jnp.sum(p, axis=-1, keepdims=True)
        oc_ref[:, cols] = _dot(p.astype(MXU_DTYPE), v_ref[:, cols]).astype(oc_ref.dtype)
    oc = oc_ref[...]
    d = x_ref.shape[-1]
    for c in range(d // tc):
        cols = slice(c * tc, (c + 1) * tc)
        o_ref[:, cols] = x_ref[:, cols] + _dot(oc, wo_ref[:, cols])


def _xattn(x, g, wq, kv, wo, *, layer, seq, n_mem, tm, tc):
    m, d = x.shape
    wbytes = jnp.dtype(wq.dtype).itemsize
    vmem = 4 * tm * d * 4 + 2 * tm * d * wbytes + 2 * d * d * wbytes + 4 * n_mem * d * wbytes \
        + 3 * tm * XATTN_DH * 4 + 3 * tm * n_mem * 4 + 2 * tm * tc * 4 + (6 << 20)
    tiles_per_seq = seq // tm
    return pl.pallas_call(
        functools.partial(_xattn_kernel, tc=tc),
        out_shape=jax.ShapeDtypeStruct((m, d), F32),
        grid=(m // tm,),
        in_specs=[
            pl.BlockSpec((tm, d), lambda i: (i, 0)),
            pl.BlockSpec((1, d), lambda i: (0, 0)),
            pl.BlockSpec((None, d, d), lambda i: (layer, 0, 0), pipeline_mode=pl.Buffered(1)),
            pl.BlockSpec((n_mem, d), lambda i: (i // tiles_per_seq, 0)),
            pl.BlockSpec((n_mem, d), lambda i: (i // tiles_per_seq, 1)),
            pl.BlockSpec((None, d, d), lambda i: (layer, 0, 0), pipeline_mode=pl.Buffered(1)),
        ],
        out_specs=pl.BlockSpec((tm, d), lambda i: (i, 0)),
        scratch_shapes=[pltpu.VMEM((tm, d), wq.dtype), pltpu.VMEM((tm, d), wq.dtype)],
        compiler_params=_cparams(("parallel",), vmem),
        name="xattn",
    )(x, g, wq, kv, kv, wo)


_IN_SIZES = (N_BRANCH * D_MODEL, MLA_Q_RANK, MLA_KV_RANK, MLA_ROPE, 3 * FOX_HEADS * FOX_DH, FOX_HEADS,
             GLA_HEADS * GLA_DK, GLA_HEADS * GLA_DK, GLA_HEADS * GLA_DV, GLA_GATE_RANK,
             GLA_HEADS * GLA_DV, SSD_INNER, SSD_INNER + 2 * SSD_GROUPS * SSD_STATE, SSD_HEADS)


def _rot_cols(w):
    half = w.shape[-1] // 2
    return jnp.concatenate([-w[..., half:], w[..., :half]], axis=-1)


def _prep_w_in(w_mix):
    w_in = w_mix.astype(MXU_DTYPE)
    nl, d = w_in.shape[:2]
    offs = np.cumsum((0,) + _IN_SIZES[1:])
    assert offs[-1] == w_in.shape[2]
    seg = lambda i: w_in[:, :, offs[i]:offs[i + 1]]
    (cq, ckv, kr, fox, fox_f, gq, gk, gv, glr, gr, sz, sxbc, sdt) = [seg(i) for i in range(len(_IN_SIZES) - 1)]
    z = lambda n: jnp.zeros((nl, d, n), w_in.dtype)
    w_p = jnp.concatenate([
        cq, z(512 - MLA_Q_RANK), ckv, kr, z(LANES - MLA_ROPE), _rot_cols(kr), z(LANES - MLA_ROPE),
        fox, gq, gk, gv, gr, sz, sxbc], axis=2)
    assert w_p.shape[2] == P_WIDTH
    cs = np.ones((1, P_WIDTH), np.float32)
    cs[:, P_FOX_Q:P_FOX_K] = FOX_DH ** -0.5 * LOG2E
    cs[:, P_GLA_Q:P_GLA_K] = GLA_DK ** -0.5
    w_s = jnp.concatenate([glr, sdt, fox_f, z(LANES - GLA_GATE_RANK - SSD_HEADS - FOX_HEADS)], axis=2)
    return w_p.astype(MXU_DTYPE), jnp.asarray(cs), w_s.astype(MXU_DTYPE)


def _prep_mla(w_uq, w_ukv):
    nl, r = w_uq.shape[:2]
    wq = w_uq.reshape(nl, r, MLA_HEADS, MLA_NOPE + MLA_ROPE)
    nope, pe = wq[..., :MLA_NOPE], wq[..., MLA_NOPE:]
    zpad = jnp.zeros((nl, r, MLA_HEADS, MLA_DK_PAD - MLA_NOPE - MLA_ROPE), w_uq.dtype)
    wa = jnp.concatenate([nope, pe, zpad], axis=-1).reshape(nl, r, MLA_HEADS * MLA_DK_PAD)
    wb = jnp.concatenate([_rot_cols(pe), zpad], axis=-1).reshape(nl, r, MLA_HEADS * LANES)
    rk = w_ukv.shape[1]
    wkv = w_ukv.reshape(nl, rk, MLA_HEADS, MLA_NOPE + MLA_V)
    wkv = jnp.concatenate([wkv[..., :MLA_NOPE].reshape(nl, rk, -1), wkv[..., MLA_NOPE:].reshape(nl, rk, -1)],
                          axis=2)
    return wa.astype(MXU_DTYPE), wb.astype(MXU_DTYPE), wkv.astype(MXU_DTYPE)


def _tiles(m, seq):
    mid = min(512, seq)
    return dict(ffn_tm=mid, ffn_tf=512, proj_tm=mid, proj_tn=2816, prep_tm=mid,
                flash_tb=mid, flash_rmult=min(2, seq // mid), gla_tm=min(512, seq), ssd_tm=min(256, seq), merge_tm=mid, merge_tc=512,
                xattn_tm=mid, xattn_tc=512, rope_tm=min(1024, m))


@jax.jit
def kernel(x, mem, positions, ffn1_norm, ffn1_w_gate, ffn1_w_up, ffn1_w_down, mix_norm, w_in, mla_q_norm, mla_kv_norm, mla_w_uq, mla_w_ukv, fox_f_bias, gla_w_g2, gla_b_g2, gla_norm, ssd_conv_w, ssd_conv_b, ssd_dt_bias, ssd_a_log, ssd_d, ssd_norm, w_branch, w_out, xattn_norm, mem_norm, xattn_w_q, xattn_w_k, xattn_w_v, xattn_w_o, ffn2_norm, ffn2_w_gate, ffn2_w_up, ffn2_w_down, final_norm):
    batch, seq, d = x.shape
    n_mem = mem.shape[1]
    m = batch * seq
    t = _tiles(m, seq)
    depth = w_in.shape[0]
    cast = lambda a: a.astype(MXU_DTYPE)
    row = lambda a: a.reshape(1, -1).astype(F32)
    col8 = lambda v: jnp.zeros((SUBLANES, 1), F32).at[:v.shape[0], 0].set(v.astype(F32))

    xs = x.reshape(m, d)
    mem2 = mem.reshape(batch * n_mem, d)
    ct, st = _rope_tables(positions.reshape(m, 1), tm=t["rope_tm"])

    expand = np.zeros((LANES, SSD_INNER), np.float32)
    for h in range(SSD_HEADS):
        expand[PS_DT + h, h * SSD_HEADDIM:(h + 1) * SSD_HEADDIM] = 1.0
    expand = jnp.asarray(expand, MXU_DTYPE)
    rep = lambda v: jnp.repeat(v.astype(F32), SSD_HEADDIM).reshape(1, SSD_INNER)
    ffn = functools.partial(_ffn, tm=t["ffn_tm"], tf=t["ffn_tf"])
    flash = functools.partial(_flash, batch=batch, seq=seq, groups=2, tb=t["flash_tb"], rmult=t["flash_rmult"])

    f1g, f1u, f1d = cast(ffn1_w_gate), cast(ffn1_w_up), cast(ffn1_w_down)
    f2g, f2u, f2d = cast(ffn2_w_gate), cast(ffn2_w_up), cast(ffn2_w_down)
    w_p, cs, w_s = _prep_w_in(w_in[:, :, N_BRANCH * d:])
    wa, wb, wkv = _prep_mla(mla_w_uq, mla_w_ukv)
    wsq = jnp.concatenate([cast(w_in[:, :, :N_BRANCH * d]), cast(w_out)], axis=2)
    wbr = cast(w_branch)
    w_kv = cast(jnp.concatenate([xattn_w_k, xattn_w_v], axis=2))
    wq_x, wo_x = cast(xattn_w_q), cast(xattn_w_o)

    for l in range(depth):
        xs = ffn(xs, row(ffn1_norm[l]), f1g, f1u, f1d, row(final_norm), layer=l, final_norm=False)

        p, ps, pst = _proj(xs, row(mix_norm[l]), w_p, cs, w_s, layer=l, tm=t["proj_tm"], tn=t["proj_tn"])

        q_mla, k_mla, v_mla = _mla_prep(p, ct, st, row(mla_q_norm[l]), row(mla_kv_norm[l]), wa, wb, wkv,
                                        layer=l, tm=t["prep_tm"])
        y_mla = flash(q_mla, k_mla, v_mla, None, heads=MLA_HEADS,
                      dk=MLA_DK_PAD, dv=MLA_V, q_off=0, k_off=0, v_off=0)

        c_fox = _fox_cum(pst, col8(fox_f_bias[l]), batch=batch, seq=seq)
        y_fox = flash(p, p, p, c_fox, heads=FOX_HEADS,
                      dk=FOX_DH, dv=FOX_DH, q_off=P_FOX_Q, k_off=P_FOX_K, v_off=P_FOX_V)

        wg2 = jnp.zeros((LANES, GLA_HEADS * GLA_DK), F32).at[PS_GLR:PS_GLR + GLA_GATE_RANK].set(gla_w_g2[l])
        y_gla = _gla(p, ps, cast(wg2), row(gla_b_g2[l]), row(gla_norm[l]), batch=batch, seq=seq, tm=t["gla_tm"])

        dtb_row = jnp.zeros((1, LANES), F32).at[0, PS_DT:PS_DT + SSD_HEADS].set(ssd_dt_bias[l])
        y_ssd = _ssd(p, ps, pst, ssd_conv_w[l].astype(F32), row(ssd_conv_b[l]), dtb_row, col8(ssd_dt_bias[l]),
                     rep(ssd_a_log[l]), col8(ssd_a_log[l]), rep(ssd_d[l]), row(ssd_norm[l]), expand,
                     batch=batch, seq=seq, tm=t["ssd_tm"])

        xs = _merge(xs, row(mix_norm[l]), (y_mla, y_fox, y_gla, y_ssd), wsq, wbr, layer=l, tm=t["merge_tm"], tc=t["merge_tc"])

        kv = _norm_matmul(mem2, row(mem_norm[l]), w_kv, layer=l, tm=min(512, batch * n_mem), tn=1024)
        xs = _xattn(xs, row(xattn_norm[l]), wq_x, kv, wo_x, layer=l,
                    seq=seq, n_mem=n_mem, tm=t["xattn_tm"], tc=t["xattn_tc"])

        xs = ffn(xs, row(ffn2_norm[l]), f2g, f2u, f2d, row(final_norm), layer=l, final_norm=(l == depth - 1))

    return xs.reshape(batch, seq, d)
```

```python
import functools

import numpy as np
import jax
import jax.numpy as jnp
from jax import lax
from jax.experimental import pallas as pl
from jax.experimental.pallas import tpu as pltpu

D_MODEL = 2048
DEPTH = 4
EPS = 1e-6
N_BRANCH = 4
BRANCH_WIDTH = 512
MLA_HEADS, MLA_NOPE, MLA_ROPE, MLA_V = 4, 128, 64, 128
MLA_Q_RANK, MLA_KV_RANK = 384, 256
MLA_DK_PAD = 256
ROPE_THETA = 10000.0
FOX_HEADS, FOX_DH = 4, 128
GLA_HEADS, GLA_DK, GLA_DV = 4, 64, 128
GLA_GATE_RANK, GLA_GATE_NORM, GLA_CHUNK = 16, 16.0, 64
SSD_HEADS, SSD_HEADDIM, SSD_GROUPS, SSD_STATE, SSD_CONV, SSD_CHUNK = 8, 64, 2, 128, 4, 128
SSD_INNER = SSD_HEADS * SSD_HEADDIM
XATTN_HEADS = 4
XATTN_DH = D_MODEL // XATTN_HEADS
D_FF = 5632

LANES = 128
SUBLANES = 8
V7X_VMEM_LIMIT_BYTES = 60 * 1024 * 1024

MXU_DTYPE = jnp.bfloat16
FFN_SLOTS = 3
F32 = jnp.float32
LOG2E = 1.4426950408889634

P_CQ, P_CKV, P_KR, P_KRROT = 0, 512, 768, 896
P_FOX_Q, P_FOX_K, P_FOX_V = 1024, 1536, 2048
P_GLA_Q, P_GLA_K, P_GLA_V, P_GLA_R = 2560, 2816, 3072, 3584
P_SSD_Z, P_SSD_X, P_SSD_B, P_SSD_C = 4096, 4608, 5120, 5376
P_WIDTH = 5632
PS_GLR, PS_DT, PS_F = 0, 16, 24
PST_ROWS = 32


def _cparams(sem, vmem_bytes):
    return pltpu.CompilerParams(dimension_semantics=sem,
                                vmem_limit_bytes=int(min(vmem_bytes, V7X_VMEM_LIMIT_BYTES)))


def _dot(a, b):
    return jnp.dot(a, b, preferred_element_type=F32)


def _dot_nt(a, b):
    return lax.dot_general(a, b, (((1,), (1,)), ((), ())), preferred_element_type=F32)


def _dot_tn(a, b):
    return lax.dot_general(a, b, (((0,), (0,)), ((), ())), preferred_element_type=F32)


def _split3(x):
    hi = x.astype(MXU_DTYPE)
    r1 = x - hi.astype(F32)
    mid = r1.astype(MXU_DTYPE)
    lo = (r1 - mid.astype(F32)).astype(MXU_DTYPE)
    return hi, mid, lo


def _dot01_right(x, m01):
    if MXU_DTYPE == F32:
        return _dot(x, m01.astype(F32))
    m = m01.astype(MXU_DTYPE)
    hi, mid, lo = _split3(x)
    return _dot(hi, m) + _dot(mid, m) + _dot(lo, m)


def _dot01_left_multi(m01s, x):
    if MXU_DTYPE == F32:
        return [_dot(m01.astype(F32), x) for m01 in m01s]
    parts = _split3(x)
    out = []
    for m01 in m01s:
        m = m01.astype(MXU_DTYPE)
        out.append(_dot(m, parts[0]) + _dot(m, parts[1]) + _dot(m, parts[2]))
    return out


def _dot01_left(m01, x):
    return _dot01_left_multi((m01,), x)[0]


def _rmsnorm(x, g):
    ms = jnp.mean(x * x, axis=-1, keepdims=True)
    return x * lax.rsqrt(ms + EPS) * g


def _log_sigmoid(x):
    return jnp.minimum(x, 0.0) - jnp.log1p(jnp.exp(-jnp.abs(x)))


def _softplus(x):
    return jnp.maximum(x, 0.0) + jnp.log1p(jnp.exp(-jnp.abs(x)))


def _silu(x):
    return x * jax.nn.sigmoid(x)


def _chunk_masks(n, chunk):
    shift = chunk.bit_length() - 1
    assert 1 << shift == chunk
    r = lax.broadcasted_iota(jnp.int32, (n, n), 0)
    c = lax.broadcasted_iota(jnp.int32, (n, n), 1)
    same = jnp.right_shift(r, shift) == jnp.right_shift(c, shift)
    return jnp.logical_and(same, r >= c), jnp.logical_and(same, r <= c), same


def _norm_rows(x_ref, g_ref, h_ref, copy_ref=None, *, rows=256):
    tm = x_ref.shape[0]
    rows = min(rows, tm)
    g = g_ref[...]
    for r in range(tm // rows):
        sl = slice(r * rows, (r + 1) * rows)
        x = x_ref[sl, :]
        h_ref[sl, :] = _rmsnorm(x, g).astype(h_ref.dtype)
        if copy_ref is not None:
            copy_ref[sl, :] = x


def _ffn_kernel(x_ref, g_ref, wg_hbm, wu_hbm, wd_hbm, fg_ref, o_ref, h_ref, wg_buf, wu_buf, wd_buf, sem,
                *, final_norm, layer, tf, n_chunks):
    i = pl.program_id(0)

    def chunk_copies(j, slot):
        cols = pl.ds(j * tf, tf)
        return (pltpu.make_async_copy(wg_hbm.at[layer, :, cols], wg_buf.at[slot], sem.at[0, slot]),
                pltpu.make_async_copy(wu_hbm.at[layer, :, cols], wu_buf.at[slot], sem.at[1, slot]),
                pltpu.make_async_copy(wd_hbm.at[layer, cols, :], wd_buf.at[slot], sem.at[2, slot]))

    def start(j, slot):
        for cp in chunk_copies(j, slot):
            cp.start()

    @pl.when(i == 0)
    def _():
        start(0, 0)

    _norm_rows(x_ref, g_ref, h_ref, o_ref)
    h = h_ref[...]
    for j in range(n_chunks):
        slot = j % FFN_SLOTS
        for cp in chunk_copies(j, slot):
            cp.wait()
        if j + 1 < n_chunks:
            start(j + 1, (j + 1) % FFN_SLOTS)
        else:
            @pl.when(i + 1 < pl.num_programs(0))
            def _():
                start(0, 0)
        a = _dot(h, wg_buf[slot])
        u = _dot(h, wu_buf[slot])
        t = (_silu(a) * (0.5 * u)).astype(MXU_DTYPE)
        o_ref[...] += _dot(t, wd_buf[slot])

    if final_norm:
        _norm_rows(o_ref, fg_ref, o_ref)


def _ffn(x, g, wg, wu, wd, fg, *, layer, final_norm, tm, tf):
    m, d = x.shape
    f = wg.shape[-1]
    n_chunks = f // tf
    assert n_chunks * tf == f and n_chunks > FFN_SLOTS and (n_chunks - 1) % FFN_SLOTS != 0
    wb = jnp.dtype(wg.dtype).itemsize
    vmem = 4 * tm * d * 4 + tm * d * wb + FFN_SLOTS * 3 * d * tf * wb + 4 * tm * tf * 4 + tm * d * 4 + (4 << 20)
    return pl.pallas_call(
        functools.partial(_ffn_kernel, final_norm=final_norm, layer=layer, tf=tf, n_chunks=n_chunks),
        out_shape=jax.ShapeDtypeStruct((m, d), F32),
        grid=(m // tm,),
        in_specs=[
            pl.BlockSpec((tm, d), lambda i: (i, 0)),
            pl.BlockSpec((1, d), lambda i: (0, 0)),
            pl.BlockSpec(memory_space=pl.ANY),
            pl.BlockSpec(memory_space=pl.ANY),
            pl.BlockSpec(memory_space=pl.ANY),
            pl.BlockSpec((1, d), lambda i: (0, 0)),
        ],
        out_specs=pl.BlockSpec((tm, d), lambda i: (i, 0)),
        scratch_shapes=[pltpu.VMEM((tm, d), wg.dtype),
                        pltpu.VMEM((FFN_SLOTS, d, tf), wg.dtype),
                        pltpu.VMEM((FFN_SLOTS, d, tf), wg.dtype),
                        pltpu.VMEM((FFN_SLOTS, tf, d), wg.dtype),
                        pltpu.SemaphoreType.DMA((3, FFN_SLOTS))],
        compiler_params=_cparams(("arbitrary",), vmem),
        name="ffn",
    )(x, g, wg, wu, wd, fg)


def _proj_kernel(x_ref, g_ref, w_ref, cs_ref, ws_ref, p_ref, ps_ref, pst_ref, h_ref):
    j = pl.program_id(1)

    @pl.when(j == 0)
    def _():
        _norm_rows(x_ref, g_ref, h_ref)
        ps = _dot(h_ref[...], ws_ref[...])
        ps_ref[...] = ps
        pst_ref[...] = jnp.transpose(ps)[:PST_ROWS, :]

    p_ref[...] = (_dot(h_ref[...], w_ref[...]) * cs_ref[...]).astype(p_ref.dtype)


def _proj(x, g, w, cs, ws, *, layer, tm, tn):
    m, d = x.shape
    n = w.shape[-1]
    wb = jnp.dtype(w.dtype).itemsize
    vmem = 2 * tm * d * 4 + tm * d * wb + 2 * d * tn * wb + 2 * tm * tn * wb + 2 * tm * tn * 4 + (6 << 20)
    return pl.pallas_call(
        _proj_kernel,
        out_shape=(jax.ShapeDtypeStruct((m, n), w.dtype),
                   jax.ShapeDtypeStruct((m, LANES), F32),
                   jax.ShapeDtypeStruct((PST_ROWS, m), F32)),
        grid=(m // tm, n // tn),
        in_specs=[
            pl.BlockSpec((tm, d), lambda i, j: (i, 0)),
            pl.BlockSpec((1, d), lambda i, j: (0, 0)),
            pl.BlockSpec((None, d, tn), lambda i, j: (layer, 0, j)),
            pl.BlockSpec((1, tn), lambda i, j: (0, j)),
            pl.BlockSpec((None, d, LANES), lambda i, j: (layer, 0, 0)),
        ],
        out_specs=(pl.BlockSpec((tm, tn), lambda i, j: (i, j)),
                   pl.BlockSpec((tm, LANES), lambda i, j: (i, 0)),
                   pl.BlockSpec((PST_ROWS, tm), lambda i, j: (0, i))),
        scratch_shapes=[pltpu.VMEM((tm, d), w.dtype)],
        compiler_params=_cparams(("parallel", "arbitrary"), vmem),
        name="proj",
    )(x, g, w, cs, ws)


def _norm_matmul_kernel(x_ref, g_ref, w_ref, o_ref):
    h = _rmsnorm(x_ref[...], g_ref[...]).astype(w_ref.dtype)
    o_ref[...] = _dot(h, w_ref[...]).astype(o_ref.dtype)


def _norm_matmul(x, g, w, *, layer, tm, tn):
    m, d = x.shape
    n = w.shape[-1]
    wb = jnp.dtype(w.dtype).itemsize
    vmem = 3 * tm * d * 4 + 2 * d * tn * wb + 3 * tm * tn * 4 + (4 << 20)
    return pl.pallas_call(
        _norm_matmul_kernel,
        out_shape=jax.ShapeDtypeStruct((m, n), w.dtype),
        grid=(m // tm, n // tn),
        in_specs=[pl.BlockSpec((tm, d), lambda i, j: (i, 0)),
                  pl.BlockSpec((1, d), lambda i, j: (0, 0)),
                  pl.BlockSpec((None, d, tn), lambda i, j: (layer, 0, j))],
        out_specs=pl.BlockSpec((tm, tn), lambda i, j: (i, j)),
        compiler_params=_cparams(("parallel", "parallel"), vmem),
        name="mem_kv",
    )(x, g, w)


def _rope_kernel(pos_ref, ct_ref, st_ref):
    pos = pos_ref[...].astype(F32)
    lane = lax.broadcasted_iota(jnp.int32, (1, LANES), 1)
    half = MLA_ROPE // 2
    f = jnp.bitwise_and(lane, half - 1).astype(F32)
    inv_freq = jnp.power(jnp.float32(ROPE_THETA), -f / half)
    ang = pos * inv_freq
    valid = lane < MLA_ROPE
    ct_ref[...] = jnp.where(valid, jnp.cos(ang), 0.0)
    st_ref[...] = jnp.where(valid, jnp.sin(ang), 0.0)


def _rope_tables(pos_col, *, tm):
    m = pos_col.shape[0]
    return pl.pallas_call(
        _rope_kernel,
        out_shape=(jax.ShapeDtypeStruct((m, LANES), F32), jax.ShapeDtypeStruct((m, LANES), F32)),
        grid=(m // tm,),
        in_specs=[pl.BlockSpec((tm, 1), lambda i: (i, 0))],
        out_specs=(pl.BlockSpec((tm, LANES), lambda i: (i, 0)),
                   pl.BlockSpec((tm, LANES), lambda i: (i, 0))),
        compiler_params=_cparams(("parallel",), 16 << 20),
        name="rope_tables",
    )(pos_col)


def _mla_prep_kernel(cq_ref, ckv_ref, kr_ref, krrot_ref, ct_ref, st_ref, qn_ref, kvn_ref,
                     wa_ref, wb_ref, wkv_ref, q_ref, k_ref, v_ref):
    ct = ct_ref[...]
    st = st_ref[...]
    scale = (MLA_NOPE + MLA_ROPE) ** -0.5 * LOG2E

    ckv = ckv_ref[...].astype(F32)
    ckvn = _rmsnorm(ckv, kvn_ref[...]).astype(MXU_DTYPE)
    kv = _dot(ckvn, wkv_ref[...])
    kpe = kr_ref[...].astype(F32) * ct + krrot_ref[...].astype(F32) * st
    kpe = kpe.astype(k_ref.dtype)
    hw = MLA_HEADS * MLA_NOPE
    for h in range(MLA_HEADS):
        base = h * MLA_DK_PAD
        k_ref[:, base:base + MLA_NOPE] = kv[:, h * MLA_NOPE:(h + 1) * MLA_NOPE].astype(k_ref.dtype)
        k_ref[:, base + MLA_NOPE:base + MLA_DK_PAD] = kpe
    v_ref[...] = kv[:, hw:].astype(v_ref.dtype)

    cq = cq_ref[...].astype(F32)[:, :MLA_Q_RANK]
    cqn = _rmsnorm(cq, qn_ref[...]).astype(MXU_DTYPE)
    qa = _dot(cqn, wa_ref[...])
    qb = _dot(cqn, wb_ref[...])
    for h in range(MLA_HEADS):
        base = h * MLA_DK_PAD
        q_ref[:, base:base + MLA_NOPE] = (qa[:, base:base + MLA_NOPE] * scale).astype(q_ref.dtype)
        pe = qa[:, base + MLA_NOPE:base + MLA_DK_PAD] * ct + qb[:, h * LANES:(h + 1) * LANES] * st
        q_ref[:, base + MLA_NOPE:base + MLA_DK_PAD] = (pe * scale).astype(q_ref.dtype)


def _mla_prep(p, ct, st, qn, kvn, wa, wb, wkv, *, layer, tm):
    m = p.shape[0]
    qk_w = MLA_HEADS * MLA_DK_PAD
    v_w = MLA_HEADS * MLA_V
    full = lambda a: pl.BlockSpec(a.shape, lambda i: (0,) * a.ndim)
    of_layer = lambda a: pl.BlockSpec((None,) + a.shape[1:], lambda i: (layer,) + (0,) * (a.ndim - 1))
    return pl.pallas_call(
        _mla_prep_kernel,
        out_shape=(jax.ShapeDtypeStruct((m, qk_w), p.dtype),
                   jax.ShapeDtypeStruct((m, qk_w), p.dtype),
                   jax.ShapeDtypeStruct((m, v_w), p.dtype)),
        grid=(m // tm,),
        in_specs=[
            pl.BlockSpec((tm, 512), lambda i: (i, P_CQ // 512)),
            pl.BlockSpec((tm, 256), lambda i: (i, P_CKV // 256)),
            pl.BlockSpec((tm, 128), lambda i: (i, P_KR // 128)),
            pl.BlockSpec((tm, 128), lambda i: (i, P_KRROT // 128)),
            pl.BlockSpec((tm, LANES), lambda i: (i, 0)),
            pl.BlockSpec((tm, LANES), lambda i: (i, 0)),
            full(qn), full(kvn), of_layer(wa), of_layer(wb), of_layer(wkv),
        ],
        out_specs=(pl.BlockSpec((tm, qk_w), lambda i: (i, 0)),
                   pl.BlockSpec((tm, qk_w), lambda i: (i, 0)),
                   pl.BlockSpec((tm, v_w), lambda i: (i, 0))),
        compiler_params=_cparams(("parallel",), 40 << 20),
        name="mla_prep",
    )(p, p, p, p, ct, st, qn, kvn, wa, wb, wkv)


def _fox_cum_kernel(ft_ref, fb_ref, c_ref, *, seq):
    lf = _log_sigmoid(ft_ref[...] + fb_ref[...])
    r = lax.broadcasted_iota(jnp.int32, (LANES, LANES), 0)
    c = lax.broadcasted_iota(jnp.int32, (LANES, LANES), 1)
    upper = r <= c
    carry = jnp.zeros((SUBLANES, 1), F32)
    for blk in range(seq // LANES):
        sl = slice(blk * LANES, (blk + 1) * LANES)
        cs = _dot01_right(lf[:, sl], upper) + carry
        for h in range(FOX_HEADS):
            c_ref[h, :, sl] = cs[h:h + 1, :] * LOG2E
        carry = cs[:, LANES - 1:LANES]


def _fox_cum(pst, fb_col, *, batch, seq):
    return pl.pallas_call(
        functools.partial(_fox_cum_kernel, seq=seq),
        out_shape=jax.ShapeDtypeStruct((batch * FOX_HEADS, 1, seq), F32),
        grid=(batch,),
        in_specs=[pl.BlockSpec((SUBLANES, seq), lambda b: (PS_F // SUBLANES, b)),
                  pl.BlockSpec((SUBLANES, 1), lambda b: (0, 0))],
        out_specs=pl.BlockSpec((FOX_HEADS, 1, seq), lambda b: (b, 0, 0)),
        compiler_params=_cparams(("parallel",), 16 << 20),
        name="fox_cum",
    )(pst, fb_col)


def _flash_kernel(*refs, seq, tb, rmult, groups, dk, dv, has_bias):
    refs = list(refs)
    q_ref, k_ref, v_ref = refs[:3]
    c_ref = refs[3] if has_bias else None
    o_ref = refs[3 + int(has_bias)]
    rows_st = tb * rmult

    def update(state, s, v_blk):
        m, l, acc = state
        m_new = jnp.maximum(m, jnp.max(s, axis=-1, keepdims=True))
        alpha = jnp.exp2(m - m_new)
        p = jnp.exp2(s - m_new)
        l = alpha * l + jnp.sum(p, axis=-1, keepdims=True)
        acc = alpha * acc + _dot(p.astype(MXU_DTYPE), v_blk)
        return m_new, l, acc

    def super_tile(pj, unused):
        r0 = pl.multiple_of(pj * rows_st, rows_st)
        qs = [q_ref[pl.ds(r0, rows_st), g * dk:(g + 1) * dk] for g in range(groups)]
        if has_bias:
            c0 = [jnp.max(c_ref[g, :, pl.ds(r0, tb)], axis=-1, keepdims=True) for g in range(groups)]

        def scores(q, g, start, width=tb):
            s = _dot_nt(q, k_ref[pl.ds(start, width), g * dk:(g + 1) * dk])
            if has_bias:
                s = s + (c0[g] - c_ref[g, :, pl.ds(start, width)])
            return s

        def v_block(g, start, width=tb):
            return v_ref[pl.ds(start, width), g * dv:(g + 1) * dv]

        def body(ki, carry):
            start = pl.multiple_of(ki * rows_st, rows_st)
            return tuple(update(carry[g], scores(qs[g], g, start, rows_st), v_block(g, start, rows_st))
                         for g in range(groups))

        init = tuple((jnp.full((rows_st, 1), -jnp.inf, F32), jnp.zeros((rows_st, 1), F32),
                      jnp.zeros((rows_st, dv), F32)) for _ in range(groups))
        carry = lax.fori_loop(0, pj, body, init)

        for g in range(groups):
            m, l, acc = carry[g]
            for a in range(rmult):
                rs = slice(a * tb, (a + 1) * tb)
                width = (a + 1) * tb
                row = lax.broadcasted_iota(jnp.int32, (tb, width), 0)
                col = lax.broadcasted_iota(jnp.int32, (tb, width), 1)
                s = jnp.where(col <= row + a * tb, scores(qs[g][rs], g, r0, width), -jnp.inf)
                state = update((m[rs], l[rs], acc[rs]), s, v_block(g, r0, width))
                o_ref[pl.ds(pl.multiple_of(r0 + a * tb, tb), tb), g * dv:(g + 1) * dv] = \
                    (state[2] / state[1]).astype(o_ref.dtype)
        return unused

    lax.fori_loop(0, seq // rows_st, super_tile, 0)


def _flash(q_arr, k_arr, v_arr, c_arr, *, batch, seq, heads, groups, dk, dv,
           q_off, k_off, v_off, tb, rmult):
    rows_st = tb * rmult
    has_bias = c_arr is not None
    gk, gv = groups * dk, groups * dv
    in_specs = [pl.BlockSpec((seq, gk), lambda b, h: (b, q_off // gk + h)),
                pl.BlockSpec((seq, gk), lambda b, h: (b, k_off // gk + h)),
                pl.BlockSpec((seq, gv), lambda b, h: (b, v_off // gv + h))]
    args = [q_arr, k_arr, v_arr]
    if has_bias:
        in_specs.append(pl.BlockSpec((groups, 1, seq), lambda b, h: (b * (heads // groups) + h, 0, 0)))
        args.append(c_arr)
    vmem = 2 * seq * (2 * gk + 2 * gv) * 2 + groups * (3 * rows_st * rows_st * 4 + 4 * rows_st * dv * 4) + (8 << 20)
    return pl.pallas_call(
        functools.partial(_flash_kernel, seq=seq, tb=tb, rmult=rmult, groups=groups, dk=dk, dv=dv,
                          has_bias=has_bias),
        out_shape=jax.ShapeDtypeStruct((batch * seq, heads * dv), q_arr.dtype),
        grid=(batch, heads // groups),
        in_specs=in_specs,
        out_specs=pl.BlockSpec((seq, gv), lambda b, h: (b, h)),
        compiler_params=_cparams(("parallel", "parallel"), vmem),
        name="flash_fox" if has_bias else "flash_mla",
    )(*args)


def _gla_kernel(q_ref, k_ref, v_ref, r_ref, ps_ref, wg2_ref, bg2_ref, nw_ref, o_ref, st_ref, *, tm):
    @pl.when(pl.program_id(1) == 0)
    def _():
        st_ref[...] = jnp.zeros_like(st_ref)

    nc = tm // GLA_CHUNK
    glin = _dot(ps_ref[...].astype(MXU_DTYPE), wg2_ref[...]) + bg2_ref[...]
    gl = _log_sigmoid(glin) / GLA_GATE_NORM
    tril, _, same = _chunk_masks(tm, GLA_CHUNK)
    gcum, gend = _dot01_left_multi((tril, same), gl)
    q = q_ref[...].astype(F32)
    k = k_ref[...].astype(F32)
    qd = (q * jnp.exp(gcum)).astype(MXU_DTYPE)
    kd = (k * jnp.exp(-gcum)).astype(MXU_DTYPE)
    ku = (k * jnp.exp(gend - gcum)).astype(MXU_DTYPE)
    dec = jnp.exp(gend)
    nw = nw_ref[...]
    er = lax.broadcasted_iota(jnp.int32, (tm, nc * GLA_DK), 0)
    ec = lax.broadcasted_iota(jnp.int32, (tm, nc * GLA_DK), 1)
    own = jnp.right_shift(er, GLA_CHUNK.bit_length() - 1) == jnp.right_shift(ec, GLA_DK.bit_length() - 1)
    zero = jnp.zeros((), MXU_DTYPE)

    for h in range(GLA_HEADS):
        kc = slice(h * GLA_DK, (h + 1) * GLA_DK)
        vc = slice(h * GLA_DV, (h + 1) * GLA_DV)
        qd_h, kd_h, ku_h = qd[:, kc], kd[:, kc], ku[:, kc]
        v_h = v_ref[:, vc]
        att = jnp.where(tril, _dot_nt(qd_h, kd_h), 0.0)
        o = _dot(att.astype(MXU_DTYPE), v_h)
        qd_x = jnp.where(own, jnp.concatenate([qd_h] * nc, axis=1), zero)
        ku_x = jnp.where(own, jnp.concatenate([ku_h] * nc, axis=1), zero)
        upd = _dot_tn(v_h, ku_x)
        state = st_ref[:, kc]
        entering = []
        for c in range(nc):
            entering.append(state)
            state = state * dec[c * GLA_CHUNK:c * GLA_CHUNK + 1, kc] + upd[:, c * GLA_DK:(c + 1) * GLA_DK]
        st_ref[:, kc] = state
        s_all = jnp.concatenate(entering, axis=1).astype(MXU_DTYPE)
        o = o + _dot_nt(qd_x, s_all)
        o = _rmsnorm(o, nw)
        o_ref[:, vc] = (o * _silu(r_ref[:, vc].astype(F32))).astype(o_ref.dtype)


def _gla(p, ps, wg2, bg2, nw, *, batch, seq, tm):
    nt = seq // tm
    full = lambda a: pl.BlockSpec(a.shape, lambda b, i: (0,) * a.ndim)
    row = lambda b, i: b * nt + i
    return pl.pallas_call(
        functools.partial(_gla_kernel, tm=tm),
        out_shape=jax.ShapeDtypeStruct((p.shape[0], GLA_HEADS * GLA_DV), p.dtype),
        grid=(batch, nt),
        in_specs=[
            pl.BlockSpec((tm, 256), lambda b, i: (row(b, i), P_GLA_Q // 256)),
            pl.BlockSpec((tm, 256), lambda b, i: (row(b, i), P_GLA_K // 256)),
            pl.BlockSpec((tm, 512), lambda b, i: (row(b, i), P_GLA_V // 512)),
            pl.BlockSpec((tm, 512), lambda b, i: (row(b, i), P_GLA_R // 512)),
            pl.BlockSpec((tm, LANES), lambda b, i: (row(b, i), 0)),
            full(wg2), full(bg2), full(nw),
        ],
        out_specs=pl.BlockSpec((tm, GLA_HEADS * GLA_DV), lambda b, i: (row(b, i), 0)),
        scratch_shapes=[pltpu.VMEM((GLA_DV, GLA_HEADS * GLA_DK), F32)],
        compiler_params=_cparams(("parallel", "arbitrary"), 32 << 20),
        name="gla",
    )(p, p, p, p, ps, wg2, bg2, nw)


def _ssd_kernel(z_ref, x_ref, b_ref, c_ref, ps_ref, pst_ref, cw_ref, cb_ref, dtb_row_ref,
                dtb_col_ref, alog_row_ref, alog_col_ref, dskip_ref, nw_ref, exp_ref,
                o_ref, st_ref, tail_ref, *, tm):
    @pl.when(pl.program_id(1) == 0)
    def _():
        st_ref[...] = jnp.zeros_like(st_ref)
        tail_ref[...] = jnp.zeros_like(tail_ref)

    inner = SSD_INNER
    gw = inner // SSD_GROUPS
    xbc = jnp.concatenate([x_ref[...], b_ref[...], c_ref[...]], axis=-1).astype(F32)
    ext = jnp.concatenate([tail_ref[...], xbc], axis=0)
    tail_ref[...] = xbc[tm - SUBLANES:, :]
    conv = cb_ref[...]
    for j in range(SSD_CONV):
        shift = SSD_CONV - 1 - j
        src = ext if shift == 0 else pltpu.roll(ext, shift, axis=0)
        conv = conv + cw_ref[j:j + 1, :] * src[SUBLANES:, :]
    xc = _silu(conv)
    xs = xc[:, :inner]
    bm = xc[:, inner:inner + SSD_GROUPS * SSD_STATE].astype(MXU_DTYPE)
    cm = xc[:, inner + SSD_GROUPS * SSD_STATE:].astype(MXU_DTYPE)

    dt_exp = _dot01_right(_softplus(ps_ref[...] + dtb_row_ref[...]), exp_ref[...])
    a_row = -jnp.exp(alog_row_ref[...])
    dta = dt_exp * a_row
    tril, triu, same = _chunk_masks(tm, SSD_CHUNK)
    acum, aend = _dot01_left_multi((tril, same), dta)
    dta_t = _softplus(pst_ref[...] + dtb_col_ref[...]) * (-jnp.exp(alog_col_ref[...]))
    acum_t = _dot01_right(dta_t, triu)
    x_dt = xs * dt_exp
    xw = (x_dt * jnp.exp(aend - acum)).astype(MXU_DTYPE)
    x_dt_m = x_dt.astype(MXU_DTYPE)
    e_acum = jnp.exp(acum)
    cr = lax.broadcasted_iota(jnp.int32, (SSD_CHUNK, SSD_CHUNK), 0)
    cc = lax.broadcasted_iota(jnp.int32, (SSD_CHUNK, SSD_CHUNK), 1)
    causal = cr >= cc
    dskip = dskip_ref[...]
    nw = nw_ref[...]
    hg = SSD_HEADS // SSD_GROUPS

    for c in range(tm // SSD_CHUNK):
        r0 = c * SSD_CHUNK
        rows = slice(r0, r0 + SSD_CHUNK)
        for g in range(SSD_GROUPS):
            gl = slice(g * gw, (g + 1) * gw)
            nl = slice(g * SSD_STATE, (g + 1) * SSD_STATE)
            b_g, c_g = bm[rows, nl], cm[rows, nl]
            cb = _dot_nt(c_g, b_g)
            parts = []
            for hh in range(hg):
                h = g * hg + hh
                lane0 = h * SSD_HEADDIM
                colv = jnp.broadcast_to(acum[rows, lane0:lane0 + 1], (SSD_CHUNK, SSD_CHUNK))
                rowv = acum_t[h:h + 1, r0:r0 + SSD_CHUNK]
                decay = jnp.where(causal, jnp.exp(colv - rowv), 0.0)
                mmat = (cb * decay).astype(MXU_DTYPE)
                parts.append(_dot(mmat, x_dt_m[rows, lane0:lane0 + SSD_HEADDIM]))
            y_diag = jnp.concatenate(parts, axis=-1)
            state_t = st_ref[:, gl]
            y_off = _dot(c_g, state_t.astype(MXU_DTYPE)) * e_acum[rows, gl]
            st_ref[:, gl] = state_t * jnp.exp(aend[r0:r0 + 1, gl]) + _dot_tn(b_g, xw[rows, gl])
            y = y_diag + y_off + dskip[:, gl] * xs[rows, gl]
            y = y * _silu(z_ref[rows, gl].astype(F32))
            o_ref[rows, gl] = _rmsnorm(y, nw[:, gl]).astype(o_ref.dtype)


def _ssd(p, ps, pst, cw, cb, dtb_row, dtb_col, alog_row, alog_col, dskip, nw, expand, *, batch, seq, tm):
    nt = seq // tm
    full = lambda a: pl.BlockSpec(a.shape, lambda b, i: (0,) * a.ndim)
    row = lambda b, i: b * nt + i
    conv_dim = SSD_INNER + 2 * SSD_GROUPS * SSD_STATE
    return pl.pallas_call(
        functools.partial(_ssd_kernel, tm=tm),
        out_shape=jax.ShapeDtypeStruct((p.shape[0], SSD_INNER), p.dtype),
        grid=(batch, nt),
        in_specs=[
            pl.BlockSpec((tm, 512), lambda b, i: (row(b, i), P_SSD_Z // 512)),
            pl.BlockSpec((tm, 512), lambda b, i: (row(b, i), P_SSD_X // 512)),
            pl.BlockSpec((tm, 256), lambda b, i: (row(b, i), P_SSD_B // 256)),
            pl.BlockSpec((tm, 256), lambda b, i: (row(b, i), P_SSD_C // 256)),
            pl.BlockSpec((tm, LANES), lambda b, i: (row(b, i), 0)),
            pl.BlockSpec((SUBLANES, tm), lambda b, i: (PS_DT // SUBLANES, row(b, i))),
            full(cw), full(cb), full(dtb_row), full(dtb_col), full(alog_row), full(alog_col),
            full(dskip), full(nw), full(expand),
        ],
        out_specs=pl.BlockSpec((tm, SSD_INNER), lambda b, i: (row(b, i), 0)),
        scratch_shapes=[pltpu.VMEM((SSD_STATE, SSD_INNER), F32),
                        pltpu.VMEM((SUBLANES, conv_dim), F32)],
        compiler_params=_cparams(("parallel", "arbitrary"), 40 << 20),
        name="ssd",
    )(p, p, p, p, ps, pst, cw, cb, dtb_row, dtb_col, alog_row, alog_col, dskip, nw, expand)


def _merge_kernel(x_ref, g_ref, y0_ref, y1_ref, y2_ref, y3_ref, wsq_ref, wbr_ref, o_ref, h_ref, acc_ref,
                  *, tc):
    j = pl.program_id(1)
    d = x_ref.shape[-1]

    @pl.when(j == 0)
    def _():
        _norm_rows(x_ref, g_ref, h_ref)
        acc_ref[...] = jnp.zeros_like(acc_ref)

    def branch(y_ref):
        h = h_ref[...]
        y = y_ref[...]
        for c in range(d // tc):
            cols = slice(c * tc, (c + 1) * tc)
            gate = jax.nn.sigmoid(_dot(h, wsq_ref[:, cols]))
            acc_ref[:, cols] += gate * _dot(y, wbr_ref[:, cols])

    for b, y_ref in enumerate((y0_ref, y1_ref, y2_ref, y3_ref)):
        pl.when(j == b)(functools.partial(branch, y_ref))

    @pl.when(j == N_BRANCH)
    def _():
        merged = acc_ref[...].astype(MXU_DTYPE)
        for c in range(d // tc):
            cols = slice(c * tc, (c + 1) * tc)
            o_ref[:, cols] = x_ref[:, cols] + _dot(merged, wsq_ref[:, cols])


def _merge(x, g, ys, wsq, wbr, *, layer, tm, tc):
    m, d = x.shape
    assert len(ys) == N_BRANCH
    bw = ys[0].shape[-1]
    wbytes = jnp.dtype(wsq.dtype).itemsize
    vmem = 4 * tm * d * 4 + tm * d * wbytes + tm * d * 4 + 2 * d * d * wbytes \
        + 2 * bw * d * wbytes + 3 * N_BRANCH * tm * bw * wbytes + 4 * tm * tc * 4 + (4 << 20)
    last = N_BRANCH - 1
    y_spec = pl.BlockSpec((tm, bw), lambda i, j: (i, 0))
    return pl.pallas_call(
        functools.partial(_merge_kernel, tc=tc),
        out_shape=jax.ShapeDtypeStruct((m, d), F32),
        grid=(m // tm, N_BRANCH + 1),
        in_specs=[
            pl.BlockSpec((tm, d), lambda i, j: (i, 0)),
            pl.BlockSpec((1, d), lambda i, j: (0, 0)),
            y_spec, y_spec, y_spec, y_spec,
            pl.BlockSpec((None, d, d), lambda i, j: (layer, 0, j)),
            pl.BlockSpec((None, None, bw, d), lambda i, j: (layer, jnp.minimum(j, last), 0, 0)),
        ],
        out_specs=pl.BlockSpec((tm, d), lambda i, j: (i, 0)),
        scratch_shapes=[pltpu.VMEM((tm, d), wsq.dtype), pltpu.VMEM((tm, d), F32)],
        compiler_params=_cparams(("parallel", "arbitrary"), vmem),
        name="merge",
    )(x, g, *ys, wsq, wbr)


def _xattn_kernel(x_ref, g_ref, wq_ref, k_ref, v_ref, wo_ref, o_ref, h_ref, oc_ref, *, tc):
    _norm_rows(x_ref, g_ref, h_ref)
    h = h_ref[...]
    dh = XATTN_DH
    for hd in range(XATTN_HEADS):
        cols = slice(hd * dh, (hd + 1) * dh)
        q = _dot(h, wq_ref[:, cols]).astype(MXU_DTYPE)
        s = _dot_nt(q, k_ref[:, cols]) * (dh ** -0.5 * LOG2E)
        p = jnp.exp2(s - jnp.max(s, axis=-1, keepdims=True))
        p = p / jnp.sum(p, axis=-1, keepdims=True)
        oc_ref[:, cols] = _dot(p.astype(MXU_DTYPE), v_ref[:, cols]).astype(oc_ref.dtype)
    oc = oc_ref[...]
    d = x_ref.shape[-1]
    for c in range(d // tc):
        cols = slice(c * tc, (c + 1) * tc)
        o_ref[:, cols] = x_ref[:, cols] + _dot(oc, wo_ref[:, cols])


def _xattn(x, g, wq, kv, wo, *, layer, seq, n_mem, tm, tc):
    m, d = x.shape
    wbytes = jnp.dtype(wq.dtype).itemsize
    vmem = 4 * tm * d * 4 + 2 * tm * d * wbytes + 2 * d * d * wbytes + 4 * n_mem * d * wbytes \
        + 3 * tm * XATTN_DH * 4 + 3 * tm * n_mem * 4 + 2 * tm * tc * 4 + (6 << 20)
    tiles_per_seq = seq // tm
    return pl.pallas_call(
        functools.partial(_xattn_kernel, tc=tc),
        out_shape=jax.ShapeDtypeStruct((m, d), F32),
        grid=(m // tm,),
        in_specs=[
            pl.BlockSpec((tm, d), lambda i: (i, 0)),
            pl.BlockSpec((1, d), lambda i: (0, 0)),
            pl.BlockSpec((None, d, d), lambda i: (layer, 0, 0), pipeline_mode=pl.Buffered(1)),
            pl.BlockSpec((n_mem, d), lambda i: (i // tiles_per_seq, 0)),
            pl.BlockSpec((n_mem, d), lambda i: (i // tiles_per_seq, 1)),
            pl.BlockSpec((None, d, d), lambda i: (layer, 0, 0), pipeline_mode=pl.Buffered(1)),
        ],
        out_specs=pl.BlockSpec((tm, d), lambda i: (i, 0)),
        scratch_shapes=[pltpu.VMEM((tm, d), wq.dtype), pltpu.VMEM((tm, d), wq.dtype)],
        compiler_params=_cparams(("parallel",), vmem),
        name="xattn",
    )(x, g, wq, kv, kv, wo)


_IN_SIZES = (N_BRANCH * D_MODEL, MLA_Q_RANK, MLA_KV_RANK, MLA_ROPE, 3 * FOX_HEADS * FOX_DH, FOX_HEADS,
             GLA_HEADS * GLA_DK, GLA_HEADS * GLA_DK, GLA_HEADS * GLA_DV, GLA_GATE_RANK,
             GLA_HEADS * GLA_DV, SSD_INNER, SSD_INNER + 2 * SSD_GROUPS * SSD_STATE, SSD_HEADS)


def _rot_cols(w):
    half = w.shape[-1] // 2
    return jnp.concatenate([-w[..., half:], w[..., :half]], axis=-1)


def _prep_w_in(w_mix):
    w_in = w_mix.astype(MXU_DTYPE)
    nl, d = w_in.shape[:2]
    offs = np.cumsum((0,) + _IN_SIZES[1:])
    assert offs[-1] == w_in.shape[2]
    seg = lambda i: w_in[:, :, offs[i]:offs[i + 1]]
    (cq, ckv, kr, fox, fox_f, gq, gk, gv, glr, gr, sz, sxbc, sdt) = [seg(i) for i in range(len(_IN_SIZES) - 1)]
    z = lambda n: jnp.zeros((nl, d, n), w_in.dtype)
    w_p = jnp.concatenate([
        cq, z(512 - MLA_Q_RANK), ckv, kr, z(LANES - MLA_ROPE), _rot_cols(kr), z(LANES - MLA_ROPE),
        fox, gq, gk, gv, gr, sz, sxbc], axis=2)
    assert w_p.shape[2] == P_WIDTH
    cs = np.ones((1, P_WIDTH), np.float32)
    cs[:, P_FOX_Q:P_FOX_K] = FOX_DH ** -0.5 * LOG2E
    cs[:, P_GLA_Q:P_GLA_K] = GLA_DK ** -0.5
    w_s = jnp.concatenate([glr, sdt, fox_f, z(LANES - GLA_GATE_RANK - SSD_HEADS - FOX_HEADS)], axis=2)
    return w_p.astype(MXU_DTYPE), jnp.asarray(cs), w_s.astype(MXU_DTYPE)


def _prep_mla(w_uq, w_ukv):
    nl, r = w_uq.shape[:2]
    wq = w_uq.reshape(nl, r, MLA_HEADS, MLA_NOPE + MLA_ROPE)
    nope, pe = wq[..., :MLA_NOPE], wq[..., MLA_NOPE:]
    zpad = jnp.zeros((nl, r, MLA_HEADS, MLA_DK_PAD - MLA_NOPE - MLA_ROPE), w_uq.dtype)
    wa = jnp.concatenate([nope, pe, zpad], axis=-1).reshape(nl, r, MLA_HEADS * MLA_DK_PAD)
    wb = jnp.concatenate([_rot_cols(pe), zpad], axis=-1).reshape(nl, r, MLA_HEADS * LANES)
    rk = w_ukv.shape[1]
    wkv = w_ukv.reshape(nl, rk, MLA_HEADS, MLA_NOPE + MLA_V)
    wkv = jnp.concatenate([wkv[..., :MLA_NOPE].reshape(nl, rk, -1), wkv[..., MLA_NOPE:].reshape(nl, rk, -1)],
                          axis=2)
    return wa.astype(MXU_DTYPE), wb.astype(MXU_DTYPE), wkv.astype(MXU_DTYPE)


def _tiles(m, seq):
    mid = min(512, seq)
    return dict(ffn_tm=mid, ffn_tf=512, proj_tm=mid, proj_tn=2816, prep_tm=min(1024, seq),
                flash_tb=mid, flash_rmult=min(2, seq // mid), gla_tm=min(512, seq), ssd_tm=min(256, seq), merge_tm=mid, merge_tc=512,
                xattn_tm=mid, xattn_tc=512, rope_tm=min(1024, m))


@jax.jit
def kernel(x, mem, positions, ffn1_norm, ffn1_w_gate, ffn1_w_up, ffn1_w_down, mix_norm, w_in, mla_q_norm, mla_kv_norm, mla_w_uq, mla_w_ukv, fox_f_bias, gla_w_g2, gla_b_g2, gla_norm, ssd_conv_w, ssd_conv_b, ssd_dt_bias, ssd_a_log, ssd_d, ssd_norm, w_branch, w_out, xattn_norm, mem_norm, xattn_w_q, xattn_w_k, xattn_w_v, xattn_w_o, ffn2_norm, ffn2_w_gate, ffn2_w_up, ffn2_w_down, final_norm):
    batch, seq, d = x.shape
    n_mem = mem.shape[1]
    m = batch * seq
    t = _tiles(m, seq)
    depth = w_in.shape[0]
    cast = lambda a: a.astype(MXU_DTYPE)
    row = lambda a: a.reshape(1, -1).astype(F32)
    col8 = lambda v: jnp.zeros((SUBLANES, 1), F32).at[:v.shape[0], 0].set(v.astype(F32))

    xs = x.reshape(m, d)
    mem2 = mem.reshape(batch * n_mem, d)
    ct, st = _rope_tables(positions.reshape(m, 1), tm=t["rope_tm"])

    expand = np.zeros((LANES, SSD_INNER), np.float32)
    for h in range(SSD_HEADS):
        expand[PS_DT + h, h * SSD_HEADDIM:(h + 1) * SSD_HEADDIM] = 1.0
    expand = jnp.asarray(expand, MXU_DTYPE)
    rep = lambda v: jnp.repeat(v.astype(F32), SSD_HEADDIM).reshape(1, SSD_INNER)
    ffn = functools.partial(_ffn, tm=t["ffn_tm"], tf=t["ffn_tf"])
    flash = functools.partial(_flash, batch=batch, seq=seq, groups=2, tb=t["flash_tb"], rmult=t["flash_rmult"])

    f1g, f1u, f1d = cast(ffn1_w_gate), cast(ffn1_w_up), cast(ffn1_w_down)
    f2g, f2u, f2d = cast(ffn2_w_gate), cast(ffn2_w_up), cast(ffn2_w_down)
    w_p, cs, w_s = _prep_w_in(w_in[:, :, N_BRANCH * d:])
    wa, wb, wkv = _prep_mla(mla_w_uq, mla_w_ukv)
    wsq = jnp.concatenate([cast(w_in[:, :, :N_BRANCH * d]), cast(w_out)], axis=2)
    wbr = cast(w_branch)
    w_kv = cast(jnp.concatenate([xattn_w_k, xattn_w_v], axis=2))
    wq_x, wo_x = cast(xattn_w_q), cast(xattn_w_o)

    for l in range(depth):
        xs = ffn(xs, row(ffn1_norm[l]), f1g, f1u, f1d, row(final_norm), layer=l, final_norm=False)

        p, ps, pst = _proj(xs, row(mix_norm[l]), w_p, cs, w_s, layer=l, tm=t["proj_tm"], tn=t["proj_tn"])

        q_mla, k_mla, v_mla = _mla_prep(p, ct, st, row(mla_q_norm[l]), row(mla_kv_norm[l]), wa, wb, wkv,
                                        layer=l, tm=t["prep_tm"])
        y_mla = flash(q_mla, k_mla, v_mla, None, heads=MLA_HEADS,
                      dk=MLA_DK_PAD, dv=MLA_V, q_off=0, k_off=0, v_off=0)

        c_fox = _fox_cum(pst, col8(fox_f_bias[l]), batch=batch, seq=seq)
        y_fox = flash(p, p, p, c_fox, heads=FOX_HEADS,
                      dk=FOX_DH, dv=FOX_DH, q_off=P_FOX_Q, k_off=P_FOX_K, v_off=P_FOX_V)

        wg2 = jnp.zeros((LANES, GLA_HEADS * GLA_DK), F32).at[PS_GLR:PS_GLR + GLA_GATE_RANK].set(gla_w_g2[l])
        y_gla = _gla(p, ps, cast(wg2), row(gla_b_g2[l]), row(gla_norm[l]), batch=batch, seq=seq, tm=t["gla_tm"])

        dtb_row = jnp.zeros((1, LANES), F32).at[0, PS_DT:PS_DT + SSD_HEADS].set(ssd_dt_bias[l])
        y_ssd = _ssd(p, ps, pst, ssd_conv_w[l].astype(F32), row(ssd_conv_b[l]), dtb_row, col8(ssd_dt_bias[l]),
                     rep(ssd_a_log[l]), col8(ssd_a_log[l]), rep(ssd_d[l]), row(ssd_norm[l]), expand,
                     batch=batch, seq=seq, tm=t["ssd_tm"])

        xs = _merge(xs, row(mix_norm[l]), (y_mla, y_fox, y_gla, y_ssd), wsq, wbr, layer=l, tm=t["merge_tm"], tc=t["merge_tc"])

        kv = _norm_matmul(mem2, row(mem_norm[l]), w_kv, layer=l, tm=min(512, batch * n_mem), tn=1024)
        xs = _xattn(xs, row(xattn_norm[l]), wq_x, kv, wo_x, layer=l,
                    seq=seq, n_mem=n_mem, tm=t["xattn_tm"], tc=t["xattn_tc"])

        xs = ffn(xs, row(ffn2_norm[l]), f2g, f2u, f2d, row(final_norm), layer=l, final_norm=(l == depth - 1))

    return xs.reshape(batch, seq, d)
```

```python
import functools

import numpy as np
import jax
import jax.numpy as jnp
from jax import lax
from jax.experimental import pallas as pl
from jax.experimental.pallas import tpu as pltpu

D_MODEL = 2048
DEPTH = 4
EPS = 1e-6
N_BRANCH = 4
BRANCH_WIDTH = 512
MLA_HEADS, MLA_NOPE, MLA_ROPE, MLA_V = 4, 128, 64, 128
MLA_Q_RANK, MLA_KV_RANK = 384, 256
MLA_DK_PAD = 256
ROPE_THETA = 10000.0
FOX_HEADS, FOX_DH = 4, 128
GLA_HEADS, GLA_DK, GLA_DV = 4, 64, 128
GLA_GATE_RANK, GLA_GATE_NORM, GLA_CHUNK = 16, 16.0, 64
SSD_HEADS, SSD_HEADDIM, SSD_GROUPS, SSD_STATE, SSD_CONV, SSD_CHUNK = 8, 64, 2, 128, 4, 128
SSD_INNER = SSD_HEADS * SSD_HEADDIM
XATTN_HEADS = 4
XATTN_DH = D_MODEL // XATTN_HEADS
D_FF = 5632

LANES = 128
SUBLANES = 8
V7X_VMEM_LIMIT_BYTES = 60 * 1024 * 1024

MXU_DTYPE = jnp.bfloat16
FFN_SLOTS = 2
F32 = jnp.float32
LOG2E = 1.4426950408889634

P_CQ, P_CKV, P_KR, P_KRROT = 0, 512, 768, 896
P_FOX_Q, P_FOX_K, P_FOX_V = 1024, 1536, 2048
P_GLA_Q, P_GLA_K, P_GLA_V, P_GLA_R = 2560, 2816, 3072, 3584
P_SSD_Z, P_SSD_X, P_SSD_B, P_SSD_C = 4096, 4608, 5120, 5376
P_WIDTH = 5632
PS_GLR, PS_DT, PS_F = 0, 16, 24
PST_ROWS = 32


def _cparams(sem, vmem_bytes):
    return pltpu.CompilerParams(dimension_semantics=sem,
                                vmem_limit_bytes=int(min(vmem_bytes, V7X_VMEM_LIMIT_BYTES)))


def _dot(a, b):
    return jnp.dot(a, b, preferred_element_type=F32)


def _dot_nt(a, b):
    return lax.dot_general(a, b, (((1,), (1,)), ((), ())), preferred_element_type=F32)


def _dot_tn(a, b):
    return lax.dot_general(a, b, (((0,), (0,)), ((), ())), preferred_element_type=F32)


def _split3(x):
    hi = x.astype(MXU_DTYPE)
    r1 = x - hi.astype(F32)
    mid = r1.astype(MXU_DTYPE)
    lo = (r1 - mid.astype(F32)).astype(MXU_DTYPE)
    return hi, mid, lo


def _dot01_right(x, m01):
    if MXU_DTYPE == F32:
        return _dot(x, m01.astype(F32))
    m = m01.astype(MXU_DTYPE)
    hi, mid, lo = _split3(x)
    return _dot(hi, m) + _dot(mid, m) + _dot(lo, m)


def _dot01_left_multi(m01s, x):
    if MXU_DTYPE == F32:
        return [_dot(m01.astype(F32), x) for m01 in m01s]
    parts = _split3(x)
    out = []
    for m01 in m01s:
        m = m01.astype(MXU_DTYPE)
        out.append(_dot(m, parts[0]) + _dot(m, parts[1]) + _dot(m, parts[2]))
    return out


def _dot01_left(m01, x):
    return _dot01_left_multi((m01,), x)[0]


def _rmsnorm(x, g):
    ms = jnp.mean(x * x, axis=-1, keepdims=True)
    return x * lax.rsqrt(ms + EPS) * g


def _log_sigmoid(x):
    return jnp.minimum(x, 0.0) - jnp.log1p(jnp.exp(-jnp.abs(x)))


def _softplus(x):
    return jnp.maximum(x, 0.0) + jnp.log1p(jnp.exp(-jnp.abs(x)))


def _silu(x):
    return x * jax.nn.sigmoid(x)


def _chunk_masks(n, chunk):
    shift = chunk.bit_length() - 1
    assert 1 << shift == chunk
    r = lax.broadcasted_iota(jnp.int32, (n, n), 0)
    c = lax.broadcasted_iota(jnp.int32, (n, n), 1)
    same = jnp.right_shift(r, shift) == jnp.right_shift(c, shift)
    return jnp.logical_and(same, r >= c), jnp.logical_and(same, r <= c), same


def _norm_rows(x_ref, g_ref, h_ref, copy_ref=None, *, rows=256):
    tm = x_ref.shape[0]
    rows = min(rows, tm)
    g = g_ref[...]
    for r in range(tm // rows):
        sl = slice(r * rows, (r + 1) * rows)
        x = x_ref[sl, :]
        h_ref[sl, :] = _rmsnorm(x, g).astype(h_ref.dtype)
        if copy_ref is not None:
            copy_ref[sl, :] = x


def _ffn_kernel(x_ref, g_ref, wg_hbm, wu_hbm, wd_hbm, fg_ref, o_ref, h_ref, wg_buf, wu_buf, wd_buf, sem,
                *, final_norm, layer, tf, tc, n_chunks):
    i = pl.program_id(0)

    def chunk_copies(j, slot):
        cols = pl.ds(j * tf, tf)
        return (pltpu.make_async_copy(wg_hbm.at[layer, :, cols], wg_buf.at[slot], sem.at[0, slot]),
                pltpu.make_async_copy(wu_hbm.at[layer, :, cols], wu_buf.at[slot], sem.at[1, slot]),
                pltpu.make_async_copy(wd_hbm.at[layer, cols, :], wd_buf.at[slot], sem.at[2, slot]))

    def start(j, slot):
        for cp in chunk_copies(j, slot):
            cp.start()

    @pl.when(i == 0)
    def _():
        start(0, 0)

    _norm_rows(x_ref, g_ref, h_ref, o_ref)
    h = h_ref[...]
    for j in range(n_chunks):
        slot = j % FFN_SLOTS
        for cp in chunk_copies(j, slot):
            cp.wait()
        if j + 1 < n_chunks:
            start(j + 1, (j + 1) % FFN_SLOTS)
        else:
            @pl.when(i + 1 < pl.num_programs(0))
            def _():
                start(0, 0)
        a = _dot(h, wg_buf[slot])
        u = _dot(h, wu_buf[slot])
        t = (_silu(a) * (0.5 * u)).astype(MXU_DTYPE)
        d = o_ref.shape[-1]
        for c in range(d // tc):
            cols = slice(c * tc, (c + 1) * tc)
            o_ref[:, cols] += _dot(t, wd_buf[slot, :, cols])

    if final_norm:
        _norm_rows(o_ref, fg_ref, o_ref)


def _ffn(x, g, wg, wu, wd, fg, *, layer, final_norm, tm, tf, tc):
    m, d = x.shape
    f = wg.shape[-1]
    n_chunks = f // tf
    assert n_chunks * tf == f and n_chunks > FFN_SLOTS and (n_chunks - 1) % FFN_SLOTS != 0
    wb = jnp.dtype(wg.dtype).itemsize
    vmem = 4 * tm * d * 4 + tm * d * wb + FFN_SLOTS * 3 * d * tf * wb + 4 * tm * tf * 4 + 2 * tm * tc * 4 + (4 << 20)
    return pl.pallas_call(
        functools.partial(_ffn_kernel, final_norm=final_norm, layer=layer, tf=tf, tc=tc, n_chunks=n_chunks),
        out_shape=jax.ShapeDtypeStruct((m, d), F32),
        grid=(m // tm,),
        in_specs=[
            pl.BlockSpec((tm, d), lambda i: (i, 0)),
            pl.BlockSpec((1, d), lambda i: (0, 0)),
            pl.BlockSpec(memory_space=pl.ANY),
            pl.BlockSpec(memory_space=pl.ANY),
            pl.BlockSpec(memory_space=pl.ANY),
            pl.BlockSpec((1, d), lambda i: (0, 0)),
        ],
        out_specs=pl.BlockSpec((tm, d), lambda i: (i, 0)),
        scratch_shapes=[pltpu.VMEM((tm, d), wg.dtype),
                        pltpu.VMEM((FFN_SLOTS, d, tf), wg.dtype),
                        pltpu.VMEM((FFN_SLOTS, d, tf), wg.dtype),
                        pltpu.VMEM((FFN_SLOTS, tf, d), wg.dtype),
                        pltpu.SemaphoreType.DMA((3, FFN_SLOTS))],
        compiler_params=_cparams(("arbitrary",), vmem),
        name="ffn",
    )(x, g, wg, wu, wd, fg)


def _proj_kernel(x_ref, g_ref, w_ref, cs_ref, ws_ref, p_ref, ps_ref, pst_ref, h_ref):
    j = pl.program_id(1)

    @pl.when(j == 0)
    def _():
        _norm_rows(x_ref, g_ref, h_ref)
        ps = _dot(h_ref[...], ws_ref[...])
        ps_ref[...] = ps
        pst_ref[...] = jnp.transpose(ps)[:PST_ROWS, :]

    p_ref[...] = (_dot(h_ref[...], w_ref[...]) * cs_ref[...]).astype(p_ref.dtype)


def _proj(x, g, w, cs, ws, *, layer, tm, tn):
    m, d = x.shape
    n = w.shape[-1]
    wb = jnp.dtype(w.dtype).itemsize
    vmem = 2 * tm * d * 4 + tm * d * wb + 2 * d * tn * wb + 2 * tm * tn * wb + 2 * tm * tn * 4 + (6 << 20)
    return pl.pallas_call(
        _proj_kernel,
        out_shape=(jax.ShapeDtypeStruct((m, n), w.dtype),
                   jax.ShapeDtypeStruct((m, LANES), F32),
                   jax.ShapeDtypeStruct((PST_ROWS, m), F32)),
        grid=(m // tm, n // tn),
        in_specs=[
            pl.BlockSpec((tm, d), lambda i, j: (i, 0)),
            pl.BlockSpec((1, d), lambda i, j: (0, 0)),
            pl.BlockSpec((None, d, tn), lambda i, j: (layer, 0, j)),
            pl.BlockSpec((1, tn), lambda i, j: (0, j)),
            pl.BlockSpec((None, d, LANES), lambda i, j: (layer, 0, 0)),
        ],
        out_specs=(pl.BlockSpec((tm, tn), lambda i, j: (i, j)),
                   pl.BlockSpec((tm, LANES), lambda i, j: (i, 0)),
                   pl.BlockSpec((PST_ROWS, tm), lambda i, j: (0, i))),
        scratch_shapes=[pltpu.VMEM((tm, d), w.dtype)],
        compiler_params=_cparams(("parallel", "arbitrary"), vmem),
        name="proj",
    )(x, g, w, cs, ws)


def _norm_matmul_kernel(x_ref, g_ref, w_ref, o_ref):
    h = _rmsnorm(x_ref[...], g_ref[...]).astype(w_ref.dtype)
    o_ref[...] = _dot(h, w_ref[...]).astype(o_ref.dtype)


def _norm_matmul(x, g, w, *, layer, tm, tn):
    m, d = x.shape
    n = w.shape[-1]
    wb = jnp.dtype(w.dtype).itemsize
    vmem = 3 * tm * d * 4 + 2 * d * tn * wb + 3 * tm * tn * 4 + (4 << 20)
    return pl.pallas_call(
        _norm_matmul_kernel,
        out_shape=jax.ShapeDtypeStruct((m, n), w.dtype),
        grid=(m // tm, n // tn),
        in_specs=[pl.BlockSpec((tm, d), lambda i, j: (i, 0)),
                  pl.BlockSpec((1, d), lambda i, j: (0, 0)),
                  pl.BlockSpec((None, d, tn), lambda i, j: (layer, 0, j))],
        out_specs=pl.BlockSpec((tm, tn), lambda i, j: (i, j)),
        compiler_params=_cparams(("parallel", "parallel"), vmem),
        name="mem_kv",
    )(x, g, w)


def _rope_kernel(pos_ref, ct_ref, st_ref):
    pos = pos_ref[...].astype(F32)
    lane = lax.broadcasted_iota(jnp.int32, (1, LANES), 1)
    half = MLA_ROPE // 2
    f = jnp.bitwise_and(lane, half - 1).astype(F32)
    inv_freq = jnp.power(jnp.float32(ROPE_THETA), -f / half)
    ang = pos * inv_freq
    valid = lane < MLA_ROPE
    ct_ref[...] = jnp.where(valid, jnp.cos(ang), 0.0)
    st_ref[...] = jnp.where(valid, jnp.sin(ang), 0.0)


def _rope_tables(pos_col, *, tm):
    m = pos_col.shape[0]
    return pl.pallas_call(
        _rope_kernel,
        out_shape=(jax.ShapeDtypeStruct((m, LANES), F32), jax.ShapeDtypeStruct((m, LANES), F32)),
        grid=(m // tm,),
        in_specs=[pl.BlockSpec((tm, 1), lambda i: (i, 0))],
        out_specs=(pl.BlockSpec((tm, LANES), lambda i: (i, 0)),
                   pl.BlockSpec((tm, LANES), lambda i: (i, 0))),
        compiler_params=_cparams(("parallel",), 16 << 20),
        name="rope_tables",
    )(pos_col)


def _mla_prep_kernel(cq_ref, ckv_ref, kr_ref, krrot_ref, ct_ref, st_ref, qn_ref, kvn_ref,
                     wa_ref, wb_ref, wkv_ref, q_ref, k_ref, v_ref):
    ct = ct_ref[...]
    st = st_ref[...]
    scale = (MLA_NOPE + MLA_ROPE) ** -0.5 * LOG2E

    ckv = ckv_ref[...].astype(F32)
    ckvn = _rmsnorm(ckv, kvn_ref[...]).astype(MXU_DTYPE)
    kv = _dot(ckvn, wkv_ref[...])
    kpe = kr_ref[...].astype(F32) * ct + krrot_ref[...].astype(F32) * st
    kpe = kpe.astype(k_ref.dtype)
    hw = MLA_HEADS * MLA_NOPE
    for h in range(MLA_HEADS):
        base = h * MLA_DK_PAD
        k_ref[:, base:base + MLA_NOPE] = kv[:, h * MLA_NOPE:(h + 1) * MLA_NOPE].astype(k_ref.dtype)
        k_ref[:, base + MLA_NOPE:base + MLA_DK_PAD] = kpe
    v_ref[...] = kv[:, hw:].astype(v_ref.dtype)

    cq = cq_ref[...].astype(F32)[:, :MLA_Q_RANK]
    cqn = _rmsnorm(cq, qn_ref[...]).astype(MXU_DTYPE)
    qa = _dot(cqn, wa_ref[...])
    qb = _dot(cqn, wb_ref[...])
    for h in range(MLA_HEADS):
        base = h * MLA_DK_PAD
        q_ref[:, base:base + MLA_NOPE] = (qa[:, base:base + MLA_NOPE] * scale).astype(q_ref.dtype)
        pe = qa[:, base + MLA_NOPE:base + MLA_DK_PAD] * ct + qb[:, h * LANES:(h + 1) * LANES] * st
        q_ref[:, base + MLA_NOPE:base + MLA_DK_PAD] = (pe * scale).astype(q_ref.dtype)


def _mla_prep(p, ct, st, qn, kvn, wa, wb, wkv, *, layer, tm):
    m = p.shape[0]
    qk_w = MLA_HEADS * MLA_DK_PAD
    v_w = MLA_HEADS * MLA_V
    full = lambda a: pl.BlockSpec(a.shape, lambda i: (0,) * a.ndim)
    of_layer = lambda a: pl.BlockSpec((None,) + a.shape[1:], lambda i: (layer,) + (0,) * (a.ndim - 1))
    return pl.pallas_call(
        _mla_prep_kernel,
        out_shape=(jax.ShapeDtypeStruct((m, qk_w), p.dtype),
                   jax.ShapeDtypeStruct((m, qk_w), p.dtype),
                   jax.ShapeDtypeStruct((m, v_w), p.dtype)),
        grid=(m // tm,),
        in_specs=[
            pl.BlockSpec((tm, 512), lambda i: (i, P_CQ // 512)),
            pl.BlockSpec((tm, 256), lambda i: (i, P_CKV // 256)),
            pl.BlockSpec((tm, 128), lambda i: (i, P_KR // 128)),
            pl.BlockSpec((tm, 128), lambda i: (i, P_KRROT // 128)),
            pl.BlockSpec((tm, LANES), lambda i: (i, 0)),
            pl.BlockSpec((tm, LANES), lambda i: (i, 0)),
            full(qn), full(kvn), of_layer(wa), of_layer(wb), of_layer(wkv),
        ],
        out_specs=(pl.BlockSpec((tm, qk_w), lambda i: (i, 0)),
                   pl.BlockSpec((tm, qk_w), lambda i: (i, 0)),
                   pl.BlockSpec((tm, v_w), lambda i: (i, 0))),
        compiler_params=_cparams(("parallel",), 40 << 20),
        name="mla_prep",
    )(p, p, p, p, ct, st, qn, kvn, wa, wb, wkv)


def _fox_cum_kernel(ft_ref, fb_ref, c_ref, *, seq):
    lf = _log_sigmoid(ft_ref[...] + fb_ref[...])
    r = lax.broadcasted_iota(jnp.int32, (LANES, LANES), 0)
    c = lax.broadcasted_iota(jnp.int32, (LANES, LANES), 1)
    upper = r <= c
    carry = jnp.zeros((SUBLANES, 1), F32)
    for blk in range(seq // LANES):
        sl = slice(blk * LANES, (blk + 1) * LANES)
        cs = _dot01_right(lf[:, sl], upper) + carry
        for h in range(FOX_HEADS):
            c_ref[h, :, sl] = cs[h:h + 1, :] * LOG2E
        carry = cs[:, LANES - 1:LANES]


def _fox_cum(pst, fb_col, *, batch, seq):
    return pl.pallas_call(
        functools.partial(_fox_cum_kernel, seq=seq),
        out_shape=jax.ShapeDtypeStruct((batch * FOX_HEADS, 1, seq), F32),
        grid=(batch,),
        in_specs=[pl.BlockSpec((SUBLANES, seq), lambda b: (PS_F // SUBLANES, b)),
                  pl.BlockSpec((SUBLANES, 1), lambda b: (0, 0))],
        out_specs=pl.BlockSpec((FOX_HEADS, 1, seq), lambda b: (b, 0, 0)),
        compiler_params=_cparams(("parallel",), 16 << 20),
        name="fox_cum",
    )(pst, fb_col)


def _flash_kernel(*refs, seq, tb, rmult, groups, dk, dv, has_bias):
    refs = list(refs)
    q_ref, k_ref, v_ref = refs[:3]
    c_ref = refs[3] if has_bias else None
    o_ref = refs[3 + int(has_bias)]
    rows_st = tb * rmult

    def update(state, s, v_blk):
        m, l, acc = state
        m_new = jnp.maximum(m, jnp.max(s, axis=-1, keepdims=True))
        alpha = jnp.exp2(m - m_new)
        p = jnp.exp2(s - m_new)
        l = alpha * l + jnp.sum(p, axis=-1, keepdims=True)
        acc = alpha * acc + _dot(p.astype(MXU_DTYPE), v_blk)
        return m_new, l, acc

    def super_tile(pj, unused):
        r0 = pl.multiple_of(pj * rows_st, rows_st)
        qs = [q_ref[pl.ds(r0, rows_st), g * dk:(g + 1) * dk] for g in range(groups)]
        if has_bias:
            c0 = [jnp.max(c_ref[g, :, pl.ds(r0, tb)], axis=-1, keepdims=True) for g in range(groups)]

        def scores(q, g, start, width=tb):
            s = _dot_nt(q, k_ref[pl.ds(start, width), g * dk:(g + 1) * dk])
            if has_bias:
                s = s + (c0[g] - c_ref[g, :, pl.ds(start, width)])
            return s

        def v_block(g, start, width=tb):
            return v_ref[pl.ds(start, width), g * dv:(g + 1) * dv]

        def body(ki, carry):
            start = pl.multiple_of(ki * rows_st, rows_st)
            return tuple(update(carry[g], scores(qs[g], g, start, rows_st), v_block(g, start, rows_st))
                         for g in range(groups))

        init = tuple((jnp.full((rows_st, 1), -jnp.inf, F32), jnp.zeros((rows_st, 1), F32),
                      jnp.zeros((rows_st, dv), F32)) for _ in range(groups))
        carry = lax.fori_loop(0, pj, body, init)

        for g in range(groups):
            m, l, acc = carry[g]
            for a in range(rmult):
                rs = slice(a * tb, (a + 1) * tb)
                width = (a + 1) * tb
                row = lax.broadcasted_iota(jnp.int32, (tb, width), 0)
                col = lax.broadcasted_iota(jnp.int32, (tb, width), 1)
                s = jnp.where(col <= row + a * tb, scores(qs[g][rs], g, r0, width), -jnp.inf)
                state = update((m[rs], l[rs], acc[rs]), s, v_block(g, r0, width))
                o_ref[pl.ds(pl.multiple_of(r0 + a * tb, tb), tb), g * dv:(g + 1) * dv] = \
                    (state[2] / state[1]).astype(o_ref.dtype)
        return unused

    lax.fori_loop(0, seq // rows_st, super_tile, 0)


def _flash(q_arr, k_arr, v_arr, c_arr, *, batch, seq, heads, groups, dk, dv,
           q_off, k_off, v_off, tb, rmult):
    rows_st = tb * rmult
    has_bias = c_arr is not None
    gk, gv = groups * dk, groups * dv
    in_specs = [pl.BlockSpec((seq, gk), lambda b, h: (b, q_off // gk + h)),
                pl.BlockSpec((seq, gk), lambda b, h: (b, k_off // gk + h)),
                pl.BlockSpec((seq, gv), lambda b, h: (b, v_off // gv + h))]
    args = [q_arr, k_arr, v_arr]
    if has_bias:
        in_specs.append(pl.BlockSpec((groups, 1, seq), lambda b, h: (b * (heads // groups) + h, 0, 0)))
        args.append(c_arr)
    vmem = 2 * seq * (2 * gk + 2 * gv) * 2 + groups * (3 * rows_st * rows_st * 4 + 4 * rows_st * dv * 4) + (8 << 20)
    return pl.pallas_call(
        functools.partial(_flash_kernel, seq=seq, tb=tb, rmult=rmult, groups=groups, dk=dk, dv=dv,
                          has_bias=has_bias),
        out_shape=jax.ShapeDtypeStruct((batch * seq, heads * dv), q_arr.dtype),
        grid=(batch, heads // groups),
        in_specs=in_specs,
        out_specs=pl.BlockSpec((seq, gv), lambda b, h: (b, h)),
        compiler_params=_cparams(("parallel", "parallel"), vmem),
        name="flash_fox" if has_bias else "flash_mla",
    )(*args)


def _gla_kernel(q_ref, k_ref, v_ref, r_ref, ps_ref, wg2_ref, bg2_ref, nw_ref, o_ref, st_ref, *, tm):
    @pl.when(pl.program_id(1) == 0)
    def _():
        st_ref[...] = jnp.zeros_like(st_ref)

    nc = tm // GLA_CHUNK
    glin = _dot(ps_ref[...].astype(MXU_DTYPE), wg2_ref[...]) + bg2_ref[...]
    gl = _log_sigmoid(glin) / GLA_GATE_NORM
    tril, _, same = _chunk_masks(tm, GLA_CHUNK)
    gcum, gend = _dot01_left_multi((tril, same), gl)
    q = q_ref[...].astype(F32)
    k = k_ref[...].astype(F32)
    qd = (q * jnp.exp(gcum)).astype(MXU_DTYPE)
    kd = (k * jnp.exp(-gcum)).astype(MXU_DTYPE)
    ku = (k * jnp.exp(gend - gcum)).astype(MXU_DTYPE)
    dec = jnp.exp(gend)
    nw = nw_ref[...]
    er = lax.broadcasted_iota(jnp.int32, (tm, nc * GLA_DK), 0)
    ec = lax.broadcasted_iota(jnp.int32, (tm, nc * GLA_DK), 1)
    own = jnp.right_shift(er, GLA_CHUNK.bit_length() - 1) == jnp.right_shift(ec, GLA_DK.bit_length() - 1)
    zero = jnp.zeros((), MXU_DTYPE)

    for h in range(GLA_HEADS):
        kc = slice(h * GLA_DK, (h + 1) * GLA_DK)
        vc = slice(h * GLA_DV, (h + 1) * GLA_DV)
        qd_h, kd_h, ku_h = qd[:, kc], kd[:, kc], ku[:, kc]
        v_h = v_ref[:, vc]
        att = jnp.where(tril, _dot_nt(qd_h, kd_h), 0.0)
        o = _dot(att.astype(MXU_DTYPE), v_h)
        qd_x = jnp.where(own, jnp.concatenate([qd_h] * nc, axis=1), zero)
        ku_x = jnp.where(own, jnp.concatenate([ku_h] * nc, axis=1), zero)
        upd = _dot_tn(v_h, ku_x)
        state = st_ref[:, kc]
        entering = []
        for c in range(nc):
            entering.append(state)
            state = state * dec[c * GLA_CHUNK:c * GLA_CHUNK + 1, kc] + upd[:, c * GLA_DK:(c + 1) * GLA_DK]
        st_ref[:, kc] = state
        s_all = jnp.concatenate(entering, axis=1).astype(MXU_DTYPE)
        o = o + _dot_nt(qd_x, s_all)
        o = _rmsnorm(o, nw)
        o_ref[:, vc] = (o * _silu(r_ref[:, vc].astype(F32))).astype(o_ref.dtype)


def _gla(p, ps, wg2, bg2, nw, *, batch, seq, tm):
    nt = seq // tm
    full = lambda a: pl.BlockSpec(a.shape, lambda b, i: (0,) * a.ndim)
    row = lambda b, i: b * nt + i
    return pl.pallas_call(
        functools.partial(_gla_kernel, tm=tm),
        out_shape=jax.ShapeDtypeStruct((p.shape[0], GLA_HEADS * GLA_DV), p.dtype),
        grid=(batch, nt),
        in_specs=[
            pl.BlockSpec((tm, 256), lambda b, i: (row(b, i), P_GLA_Q // 256)),
            pl.BlockSpec((tm, 256), lambda b, i: (row(b, i), P_GLA_K // 256)),
            pl.BlockSpec((tm, 512), lambda b, i: (row(b, i), P_GLA_V // 512)),
            pl.BlockSpec((tm, 512), lambda b, i: (row(b, i), P_GLA_R // 512)),
            pl.BlockSpec((tm, LANES), lambda b, i: (row(b, i), 0)),
            full(wg2), full(bg2), full(nw),
        ],
        out_specs=pl.BlockSpec((tm, GLA_HEADS * GLA_DV), lambda b, i: (row(b, i), 0)),
        scratch_shapes=[pltpu.VMEM((GLA_DV, GLA_HEADS * GLA_DK), F32)],
        compiler_params=_cparams(("parallel", "arbitrary"), 32 << 20),
        name="gla",
    )(p, p, p, p, ps, wg2, bg2, nw)


def _ssd_kernel(z_ref, x_ref, b_ref, c_ref, ps_ref, pst_ref, cw_ref, cb_ref, dtb_row_ref,
                dtb_col_ref, alog_row_ref, alog_col_ref, dskip_ref, nw_ref, exp_ref,
                o_ref, st_ref, tail_ref, *, tm):
    @pl.when(pl.program_id(1) == 0)
    def _():
        st_ref[...] = jnp.zeros_like(st_ref)
        tail_ref[...] = jnp.zeros_like(tail_ref)

    inner = SSD_INNER
    gw = inner // SSD_GROUPS
    xbc = jnp.concatenate([x_ref[...], b_ref[...], c_ref[...]], axis=-1).astype(F32)
    ext = jnp.concatenate([tail_ref[...], xbc], axis=0)
    tail_ref[...] = xbc[tm - SUBLANES:, :]
    conv = cb_ref[...]
    for j in range(SSD_CONV):
        shift = SSD_CONV - 1 - j
        src = ext if shift == 0 else pltpu.roll(ext, shift, axis=0)
        conv = conv + cw_ref[j:j + 1, :] * src[SUBLANES:, :]
    xc = _silu(conv)
    xs = xc[:, :inner]
    bm = xc[:, inner:inner + SSD_GROUPS * SSD_STATE].astype(MXU_DTYPE)
    cm = xc[:, inner + SSD_GROUPS * SSD_STATE:].astype(MXU_DTYPE)

    dt_exp = _dot01_right(_softplus(ps_ref[...] + dtb_row_ref[...]), exp_ref[...])
    a_row = -jnp.exp(alog_row_ref[...])
    dta = dt_exp * a_row
    tril, triu, same = _chunk_masks(tm, SSD_CHUNK)
    acum, aend = _dot01_left_multi((tril, same), dta)
    dta_t = _softplus(pst_ref[...] + dtb_col_ref[...]) * (-jnp.exp(alog_col_ref[...]))
    acum_t = _dot01_right(dta_t, triu)
    x_dt = xs * dt_exp
    xw = (x_dt * jnp.exp(aend - acum)).astype(MXU_DTYPE)
    x_dt_m = x_dt.astype(MXU_DTYPE)
    e_acum = jnp.exp(acum)
    cr = lax.broadcasted_iota(jnp.int32, (SSD_CHUNK, SSD_CHUNK), 0)
    cc = lax.broadcasted_iota(jnp.int32, (SSD_CHUNK, SSD_CHUNK), 1)
    causal = cr >= cc
    dskip = dskip_ref[...]
    nw = nw_ref[...]
    hg = SSD_HEADS // SSD_GROUPS

    for c in range(tm // SSD_CHUNK):
        r0 = c * SSD_CHUNK
        rows = slice(r0, r0 + SSD_CHUNK)
        for g in range(SSD_GROUPS):
            gl = slice(g * gw, (g + 1) * gw)
            nl = slice(g * SSD_STATE, (g + 1) * SSD_STATE)
            b_g, c_g = bm[rows, nl], cm[rows, nl]
            cb = _dot_nt(c_g, b_g)
            parts = []
            for hh in range(hg):
                h = g * hg + hh
                lane0 = h * SSD_HEADDIM
                colv = jnp.broadcast_to(acum[rows, lane0:lane0 + 1], (SSD_CHUNK, SSD_CHUNK))
                rowv = acum_t[h:h + 1, r0:r0 + SSD_CHUNK]
                decay = jnp.where(causal, jnp.exp(colv - rowv), 0.0)
                mmat = (cb * decay).astype(MXU_DTYPE)
                parts.append(_dot(mmat, x_dt_m[rows, lane0:lane0 + SSD_HEADDIM]))
            y_diag = jnp.concatenate(parts, axis=-1)
            state_t = st_ref[:, gl]
            y_off = _dot(c_g, state_t.astype(MXU_DTYPE)) * e_acum[rows, gl]
            st_ref[:, gl] = state_t * jnp.exp(aend[r0:r0 + 1, gl]) + _dot_tn(b_g, xw[rows, gl])
            y = y_diag + y_off + dskip[:, gl] * xs[rows, gl]
            y = y * _silu(z_ref[rows, gl].astype(F32))
            o_ref[rows, gl] = _rmsnorm(y, nw[:, gl]).astype(o_ref.dtype)


def _ssd(p, ps, pst, cw, cb, dtb_row, dtb_col, alog_row, alog_col, dskip, nw, expand, *, batch, seq, tm):
    nt = seq // tm
    full = lambda a: pl.BlockSpec(a.shape, lambda b, i: (0,) * a.ndim)
    row = lambda b, i: b * nt + i
    conv_dim = SSD_INNER + 2 * SSD_GROUPS * SSD_STATE
    return pl.pallas_call(
        functools.partial(_ssd_kernel, tm=tm),
        out_shape=jax.ShapeDtypeStruct((p.shape[0], SSD_INNER), p.dtype),
        grid=(batch, nt),
        in_specs=[
            pl.BlockSpec((tm, 512), lambda b, i: (row(b, i), P_SSD_Z // 512)),
            pl.BlockSpec((tm, 512), lambda b, i: (row(b, i), P_SSD_X // 512)),
            pl.BlockSpec((tm, 256), lambda b, i: (row(b, i), P_SSD_B // 256)),
            pl.BlockSpec((tm, 256), lambda b, i: (row(b, i), P_SSD_C // 256)),
            pl.BlockSpec((tm, LANES), lambda b, i: (row(b, i), 0)),
            pl.BlockSpec((SUBLANES, tm), lambda b, i: (PS_DT // SUBLANES, row(b, i))),
            full(cw), full(cb), full(dtb_row), full(dtb_col), full(alog_row), full(alog_col),
            full(dskip), full(nw), full(expand),
        ],
        out_specs=pl.BlockSpec((tm, SSD_INNER), lambda b, i: (row(b, i), 0)),
        scratch_shapes=[pltpu.VMEM((SSD_STATE, SSD_INNER), F32),
                        pltpu.VMEM((SUBLANES, conv_dim), F32)],
        compiler_params=_cparams(("parallel", "arbitrary"), 40 << 20),
        name="ssd",
    )(p, p, p, p, ps, pst, cw, cb, dtb_row, dtb_col, alog_row, alog_col, dskip, nw, expand)


def _merge_kernel(x_ref, g_ref, y0_ref, y1_ref, y2_ref, y3_ref, wsq_ref, wbr_ref, o_ref, h_ref, acc_ref,
                  *, tc):
    j = pl.program_id(1)
    d = x_ref.shape[-1]

    @pl.when(j == 0)
    def _():
        _norm_rows(x_ref, g_ref, h_ref)
        acc_ref[...] = jnp.zeros_like(acc_ref)

    def branch(y_ref):
        h = h_ref[...]
        y = y_ref[...]
        for c in range(d // tc):
            cols = slice(c * tc, (c + 1) * tc)
            gate = jax.nn.sigmoid(_dot(h, wsq_ref[:, cols]))
            acc_ref[:, cols] += gate * _dot(y, wbr_ref[:, cols])

    for b, y_ref in enumerate((y0_ref, y1_ref, y2_ref, y3_ref)):
        pl.when(j == b)(functools.partial(branch, y_ref))

    @pl.when(j == N_BRANCH)
    def _():
        merged = acc_ref[...].astype(MXU_DTYPE)
        for c in range(d // tc):
            cols = slice(c * tc, (c + 1) * tc)
            o_ref[:, cols] = x_ref[:, cols] + _dot(merged, wsq_ref[:, cols])


def _merge(x, g, ys, wsq, wbr, *, layer, tm, tc):
    m, d = x.shape
    assert len(ys) == N_BRANCH
    bw = ys[0].shape[-1]
    wbytes = jnp.dtype(wsq.dtype).itemsize
    vmem = 4 * tm * d * 4 + tm * d * wbytes + tm * d * 4 + 2 * d * d * wbytes \
        + 2 * bw * d * wbytes + 3 * N_BRANCH * tm * bw * wbytes + 4 * tm * tc * 4 + (4 << 20)
    last = N_BRANCH - 1
    y_spec = pl.BlockSpec((tm, bw), lambda i, j: (i, 0))
    return pl.pallas_call(
        functools.partial(_merge_kernel, tc=tc),
        out_shape=jax.ShapeDtypeStruct((m, d), F32),
        grid=(m // tm, N_BRANCH + 1),
        in_specs=[
            pl.BlockSpec((tm, d), lambda i, j: (i, 0)),
            pl.BlockSpec((1, d), lambda i, j: (0, 0)),
            y_spec, y_spec, y_spec, y_spec,
            pl.BlockSpec((None, d, d), lambda i, j: (layer, 0, j)),
            pl.BlockSpec((None, None, bw, d), lambda i, j: (layer, jnp.minimum(j, last), 0, 0)),
        ],
        out_specs=pl.BlockSpec((tm, d), lambda i, j: (i, 0)),
        scratch_shapes=[pltpu.VMEM((tm, d), wsq.dtype), pltpu.VMEM((tm, d), F32)],
        compiler_params=_cparams(("parallel", "arbitrary"), vmem),
        name="merge",
    )(x, g, *ys, wsq, wbr)


def _xattn_kernel(x_ref, g_ref, wq_ref, k_ref, v_ref, wo_ref, o_ref, h_ref, oc_ref, *, tc):
    _norm_rows(x_ref, g_ref, h_ref)
    h = h_ref[...]
    dh = XATTN_DH
    for hd in range(XATTN_HEADS):
        cols = slice(hd * dh, (hd + 1) * dh)
        q = _dot(h, wq_ref[:, cols]).astype(MXU_DTYPE)
        s = _dot_nt(q, k_ref[:, cols]) * (dh ** -0.5 * LOG2E)
        p = jnp.exp2(s - jnp.max(s, axis=-1, keepdims=True))
        p = p / jnp.sum(p, axis=-1, keepdims=True)
        oc_ref[:, cols] = _dot(p.astype(MXU_DTYPE), v_ref[:, cols]).astype(oc_ref.dtype)
    oc = oc_ref[...]
    d = x_ref.shape[-1]
    for c in range(d // tc):
        cols = slice(c * tc, (c + 1) * tc)
        o_ref[:, cols] = x_ref[:, cols] + _dot(oc, wo_ref[:, cols])


def _xattn(x, g, wq, kv, wo, *, layer, seq, n_mem, tm, tc):
    m, d = x.shape
    wbytes = jnp.dtype(wq.dtype).itemsize
    vmem = 4 * tm * d * 4 + 2 * tm * d * wbytes + 2 * d * d * wbytes + 4 * n_mem * d * wbytes \
        + 3 * tm * XATTN_DH * 4 + 3 * tm * n_mem * 4 + 2 * tm * tc * 4 + (6 << 20)
    tiles_per_seq = seq // tm
    return pl.pallas_call(
        functools.partial(_xattn_kernel, tc=tc),
        out_shape=jax.ShapeDtypeStruct((m, d), F32),
        grid=(m // tm,),
        in_specs=[
            pl.BlockSpec((tm, d), lambda i: (i, 0)),
            pl.BlockSpec((1, d), lambda i: (0, 0)),
            pl.BlockSpec((None, d, d), lambda i: (layer, 0, 0), pipeline_mode=pl.Buffered(1)),
            pl.BlockSpec((n_mem, d), lambda i: (i // tiles_per_seq, 0)),
            pl.BlockSpec((n_mem, d), lambda i: (i // tiles_per_seq, 1)),
            pl.BlockSpec((None, d, d), lambda i: (layer, 0, 0), pipeline_mode=pl.Buffered(1)),
        ],
        out_specs=pl.BlockSpec((tm, d), lambda i: (i, 0)),
        scratch_shapes=[pltpu.VMEM((tm, d), wq.dtype), pltpu.VMEM((tm, d), wq.dtype)],
        compiler_params=_cparams(("parallel",), vmem),
        name="xattn",
    )(x, g, wq, kv, kv, wo)


_IN_SIZES = (N_BRANCH * D_MODEL, MLA_Q_RANK, MLA_KV_RANK, MLA_ROPE, 3 * FOX_HEADS * FOX_DH, FOX_HEADS,
             GLA_HEADS * GLA_DK, GLA_HEADS * GLA_DK, GLA_HEADS * GLA_DV, GLA_GATE_RANK,
             GLA_HEADS * GLA_DV, SSD_INNER, SSD_INNER + 2 * SSD_GROUPS * SSD_STATE, SSD_HEADS)


def _rot_cols(w):
    half = w.shape[-1] // 2
    return jnp.concatenate([-w[..., half:], w[..., :half]], axis=-1)


def _prep_w_in(w_mix):
    w_in = w_mix.astype(MXU_DTYPE)
    nl, d = w_in.shape[:2]
    offs = np.cumsum((0,) + _IN_SIZES[1:])
    assert offs[-1] == w_in.shape[2]
    seg = lambda i: w_in[:, :, offs[i]:offs[i + 1]]
    (cq, ckv, kr, fox, fox_f, gq, gk, gv, glr, gr, sz, sxbc, sdt) = [seg(i) for i in range(len(_IN_SIZES) - 1)]
    z = lambda n: jnp.zeros((nl, d, n), w_in.dtype)
    w_p = jnp.concatenate([
        cq, z(512 - MLA_Q_RANK), ckv, kr, z(LANES - MLA_ROPE), _rot_cols(kr), z(LANES - MLA_ROPE),
        fox, gq, gk, gv, gr, sz, sxbc], axis=2)
    assert w_p.shape[2] == P_WIDTH
    cs = np.ones((1, P_WIDTH), np.float32)
    cs[:, P_FOX_Q:P_FOX_K] = FOX_DH ** -0.5 * LOG2E
    cs[:, P_GLA_Q:P_GLA_K] = GLA_DK ** -0.5
    w_s = jnp.concatenate([glr, sdt, fox_f, z(LANES - GLA_GATE_RANK - SSD_HEADS - FOX_HEADS)], axis=2)
    return w_p.astype(MXU_DTYPE), jnp.asarray(cs), w_s.astype(MXU_DTYPE)


def _prep_mla(w_uq, w_ukv):
    nl, r = w_uq.shape[:2]
    wq = w_uq.reshape(nl, r, MLA_HEADS, MLA_NOPE + MLA_ROPE)
    nope, pe = wq[..., :MLA_NOPE], wq[..., MLA_NOPE:]
    zpad = jnp.zeros((nl, r, MLA_HEADS, MLA_DK_PAD - MLA_NOPE - MLA_ROPE), w_uq.dtype)
    wa = jnp.concatenate([nope, pe, zpad], axis=-1).reshape(nl, r, MLA_HEADS * MLA_DK_PAD)
    wb = jnp.concatenate([_rot_cols(pe), zpad], axis=-1).reshape(nl, r, MLA_HEADS * LANES)
    rk = w_ukv.shape[1]
    wkv = w_ukv.reshape(nl, rk, MLA_HEADS, MLA_NOPE + MLA_V)
    wkv = jnp.concatenate([wkv[..., :MLA_NOPE].reshape(nl, rk, -1), wkv[..., MLA_NOPE:].reshape(nl, rk, -1)],
                          axis=2)
    return wa.astype(MXU_DTYPE), wb.astype(MXU_DTYPE), wkv.astype(MXU_DTYPE)


def _tiles(m, seq):
    mid = min(512, seq)
    return dict(ffn_tm=min(1024, seq), ffn_tf=256, ffn_tc=1024, proj_tm=mid, proj_tn=2816, prep_tm=min(1024, seq),
                flash_tb=mid, flash_rmult=min(2, seq // mid), gla_tm=min(512, seq), ssd_tm=min(256, seq), merge_tm=mid, merge_tc=512,
                xattn_tm=mid, xattn_tc=512, rope_tm=min(1024, m))


@jax.jit
def kernel(x, mem, positions, ffn1_norm, ffn1_w_gate, ffn1_w_up, ffn1_w_down, mix_norm, w_in, mla_q_norm, mla_kv_norm, mla_w_uq, mla_w_ukv, fox_f_bias, gla_w_g2, gla_b_g2, gla_norm, ssd_conv_w, ssd_conv_b, ssd_dt_bias, ssd_a_log, ssd_d, ssd_norm, w_branch, w_out, xattn_norm, mem_norm, xattn_w_q, xattn_w_k, xattn_w_v, xattn_w_o, ffn2_norm, ffn2_w_gate, ffn2_w_up, ffn2_w_down, final_norm):
    batch, seq, d = x.shape
    n_mem = mem.shape[1]
    m = batch * seq
    t = _tiles(m, seq)
    depth = w_in.shape[0]
    cast = lambda a: a.astype(MXU_DTYPE)
    row = lambda a: a.reshape(1, -1).astype(F32)
    col8 = lambda v: jnp.zeros((SUBLANES, 1), F32).at[:v.shape[0], 0].set(v.astype(F32))

    xs = x.reshape(m, d)
    mem2 = mem.reshape(batch * n_mem, d)
    ct, st = _rope_tables(positions.reshape(m, 1), tm=t["rope_tm"])

    expand = np.zeros((LANES, SSD_INNER), np.float32)
    for h in range(SSD_HEADS):
        expand[PS_DT + h, h * SSD_HEADDIM:(h + 1) * SSD_HEADDIM] = 1.0
    expand = jnp.asarray(expand, MXU_DTYPE)
    rep = lambda v: jnp.repeat(v.astype(F32), SSD_HEADDIM).reshape(1, SSD_INNER)
    ffn = functools.partial(_ffn, tm=t["ffn_tm"], tf=t["ffn_tf"], tc=t["ffn_tc"])
    flash = functools.partial(_flash, batch=batch, seq=seq, groups=2, tb=t["flash_tb"], rmult=t["flash_rmult"])

    f1g, f1u, f1d = cast(ffn1_w_gate), cast(ffn1_w_up), cast(ffn1_w_down)
    f2g, f2u, f2d = cast(ffn2_w_gate), cast(ffn2_w_up), cast(ffn2_w_down)
    w_p, cs, w_s = _prep_w_in(w_in[:, :, N_BRANCH * d:])
    wa, wb, wkv = _prep_mla(mla_w_uq, mla_w_ukv)
    wsq = jnp.concatenate([cast(w_in[:, :, :N_BRANCH * d]), cast(w_out)], axis=2)
    wbr = cast(w_branch)
    w_kv = cast(jnp.concatenate([xattn_w_k, xattn_w_v], axis=2))
    wq_x, wo_x = cast(xattn_w_q), cast(xattn_w_o)

    for l in range(depth):
        xs = ffn(xs, row(ffn1_norm[l]), f1g, f1u, f1d, row(final_norm), layer=l, final_norm=False)

        p, ps, pst = _proj(xs, row(mix_norm[l]), w_p, cs, w_s, layer=l, tm=t["proj_tm"], tn=t["proj_tn"])

        q_mla, k_mla, v_mla = _mla_prep(p, ct, st, row(mla_q_norm[l]), row(mla_kv_norm[l]), wa, wb, wkv,
                                        layer=l, tm=t["prep_tm"])
        y_mla = flash(q_mla, k_mla, v_mla, None, heads=MLA_HEADS,
                      dk=MLA_DK_PAD, dv=MLA_V, q_off=0, k_off=0, v_off=0)

        c_fox = _fox_cum(pst, col8(fox_f_bias[l]), batch=batch, seq=seq)
        y_fox = flash(p, p, p, c_fox, heads=FOX_HEADS,
                      dk=FOX_DH, dv=FOX_DH, q_off=P_FOX_Q, k_off=P_FOX_K, v_off=P_FOX_V)

        wg2 = jnp.zeros((LANES, GLA_HEADS * GLA_DK), F32).at[PS_GLR:PS_GLR + GLA_GATE_RANK].set(gla_w_g2[l])
        y_gla = _gla(p, ps, cast(wg2), row(gla_b_g2[l]), row(gla_norm[l]), batch=batch, seq=seq, tm=t["gla_tm"])

        dtb_row = jnp.zeros((1, LANES), F32).at[0, PS_DT:PS_DT + SSD_HEADS].set(ssd_dt_bias[l])
        y_ssd = _ssd(p, ps, pst, ssd_conv_w[l].astype(F32), row(ssd_conv_b[l]), dtb_row, col8(ssd_dt_bias[l]),
                     rep(ssd_a_log[l]), col8(ssd_a_log[l]), rep(ssd_d[l]), row(ssd_norm[l]), expand,
                     batch=batch, seq=seq, tm=t["ssd_tm"])

        xs = _merge(xs, row(mix_norm[l]), (y_mla, y_fox, y_gla, y_ssd), wsq, wbr, layer=l, tm=t["merge_tm"], tc=t["merge_tc"])

        kv = _norm_matmul(mem2, row(mem_norm[l]), w_kv, layer=l, tm=min(512, batch * n_mem), tn=1024)
        xs = _xattn(xs, row(xattn_norm[l]), wq_x, kv, wo_x, layer=l,
                    seq=seq, n_mem=n_mem, tm=t["xattn_tm"], tc=t["xattn_tc"])

        xs = ffn(xs, row(ffn2_norm[l]), f2g, f2u, f2d, row(final_norm), layer=l, final_norm=(l == depth - 1))

    return xs.reshape(batch, seq, d)
```

```python
import functools

import numpy as np
import jax
import jax.numpy as jnp
from jax import lax
from jax.experimental import pallas as pl
from jax.experimental.pallas import tpu as pltpu

D_MODEL = 2048
DEPTH = 4
EPS = 1e-6
N_BRANCH = 4
BRANCH_WIDTH = 512
MLA_HEADS, MLA_NOPE, MLA_ROPE, MLA_V = 4, 128, 64, 128
MLA_Q_RANK, MLA_KV_RANK = 384, 256
MLA_DK_PAD = 256
ROPE_THETA = 10000.0
FOX_HEADS, FOX_DH = 4, 128
GLA_HEADS, GLA_DK, GLA_DV = 4, 64, 128
GLA_GATE_RANK, GLA_GATE_NORM, GLA_CHUNK = 16, 16.0, 64
SSD_HEADS, SSD_HEADDIM, SSD_GROUPS, SSD_STATE, SSD_CONV, SSD_CHUNK = 8, 64, 2, 128, 4, 128
SSD_INNER = SSD_HEADS * SSD_HEADDIM
XATTN_HEADS = 4
XATTN_DH = D_MODEL // XATTN_HEADS
D_FF = 5632

LANES = 128
SUBLANES = 8
V7X_VMEM_LIMIT_BYTES = 60 * 1024 * 1024

MXU_DTYPE = jnp.bfloat16
FFN_SLOTS = 3
F32 = jnp.float32
LOG2E = 1.4426950408889634

P_CQ, P_CKV, P_KR, P_KRROT = 0, 512, 768, 896
P_FOX_Q, P_FOX_K, P_FOX_V = 1024, 1536, 2048
P_GLA_Q, P_GLA_K, P_GLA_V, P_GLA_R = 2560, 2816, 3072, 3584
P_SSD_Z, P_SSD_X, P_SSD_B, P_SSD_C = 4096, 4608, 5120, 5376
P_WIDTH = 5632
PS_GLR, PS_DT, PS_F = 0, 16, 24
PST_ROWS = 32


def _cparams(sem, vmem_bytes):
    return pltpu.CompilerParams(dimension_semantics=sem,
                                vmem_limit_bytes=int(min(vmem_bytes, V7X_VMEM_LIMIT_BYTES)))


def _dot(a, b):
    return jnp.dot(a, b, preferred_element_type=F32)


def _dot_nt(a, b):
    return lax.dot_general(a, b, (((1,), (1,)), ((), ())), preferred_element_type=F32)


def _dot_tn(a, b):
    return lax.dot_general(a, b, (((0,), (0,)), ((), ())), preferred_element_type=F32)


def _split3(x):
    hi = x.astype(MXU_DTYPE)
    r1 = x - hi.astype(F32)
    mid = r1.astype(MXU_DTYPE)
    lo = (r1 - mid.astype(F32)).astype(MXU_DTYPE)
    return hi, mid, lo


def _dot01_right(x, m01):
    if MXU_DTYPE == F32:
        return _dot(x, m01.astype(F32))
    m = m01.astype(MXU_DTYPE)
    hi, mid, lo = _split3(x)
    return _dot(hi, m) + _dot(mid, m) + _dot(lo, m)


def _dot01_left_multi(m01s, x):
    if MXU_DTYPE == F32:
        return [_dot(m01.astype(F32), x) for m01 in m01s]
    parts = _split3(x)
    out = []
    for m01 in m01s:
        m = m01.astype(MXU_DTYPE)
        out.append(_dot(m, parts[0]) + _dot(m, parts[1]) + _dot(m, parts[2]))
    return out


def _dot01_left(m01, x):
    return _dot01_left_multi((m01,), x)[0]


def _rmsnorm(x, g):
    ms = jnp.mean(x * x, axis=-1, keepdims=True)
    return x * lax.rsqrt(ms + EPS) * g


def _log_sigmoid(x):
    return jnp.minimum(x, 0.0) - jnp.log1p(jnp.exp(-jnp.abs(x)))


def _softplus(x):
    return jnp.maximum(x, 0.0) + jnp.log1p(jnp.exp(-jnp.abs(x)))


def _silu(x):
    return x * jax.nn.sigmoid(x)


def _chunk_masks(n, chunk):
    shift = chunk.bit_length() - 1
    assert 1 << shift == chunk
    r = lax.broadcasted_iota(jnp.int32, (n, n), 0)
    c = lax.broadcasted_iota(jnp.int32, (n, n), 1)
    same = jnp.right_shift(r, shift) == jnp.right_shift(c, shift)
    return jnp.logical_and(same, r >= c), jnp.logical_and(same, r <= c), same


def _norm_rows(x_ref, g_ref, h_ref, copy_ref=None, *, rows=256):
    tm = x_ref.shape[0]
    rows = min(rows, tm)
    g = g_ref[...]
    for r in range(tm // rows):
        sl = slice(r * rows, (r + 1) * rows)
        x = x_ref[sl, :]
        h_ref[sl, :] = _rmsnorm(x, g).astype(h_ref.dtype)
        if copy_ref is not None:
            copy_ref[sl, :] = x


def _ffn_kernel(x_ref, g_ref, wg_hbm, wu_hbm, wd_hbm, fg_ref, o_ref, h_ref, wg_buf, wu_buf, wd_buf, sem,
                *, final_norm, layer, tf, tc, n_chunks):
    i = pl.program_id(0)

    def chunk_copies(j, slot):
        cols = pl.ds(j * tf, tf)
        return (pltpu.make_async_copy(wg_hbm.at[layer, :, cols], wg_buf.at[slot], sem.at[0, slot]),
                pltpu.make_async_copy(wu_hbm.at[layer, :, cols], wu_buf.at[slot], sem.at[1, slot]),
                pltpu.make_async_copy(wd_hbm.at[layer, cols, :], wd_buf.at[slot], sem.at[2, slot]))

    def start(j, slot):
        for cp in chunk_copies(j, slot):
            cp.start()

    @pl.when(i == 0)
    def _():
        start(0, 0)

    _norm_rows(x_ref, g_ref, h_ref, o_ref)
    h = h_ref[...]
    held = None
    for j in range(n_chunks):
        slot = j % FFN_SLOTS
        for cp in chunk_copies(j, slot):
            cp.wait()
        if j + 1 < n_chunks:
            start(j + 1, (j + 1) % FFN_SLOTS)
        else:
            @pl.when(i + 1 < pl.num_programs(0))
            def _():
                start(0, 0)
        a = _dot(h, wg_buf[slot])
        u = _dot(h, wu_buf[slot])
        t = (_silu(a) * (0.5 * u)).astype(MXU_DTYPE)
        if j % 2 == 0 and j + 1 < n_chunks:
            held = (t, slot)
            continue
        d = o_ref.shape[-1]
        for c in range(d // tc):
            cols = slice(c * tc, (c + 1) * tc)
            part = _dot(t, wd_buf[slot, :, cols])
            if held is not None:
                part = _dot(held[0], wd_buf[held[1], :, cols]) + part
            o_ref[:, cols] += part
        held = None

    if final_norm:
        _norm_rows(o_ref, fg_ref, o_ref)


def _ffn(x, g, wg, wu, wd, fg, *, layer, final_norm, tm, tf, tc):
    m, d = x.shape
    f = wg.shape[-1]
    n_chunks = f // tf
    assert n_chunks * tf == f and n_chunks > FFN_SLOTS and (n_chunks - 1) % FFN_SLOTS != 0
    wb = jnp.dtype(wg.dtype).itemsize
    vmem = 4 * tm * d * 4 + tm * d * wb + FFN_SLOTS * 3 * d * tf * wb + 4 * tm * tf * 4 + 2 * tm * tc * 4 + (4 << 20)
    return pl.pallas_call(
        functools.partial(_ffn_kernel, final_norm=final_norm, layer=layer, tf=tf, tc=tc, n_chunks=n_chunks),
        out_shape=jax.ShapeDtypeStruct((m, d), F32),
        grid=(m // tm,),
        in_specs=[
            pl.BlockSpec((tm, d), lambda i: (i, 0)),
            pl.BlockSpec((1, d), lambda i: (0, 0)),
            pl.BlockSpec(memory_space=pl.ANY),
            pl.BlockSpec(memory_space=pl.ANY),
            pl.BlockSpec(memory_space=pl.ANY),
            pl.BlockSpec((1, d), lambda i: (0, 0)),
        ],
        out_specs=pl.BlockSpec((tm, d), lambda i: (i, 0)),
        scratch_shapes=[pltpu.VMEM((tm, d), wg.dtype),
                        pltpu.VMEM((FFN_SLOTS, d, tf), wg.dtype),
                        pltpu.VMEM((FFN_SLOTS, d, tf), wg.dtype),
                        pltpu.VMEM((FFN_SLOTS, tf, d), wg.dtype),
                        pltpu.SemaphoreType.DMA((3, FFN_SLOTS))],
        compiler_params=_cparams(("arbitrary",), vmem),
        name="ffn",
    )(x, g, wg, wu, wd, fg)


def _proj_kernel(x_ref, g_ref, w_ref, cs_ref, ws_ref, p_ref, ps_ref, pst_ref, h_ref):
    j = pl.program_id(1)

    @pl.when(j == 0)
    def _():
        _norm_rows(x_ref, g_ref, h_ref)
        ps = _dot(h_ref[...], ws_ref[...])
        ps_ref[...] = ps
        pst_ref[...] = jnp.transpose(ps)[:PST_ROWS, :]

    p_ref[...] = (_dot(h_ref[...], w_ref[...]) * cs_ref[...]).astype(p_ref.dtype)


def _proj(x, g, w, cs, ws, *, layer, tm, tn):
    m, d = x.shape
    n = w.shape[-1]
    wb = jnp.dtype(w.dtype).itemsize
    vmem = 2 * tm * d * 4 + tm * d * wb + 2 * d * tn * wb + 2 * tm * tn * wb + 2 * tm * tn * 4 + (6 << 20)
    return pl.pallas_call(
        _proj_kernel,
        out_shape=(jax.ShapeDtypeStruct((m, n), w.dtype),
                   jax.ShapeDtypeStruct((m, LANES), F32),
                   jax.ShapeDtypeStruct((PST_ROWS, m), F32)),
        grid=(m // tm, n // tn),
        in_specs=[
            pl.BlockSpec((tm, d), lambda i, j: (i, 0)),
            pl.BlockSpec((1, d), lambda i, j: (0, 0)),
            pl.BlockSpec((None, d, tn), lambda i, j: (layer, 0, j)),
            pl.BlockSpec((1, tn), lambda i, j: (0, j)),
            pl.BlockSpec((None, d, LANES), lambda i, j: (layer, 0, 0)),
        ],
        out_specs=(pl.BlockSpec((tm, tn), lambda i, j: (i, j)),
                   pl.BlockSpec((tm, LANES), lambda i, j: (i, 0)),
                   pl.BlockSpec((PST_ROWS, tm), lambda i, j: (0, i))),
        scratch_shapes=[pltpu.VMEM((tm, d), w.dtype)],
        compiler_params=_cparams(("parallel", "arbitrary"), vmem),
        name="proj",
    )(x, g, w, cs, ws)


def _norm_matmul_kernel(x_ref, g_ref, w_ref, o_ref):
    h = _rmsnorm(x_ref[...], g_ref[...]).astype(w_ref.dtype)
    o_ref[...] = _dot(h, w_ref[...]).astype(o_ref.dtype)


def _norm_matmul(x, g, w, *, layer, tm, tn):
    m, d = x.shape
    n = w.shape[-1]
    wb = jnp.dtype(w.dtype).itemsize
    vmem = 3 * tm * d * 4 + 2 * d * tn * wb + 3 * tm * tn * 4 + (4 << 20)
    return pl.pallas_call(
        _norm_matmul_kernel,
        out_shape=jax.ShapeDtypeStruct((m, n), w.dtype),
        grid=(m // tm, n // tn),
        in_specs=[pl.BlockSpec((tm, d), lambda i, j: (i, 0)),
                  pl.BlockSpec((1, d), lambda i, j: (0, 0)),
                  pl.BlockSpec((None, d, tn), lambda i, j: (layer, 0, j))],
        out_specs=pl.BlockSpec((tm, tn), lambda i, j: (i, j)),
        compiler_params=_cparams(("parallel", "parallel"), vmem),
        name="mem_kv",
    )(x, g, w)


def _rope_kernel(pos_ref, ct_ref, st_ref):
    pos = pos_ref[...].astype(F32)
    lane = lax.broadcasted_iota(jnp.int32, (1, LANES), 1)
    half = MLA_ROPE // 2
    f = jnp.bitwise_and(lane, half - 1).astype(F32)
    inv_freq = jnp.power(jnp.float32(ROPE_THETA), -f / half)
    ang = pos * inv_freq
    valid = lane < MLA_ROPE
    ct_ref[...] = jnp.where(valid, jnp.cos(ang), 0.0)
    st_ref[...] = jnp.where(valid, jnp.sin(ang), 0.0)


def _rope_tables(pos_col, *, tm):
    m = pos_col.shape[0]
    return pl.pallas_call(
        _rope_kernel,
        out_shape=(jax.ShapeDtypeStruct((m, LANES), F32), jax.ShapeDtypeStruct((m, LANES), F32)),
        grid=(m // tm,),
        in_specs=[pl.BlockSpec((tm, 1), lambda i: (i, 0))],
        out_specs=(pl.BlockSpec((tm, LANES), lambda i: (i, 0)),
                   pl.BlockSpec((tm, LANES), lambda i: (i, 0))),
        compiler_params=_cparams(("parallel",), 16 << 20),
        name="rope_tables",
    )(pos_col)


def _mla_prep_kernel(cq_ref, ckv_ref, kr_ref, krrot_ref, ct_ref, st_ref, qn_ref, kvn_ref,
                     wa_ref, wb_ref, wkv_ref, q_ref, k_ref, v_ref):
    ct = ct_ref[...]
    st = st_ref[...]
    scale = (MLA_NOPE + MLA_ROPE) ** -0.5 * LOG2E

    ckv = ckv_ref[...].astype(F32)
    ckvn = _rmsnorm(ckv, kvn_ref[...]).astype(MXU_DTYPE)
    kv = _dot(ckvn, wkv_ref[...])
    kpe = kr_ref[...].astype(F32) * ct + krrot_ref[...].astype(F32) * st
    kpe = kpe.astype(k_ref.dtype)
    hw = MLA_HEADS * MLA_NOPE
    for h in range(MLA_HEADS):
        base = h * MLA_DK_PAD
        k_ref[:, base:base + MLA_NOPE] = kv[:, h * MLA_NOPE:(h + 1) * MLA_NOPE].astype(k_ref.dtype)
        k_ref[:, base + MLA_NOPE:base + MLA_DK_PAD] = kpe
    v_ref[...] = kv[:, hw:].astype(v_ref.dtype)

    cq = cq_ref[...].astype(F32)[:, :MLA_Q_RANK]
    cqn = _rmsnorm(cq, qn_ref[...]).astype(MXU_DTYPE)
    qa = _dot(cqn, wa_ref[...])
    qb = _dot(cqn, wb_ref[...])
    for h in range(MLA_HEADS):
        base = h * MLA_DK_PAD
        q_ref[:, base:base + MLA_NOPE] = (qa[:, base:base + MLA_NOPE] * scale).astype(q_ref.dtype)
        pe = qa[:, base + MLA_NOPE:base + MLA_DK_PAD] * ct + qb[:, h * LANES:(h + 1) * LANES] * st
        q_ref[:, base + MLA_NOPE:base + MLA_DK_PAD] = (pe * scale).astype(q_ref.dtype)


def _mla_prep(p, ct, st, qn, kvn, wa, wb, wkv, *, layer, tm):
    m = p.shape[0]
    qk_w = MLA_HEADS * MLA_DK_PAD
    v_w = MLA_HEADS * MLA_V
    full = lambda a: pl.BlockSpec(a.shape, lambda i: (0,) * a.ndim)
    of_layer = lambda a: pl.BlockSpec((None,) + a.shape[1:], lambda i: (layer,) + (0,) * (a.ndim - 1))
    return pl.pallas_call(
        _mla_prep_kernel,
        out_shape=(jax.ShapeDtypeStruct((m, qk_w), p.dtype),
                   jax.ShapeDtypeStruct((m, qk_w), p.dtype),
                   jax.ShapeDtypeStruct((m, v_w), p.dtype)),
        grid=(m // tm,),
        in_specs=[
            pl.BlockSpec((tm, 512), lambda i: (i, P_CQ // 512)),
            pl.BlockSpec((tm, 256), lambda i: (i, P_CKV // 256)),
            pl.BlockSpec((tm, 128), lambda i: (i, P_KR // 128)),
            pl.BlockSpec((tm, 128), lambda i: (i, P_KRROT // 128)),
            pl.BlockSpec((tm, LANES), lambda i: (i, 0)),
            pl.BlockSpec((tm, LANES), lambda i: (i, 0)),
            full(qn), full(kvn), of_layer(wa), of_layer(wb), of_layer(wkv),
        ],
        out_specs=(pl.BlockSpec((tm, qk_w), lambda i: (i, 0)),
                   pl.BlockSpec((tm, qk_w), lambda i: (i, 0)),
                   pl.BlockSpec((tm, v_w), lambda i: (i, 0))),
        compiler_params=_cparams(("parallel",), 40 << 20),
        name="mla_prep",
    )(p, p, p, p, ct, st, qn, kvn, wa, wb, wkv)


def _fox_cum_kernel(ft_ref, fb_ref, c_ref, *, seq):
    lf = _log_sigmoid(ft_ref[...] + fb_ref[...])
    r = lax.broadcasted_iota(jnp.int32, (LANES, LANES), 0)
    c = lax.broadcasted_iota(jnp.int32, (LANES, LANES), 1)
    upper = r <= c
    carry = jnp.zeros((SUBLANES, 1), F32)
    for blk in range(seq // LANES):
        sl = slice(blk * LANES, (blk + 1) * LANES)
        cs = _dot01_right(lf[:, sl], upper) + carry
        for h in range(FOX_HEADS):
            c_ref[h, :, sl] = cs[h:h + 1, :] * LOG2E
        carry = cs[:, LANES - 1:LANES]


def _fox_cum(pst, fb_col, *, batch, seq):
    return pl.pallas_call(
        functools.partial(_fox_cum_kernel, seq=seq),
        out_shape=jax.ShapeDtypeStruct((batch * FOX_HEADS, 1, seq), F32),
        grid=(batch,),
        in_specs=[pl.BlockSpec((SUBLANES, seq), lambda b: (PS_F // SUBLANES, b)),
                  pl.BlockSpec((SUBLANES, 1), lambda b: (0, 0))],
        out_specs=pl.BlockSpec((FOX_HEADS, 1, seq), lambda b: (b, 0, 0)),
        compiler_params=_cparams(("parallel",), 16 << 20),
        name="fox_cum",
    )(pst, fb_col)


def _flash_kernel(*refs, seq, tb, rmult, groups, dk, dv, has_bias):
    refs = list(refs)
    q_ref, k_ref, v_ref = refs[:3]
    c_ref = refs[3] if has_bias else None
    o_ref = refs[3 + int(has_bias)]
    rows_st = tb * rmult

    def update(state, s, v_blk):
        m, l, acc = state
        m_new = jnp.maximum(m, jnp.max(s, axis=-1, keepdims=True))
        alpha = jnp.exp2(m - m_new)
        p = jnp.exp2(s - m_new)
        l = alpha * l + jnp.sum(p, axis=-1, keepdims=True)
        acc = alpha * acc + _dot(p.astype(MXU_DTYPE), v_blk)
        return m_new, l, acc

    def super_tile(pj, unused):
        r0 = pl.multiple_of(pj * rows_st, rows_st)
        qs = [q_ref[pl.ds(r0, rows_st), g * dk:(g + 1) * dk] for g in range(groups)]
        if has_bias:
            c0 = [jnp.max(c_ref[g, :, pl.ds(r0, tb)], axis=-1, keepdims=True) for g in range(groups)]

        def scores(q, g, start, width=tb):
            s = _dot_nt(q, k_ref[pl.ds(start, width), g * dk:(g + 1) * dk])
            if has_bias:
                s = s + (c0[g] - c_ref[g, :, pl.ds(start, width)])
            return s

        def v_block(g, start, width=tb):
            return v_ref[pl.ds(start, width), g * dv:(g + 1) * dv]

        def body(ki, carry):
            start = pl.multiple_of(ki * rows_st, rows_st)
            return tuple(update(carry[g], scores(qs[g], g, start, rows_st), v_block(g, start, rows_st))
                         for g in range(groups))

        init = tuple((jnp.full((rows_st, 1), -jnp.inf, F32), jnp.zeros((rows_st, 1), F32),
                      jnp.zeros((rows_st, dv), F32)) for _ in range(groups))
        carry = lax.fori_loop(0, pj, body, init)

        for g in range(groups):
            m, l, acc = carry[g]
            for a in range(rmult):
                rs = slice(a * tb, (a + 1) * tb)
                width = (a + 1) * tb
                row = lax.broadcasted_iota(jnp.int32, (tb, width), 0)
                col = lax.broadcasted_iota(jnp.int32, (tb, width), 1)
                s = jnp.where(col <= row + a * tb, scores(qs[g][rs], g, r0, width), -jnp.inf)
                state = update((m[rs], l[rs], acc[rs]), s, v_block(g, r0, width))
                o_ref[pl.ds(pl.multiple_of(r0 + a * tb, tb), tb), g * dv:(g + 1) * dv] = \
                    (state[2] / state[1]).astype(o_ref.dtype)
        return unused

    lax.fori_loop(0, seq // rows_st, super_tile, 0)


def _flash(q_arr, k_arr, v_arr, c_arr, *, batch, seq, heads, groups, dk, dv,
           q_off, k_off, v_off, tb, rmult):
    rows_st = tb * rmult
    has_bias = c_arr is not None
    gk, gv = groups * dk, groups * dv
    in_specs = [pl.BlockSpec((seq, gk), lambda b, h: (b, q_off // gk + h)),
                pl.BlockSpec((seq, gk), lambda b, h: (b, k_off // gk + h)),
                pl.BlockSpec((seq, gv), lambda b, h: (b, v_off // gv + h))]
    args = [q_arr, k_arr, v_arr]
    if has_bias:
        in_specs.append(pl.BlockSpec((groups, 1, seq), lambda b, h: (b * (heads // groups) + h, 0, 0)))
        args.append(c_arr)
    vmem = 2 * seq * (2 * gk + 2 * gv) * 2 + groups * (3 * rows_st * rows_st * 4 + 4 * rows_st * dv * 4) + (8 << 20)
    return pl.pallas_call(
        functools.partial(_flash_kernel, seq=seq, tb=tb, rmult=rmult, groups=groups, dk=dk, dv=dv,
                          has_bias=has_bias),
        out_shape=jax.ShapeDtypeStruct((batch * seq, heads * dv), q_arr.dtype),
        grid=(batch, heads // groups),
        in_specs=in_specs,
        out_specs=pl.BlockSpec((seq, gv), lambda b, h: (b, h)),
        compiler_params=_cparams(("parallel", "parallel"), vmem),
        name="flash_fox" if has_bias else "flash_mla",
    )(*args)


def _gla_kernel(q_ref, k_ref, v_ref, r_ref, ps_ref, wg2_ref, bg2_ref, nw_ref, o_ref, st_ref, *, tm):
    @pl.when(pl.program_id(1) == 0)
    def _():
        st_ref[...] = jnp.zeros_like(st_ref)

    nc = tm // GLA_CHUNK
    glin = _dot(ps_ref[...].astype(MXU_DTYPE), wg2_ref[...]) + bg2_ref[...]
    gl = _log_sigmoid(glin) / GLA_GATE_NORM
    tril, _, same = _chunk_masks(tm, GLA_CHUNK)
    gcum, gend = _dot01_left_multi((tril, same), gl)
    q = q_ref[...].astype(F32)
    k = k_ref[...].astype(F32)
    qd = (q * jnp.exp(gcum)).astype(MXU_DTYPE)
    kd = (k * jnp.exp(-gcum)).astype(MXU_DTYPE)
    ku = (k * jnp.exp(gend - gcum)).astype(MXU_DTYPE)
    dec = jnp.exp(gend)
    nw = nw_ref[...]
    er = lax.broadcasted_iota(jnp.int32, (tm, nc * GLA_DK), 0)
    ec = lax.broadcasted_iota(jnp.int32, (tm, nc * GLA_DK), 1)
    own = jnp.right_shift(er, GLA_CHUNK.bit_length() - 1) == jnp.right_shift(ec, GLA_DK.bit_length() - 1)
    zero = jnp.zeros((), MXU_DTYPE)

    for h in range(GLA_HEADS):
        kc = slice(h * GLA_DK, (h + 1) * GLA_DK)
        vc = slice(h * GLA_DV, (h + 1) * GLA_DV)
        qd_h, kd_h, ku_h = qd[:, kc], kd[:, kc], ku[:, kc]
        v_h = v_ref[:, vc]
        att = jnp.where(tril, _dot_nt(qd_h, kd_h), 0.0)
        o = _dot(att.astype(MXU_DTYPE), v_h)
        qd_x = jnp.where(own, jnp.concatenate([qd_h] * nc, axis=1), zero)
        ku_x = jnp.where(own, jnp.concatenate([ku_h] * nc, axis=1), zero)
        upd = _dot_tn(v_h, ku_x)
        state = st_ref[:, kc]
        entering = []
        for c in range(nc):
            entering.append(state)
            state = state * dec[c * GLA_CHUNK:c * GLA_CHUNK + 1, kc] + upd[:, c * GLA_DK:(c + 1) * GLA_DK]
        st_ref[:, kc] = state
        s_all = jnp.concatenate(entering, axis=1).astype(MXU_DTYPE)
        o = o + _dot_nt(qd_x, s_all)
        o = _rmsnorm(o, nw)
        o_ref[:, vc] = (o * _silu(r_ref[:, vc].astype(F32))).astype(o_ref.dtype)


def _gla(p, ps, wg2, bg2, nw, *, batch, seq, tm):
    nt = seq // tm
    full = lambda a: pl.BlockSpec(a.shape, lambda b, i: (0,) * a.ndim)
    row = lambda b, i: b * nt + i
    return pl.pallas_call(
        functools.partial(_gla_kernel, tm=tm),
        out_shape=jax.ShapeDtypeStruct((p.shape[0], GLA_HEADS * GLA_DV), p.dtype),
        grid=(batch, nt),
        in_specs=[
            pl.BlockSpec((tm, 256), lambda b, i: (row(b, i), P_GLA_Q // 256)),
            pl.BlockSpec((tm, 256), lambda b, i: (row(b, i), P_GLA_K // 256)),
            pl.BlockSpec((tm, 512), lambda b, i: (row(b, i), P_GLA_V // 512)),
            pl.BlockSpec((tm, 512), lambda b, i: (row(b, i), P_GLA_R // 512)),
            pl.BlockSpec((tm, LANES), lambda b, i: (row(b, i), 0)),
            full(wg2), full(bg2), full(nw),
        ],
        out_specs=pl.BlockSpec((tm, GLA_HEADS * GLA_DV), lambda b, i: (row(b, i), 0)),
        scratch_shapes=[pltpu.VMEM((GLA_DV, GLA_HEADS * GLA_DK), F32)],
        compiler_params=_cparams(("parallel", "arbitrary"), 32 << 20),
        name="gla",
    )(p, p, p, p, ps, wg2, bg2, nw)


def _ssd_kernel(z_ref, x_ref, b_ref, c_ref, ps_ref, pst_ref, cw_ref, cb_ref, dtb_row_ref,
                dtb_col_ref, alog_row_ref, alog_col_ref, dskip_ref, nw_ref, exp_ref,
                o_ref, st_ref, tail_ref, *, tm):
    @pl.when(pl.program_id(1) == 0)
    def _():
        st_ref[...] = jnp.zeros_like(st_ref)
        tail_ref[...] = jnp.zeros_like(tail_ref)

    inner = SSD_INNER
    gw = inner // SSD_GROUPS
    xbc = jnp.concatenate([x_ref[...], b_ref[...], c_ref[...]], axis=-1).astype(F32)
    ext = jnp.concatenate([tail_ref[...], xbc], axis=0)
    tail_ref[...] = xbc[tm - SUBLANES:, :]
    conv = cb_ref[...]
    for j in range(SSD_CONV):
        shift = SSD_CONV - 1 - j
        src = ext if shift == 0 else pltpu.roll(ext, shift, axis=0)
        conv = conv + cw_ref[j:j + 1, :] * src[SUBLANES:, :]
    xc = _silu(conv)
    xs = xc[:, :inner]
    bm = xc[:, inner:inner + SSD_GROUPS * SSD_STATE].astype(MXU_DTYPE)
    cm = xc[:, inner + SSD_GROUPS * SSD_STATE:].astype(MXU_DTYPE)

    dt_exp = _dot01_right(_softplus(ps_ref[...] + dtb_row_ref[...]), exp_ref[...])
    a_row = -jnp.exp(alog_row_ref[...])
    dta = dt_exp * a_row
    tril, triu, same = _chunk_masks(tm, SSD_CHUNK)
    acum, aend = _dot01_left_multi((tril, same), dta)
    dta_t = _softplus(pst_ref[...] + dtb_col_ref[...]) * (-jnp.exp(alog_col_ref[...]))
    acum_t = _dot01_right(dta_t, triu)
    x_dt = xs * dt_exp
    xw = (x_dt * jnp.exp(aend - acum)).astype(MXU_DTYPE)
    x_dt_m = x_dt.astype(MXU_DTYPE)
    e_acum = jnp.exp(acum)
    cr = lax.broadcasted_iota(jnp.int32, (SSD_CHUNK, SSD_CHUNK), 0)
    cc = lax.broadcasted_iota(jnp.int32, (SSD_CHUNK, SSD_CHUNK), 1)
    causal = cr >= cc
    dskip = dskip_ref[...]
    nw = nw_ref[...]
    hg = SSD_HEADS // SSD_GROUPS

    for c in range(tm // SSD_CHUNK):
        r0 = c * SSD_CHUNK
        rows = slice(r0, r0 + SSD_CHUNK)
        for g in range(SSD_GROUPS):
            gl = slice(g * gw, (g + 1) * gw)
            nl = slice(g * SSD_STATE, (g + 1) * SSD_STATE)
            b_g, c_g = bm[rows, nl], cm[rows, nl]
            cb = _dot_nt(c_g, b_g)
            parts = []
            for hh in range(hg):
                h = g * hg + hh
                lane0 = h * SSD_HEADDIM
                colv = jnp.broadcast_to(acum[rows, lane0:lane0 + 1], (SSD_CHUNK, SSD_CHUNK))
                rowv = acum_t[h:h + 1, r0:r0 + SSD_CHUNK]
                decay = jnp.where(causal, jnp.exp(colv - rowv), 0.0)
                mmat = (cb * decay).astype(MXU_DTYPE)
                parts.append(_dot(mmat, x_dt_m[rows, lane0:lane0 + SSD_HEADDIM]))
            y_diag = jnp.concatenate(parts, axis=-1)
            state_t = st_ref[:, gl]
            y_off = _dot(c_g, state_t.astype(MXU_DTYPE)) * e_acum[rows, gl]
            st_ref[:, gl] = state_t * jnp.exp(aend[r0:r0 + 1, gl]) + _dot_tn(b_g, xw[rows, gl])
            y = y_diag + y_off + dskip[:, gl] * xs[rows, gl]
            y = y * _silu(z_ref[rows, gl].astype(F32))
            o_ref[rows, gl] = _rmsnorm(y, nw[:, gl]).astype(o_ref.dtype)


def _ssd(p, ps, pst, cw, cb, dtb_row, dtb_col, alog_row, alog_col, dskip, nw, expand, *, batch, seq, tm):
    nt = seq // tm
    full = lambda a: pl.BlockSpec(a.shape, lambda b, i: (0,) * a.ndim)
    row = lambda b, i: b * nt + i
    conv_dim = SSD_INNER + 2 * SSD_GROUPS * SSD_STATE
    return pl.pallas_call(
        functools.partial(_ssd_kernel, tm=tm),
        out_shape=jax.ShapeDtypeStruct((p.shape[0], SSD_INNER), p.dtype),
        grid=(batch, nt),
        in_specs=[
            pl.BlockSpec((tm, 512), lambda b, i: (row(b, i), P_SSD_Z // 512)),
            pl.BlockSpec((tm, 512), lambda b, i: (row(b, i), P_SSD_X // 512)),
            pl.BlockSpec((tm, 256), lambda b, i: (row(b, i), P_SSD_B // 256)),
            pl.BlockSpec((tm, 256), lambda b, i: (row(b, i), P_SSD_C // 256)),
            pl.BlockSpec((tm, LANES), lambda b, i: (row(b, i), 0)),
            pl.BlockSpec((SUBLANES, tm), lambda b, i: (PS_DT // SUBLANES, row(b, i))),
            full(cw), full(cb), full(dtb_row), full(dtb_col), full(alog_row), full(alog_col),
            full(dskip), full(nw), full(expand),
        ],
        out_specs=pl.BlockSpec((tm, SSD_INNER), lambda b, i: (row(b, i), 0)),
        scratch_shapes=[pltpu.VMEM((SSD_STATE, SSD_INNER), F32),
                        pltpu.VMEM((SUBLANES, conv_dim), F32)],
        compiler_params=_cparams(("parallel", "arbitrary"), 40 << 20),
        name="ssd",
    )(p, p, p, p, ps, pst, cw, cb, dtb_row, dtb_col, alog_row, alog_col, dskip, nw, expand)


def _merge_kernel(x_ref, g_ref, y0_ref, y1_ref, y2_ref, y3_ref, wsq_ref, wbr_ref, o_ref, h_ref, acc_ref,
                  *, tc):
    j = pl.program_id(1)
    d = x_ref.shape[-1]

    @pl.when(j == 0)
    def _():
        _norm_rows(x_ref, g_ref, h_ref)
        acc_ref[...] = jnp.zeros_like(acc_ref)

    def branch(y_ref):
        h = h_ref[...]
        y = y_ref[...]
        for c in range(d // tc):
            cols = slice(c * tc, (c + 1) * tc)
            gate = jax.nn.sigmoid(_dot(h, wsq_ref[:, cols]))
            acc_ref[:, cols] += gate * _dot(y, wbr_ref[:, cols])

    for b, y_ref in enumerate((y0_ref, y1_ref, y2_ref, y3_ref)):
        pl.when(j == b)(functools.partial(branch, y_ref))

    @pl.when(j == N_BRANCH)
    def _():
        merged = acc_ref[...].astype(MXU_DTYPE)
        for c in range(d // tc):
            cols = slice(c * tc, (c + 1) * tc)
            o_ref[:, cols] = x_ref[:, cols] + _dot(merged, wsq_ref[:, cols])


def _merge(x, g, ys, wsq, wbr, *, layer, tm, tc):
    m, d = x.shape
    assert len(ys) == N_BRANCH
    bw = ys[0].shape[-1]
    wbytes = jnp.dtype(wsq.dtype).itemsize
    vmem = 4 * tm * d * 4 + tm * d * wbytes + tm * d * 4 + 2 * d * d * wbytes \
        + 2 * bw * d * wbytes + 3 * N_BRANCH * tm * bw * wbytes + 4 * tm * tc * 4 + (4 << 20)
    last = N_BRANCH - 1
    y_spec = pl.BlockSpec((tm, bw), lambda i, j: (i, 0))
    return pl.pallas_call(
        functools.partial(_merge_kernel, tc=tc),
        out_shape=jax.ShapeDtypeStruct((m, d), F32),
        grid=(m // tm, N_BRANCH + 1),
        in_specs=[
            pl.BlockSpec((tm, d), lambda i, j: (i, 0)),
            pl.BlockSpec((1, d), lambda i, j: (0, 0)),
            y_spec, y_spec, y_spec, y_spec,
            pl.BlockSpec((None, d, d), lambda i, j: (layer, 0, j)),
            pl.BlockSpec((None, None, bw, d), lambda i, j: (layer, jnp.minimum(j, last), 0, 0)),
        ],
        out_specs=pl.BlockSpec((tm, d), lambda i, j: (i, 0)),
        scratch_shapes=[pltpu.VMEM((tm, d), wsq.dtype), pltpu.VMEM((tm, d), F32)],
        compiler_params=_cparams(("parallel", "arbitrary"), vmem),
        name="merge",
    )(x, g, *ys, wsq, wbr)


def _xattn_kernel(x_ref, g_ref, wq_ref, k_ref, v_ref, wo_ref, o_ref, h_ref, oc_ref, *, tc):
    _norm_rows(x_ref, g_ref, h_ref)
    h = h_ref[...]
    dh = XATTN_DH
    for hd in range(XATTN_HEADS):
        cols = slice(hd * dh, (hd + 1) * dh)
        q = _dot(h, wq_ref[:, cols]).astype(MXU_DTYPE)
        s = _dot_nt(q, k_ref[:, cols]) * (dh ** -0.5 * LOG2E)
        p = jnp.exp2(s - jnp.max(s, axis=-1, keepdims=True))
        p = p / jnp.sum(p, axis=-1, keepdims=True)
        oc_ref[:, cols] = _dot(p.astype(MXU_DTYPE), v_ref[:, cols]).astype(oc_ref.dtype)
    oc = oc_ref[...]
    d = x_ref.shape[-1]
    for c in range(d // tc):
        cols = slice(c * tc, (c + 1) * tc)
        o_ref[:, cols] = x_ref[:, cols] + _dot(oc, wo_ref[:, cols])


def _xattn(x, g, wq, kv, wo, *, layer, seq, n_mem, tm, tc):
    m, d = x.shape
    wbytes = jnp.dtype(wq.dtype).itemsize
    vmem = 4 * tm * d * 4 + 2 * tm * d * wbytes + 2 * d * d * wbytes + 4 * n_mem * d * wbytes \
        + 3 * tm * XATTN_DH * 4 + 3 * tm * n_mem * 4 + 2 * tm * tc * 4 + (6 << 20)
    tiles_per_seq = seq // tm
    return pl.pallas_call(
        functools.partial(_xattn_kernel, tc=tc),
        out_shape=jax.ShapeDtypeStruct((m, d), F32),
        grid=(m // tm,),
        in_specs=[
            pl.BlockSpec((tm, d), lambda i: (i, 0)),
            pl.BlockSpec((1, d), lambda i: (0, 0)),
            pl.BlockSpec((None, d, d), lambda i: (layer, 0, 0), pipeline_mode=pl.Buffered(1)),
            pl.BlockSpec((n_mem, d), lambda i: (i // tiles_per_seq, 0)),
            pl.BlockSpec((n_mem, d), lambda i: (i // tiles_per_seq, 1)),
            pl.BlockSpec((None, d, d), lambda i: (layer, 0, 0), pipeline_mode=pl.Buffered(1)),
        ],
        out_specs=pl.BlockSpec((tm, d), lambda i: (i, 0)),
        scratch_shapes=[pltpu.VMEM((tm, d), wq.dtype), pltpu.VMEM((tm, d), wq.dtype)],
        compiler_params=_cparams(("parallel",), vmem),
        name="xattn",
    )(x, g, wq, kv, kv, wo)


_IN_SIZES = (N_BRANCH * D_MODEL, MLA_Q_RANK, MLA_KV_RANK, MLA_ROPE, 3 * FOX_HEADS * FOX_DH, FOX_HEADS,
             GLA_HEADS * GLA_DK, GLA_HEADS * GLA_DK, GLA_HEADS * GLA_DV, GLA_GATE_RANK,
             GLA_HEADS * GLA_DV, SSD_INNER, SSD_INNER + 2 * SSD_GROUPS * SSD_STATE, SSD_HEADS)


def _rot_cols(w):
    half = w.shape[-1] // 2
    return jnp.concatenate([-w[..., half:], w[..., :half]], axis=-1)


def _prep_w_in(w_mix):
    w_in = w_mix.astype(MXU_DTYPE)
    nl, d = w_in.shape[:2]
    offs = np.cumsum((0,) + _IN_SIZES[1:])
    assert offs[-1] == w_in.shape[2]
    seg = lambda i: w_in[:, :, offs[i]:offs[i + 1]]
    (cq, ckv, kr, fox, fox_f, gq, gk, gv, glr, gr, sz, sxbc, sdt) = [seg(i) for i in range(len(_IN_SIZES) - 1)]
    z = lambda n: jnp.zeros((nl, d, n), w_in.dtype)
    w_p = jnp.concatenate([
        cq, z(512 - MLA_Q_RANK), ckv, kr, z(LANES - MLA_ROPE), _rot_cols(kr), z(LANES - MLA_ROPE),
        fox, gq, gk, gv, gr, sz, sxbc], axis=2)
    assert w_p.shape[2] == P_WIDTH
    cs = np.ones((1, P_WIDTH), np.float32)
    cs[:, P_FOX_Q:P_FOX_K] = FOX_DH ** -0.5 * LOG2E
    cs[:, P_GLA_Q:P_GLA_K] = GLA_DK ** -0.5
    w_s = jnp.concatenate([glr, sdt, fox_f, z(LANES - GLA_GATE_RANK - SSD_HEADS - FOX_HEADS)], axis=2)
    return w_p.astype(MXU_DTYPE), jnp.asarray(cs), w_s.astype(MXU_DTYPE)


def _prep_mla(w_uq, w_ukv):
    nl, r = w_uq.shape[:2]
    wq = w_uq.reshape(nl, r, MLA_HEADS, MLA_NOPE + MLA_ROPE)
    nope, pe = wq[..., :MLA_NOPE], wq[..., MLA_NOPE:]
    zpad = jnp.zeros((nl, r, MLA_HEADS, MLA_DK_PAD - MLA_NOPE - MLA_ROPE), w_uq.dtype)
    wa = jnp.concatenate([nope, pe, zpad], axis=-1).reshape(nl, r, MLA_HEADS * MLA_DK_PAD)
    wb = jnp.concatenate([_rot_cols(pe), zpad], axis=-1).reshape(nl, r, MLA_HEADS * LANES)
    rk = w_ukv.shape[1]
    wkv = w_ukv.reshape(nl, rk, MLA_HEADS, MLA_NOPE + MLA_V)
    wkv = jnp.concatenate([wkv[..., :MLA_NOPE].reshape(nl, rk, -1), wkv[..., MLA_NOPE:].reshape(nl, rk, -1)],
                          axis=2)
    return wa.astype(MXU_DTYPE), wb.astype(MXU_DTYPE), wkv.astype(MXU_DTYPE)


def _tiles(m, seq):
    mid = min(512, seq)
    return dict(ffn_tm=mid, ffn_tf=512, ffn_tc=512, proj_tm=mid, proj_tn=2816, prep_tm=min(1024, seq),
                flash_tb=mid, flash_rmult=min(2, seq // mid), gla_tm=min(512, seq), ssd_tm=min(256, seq), merge_tm=mid, merge_tc=512,
                xattn_tm=mid, xattn_tc=512, rope_tm=min(1024, m))


@jax.jit
def kernel(x, mem, positions, ffn1_norm, ffn1_w_gate, ffn1_w_up, ffn1_w_down, mix_norm, w_in, mla_q_norm, mla_kv_norm, mla_w_uq, mla_w_ukv, fox_f_bias, gla_w_g2, gla_b_g2, gla_norm, ssd_conv_w, ssd_conv_b, ssd_dt_bias, ssd_a_log, ssd_d, ssd_norm, w_branch, w_out, xattn_norm, mem_norm, xattn_w_q, xattn_w_k, xattn_w_v, xattn_w_o, ffn2_norm, ffn2_w_gate, ffn2_w_up, ffn2_w_down, final_norm):
    batch, seq, d = x.shape
    n_mem = mem.shape[1]
    m = batch * seq
    t = _tiles(m, seq)
    depth = w_in.shape[0]
    cast = lambda a: a.astype(MXU_DTYPE)
    row = lambda a: a.reshape(1, -1).astype(F32)
    col8 = lambda v: jnp.zeros((SUBLANES, 1), F32).at[:v.shape[0], 0].set(v.astype(F32))

    xs = x.reshape(m, d)
    mem2 = mem.reshape(batch * n_mem, d)
    ct, st = _rope_tables(positions.reshape(m, 1), tm=t["rope_tm"])

    expand = np.zeros((LANES, SSD_INNER), np.float32)
    for h in range(SSD_HEADS):
        expand[PS_DT + h, h * SSD_HEADDIM:(h + 1) * SSD_HEADDIM] = 1.0
    expand = jnp.asarray(expand, MXU_DTYPE)
    rep = lambda v: jnp.repeat(v.astype(F32), SSD_HEADDIM).reshape(1, SSD_INNER)
    ffn = functools.partial(_ffn, tm=t["ffn_tm"], tf=t["ffn_tf"], tc=t["ffn_tc"])
    flash = functools.partial(_flash, batch=batch, seq=seq, groups=2, tb=t["flash_tb"], rmult=t["flash_rmult"])

    f1g, f1u, f1d = cast(ffn1_w_gate), cast(ffn1_w_up), cast(ffn1_w_down)
    f2g, f2u, f2d = cast(ffn2_w_gate), cast(ffn2_w_up), cast(ffn2_w_down)
    w_p, cs, w_s = _prep_w_in(w_in[:, :, N_BRANCH * d:])
    wa, wb, wkv = _prep_mla(mla_w_uq, mla_w_ukv)
    wsq = jnp.concatenate([cast(w_in[:, :, :N_BRANCH * d]), cast(w_out)], axis=2)
    wbr = cast(w_branch)
    w_kv = cast(jnp.concatenate([xattn_w_k, xattn_w_v], axis=2))
    wq_x, wo_x = cast(xattn_w_q), cast(xattn_w_o)

    for l in range(depth):
        xs = ffn(xs, row(ffn1_norm[l]), f1g, f1u, f1d, row(final_norm), layer=l, final_norm=False)

        p, ps, pst = _proj(xs, row(mix_norm[l]), w_p, cs, w_s, layer=l, tm=t["proj_tm"], tn=t["proj_tn"])

        q_mla, k_mla, v_mla = _mla_prep(p, ct, st, row(mla_q_norm[l]), row(mla_kv_norm[l]), wa, wb, wkv,
                                        layer=l, tm=t["prep_tm"])
        y_mla = flash(q_mla, k_mla, v_mla, None, heads=MLA_HEADS,
                      dk=MLA_DK_PAD, dv=MLA_V, q_off=0, k_off=0, v_off=0)

        c_fox = _fox_cum(pst, col8(fox_f_bias[l]), batch=batch, seq=seq)
        y_fox = flash(p, p, p, c_fox, heads=FOX_HEADS,
                      dk=FOX_DH, dv=FOX_DH, q_off=P_FOX_Q, k_off=P_FOX_K, v_off=P_FOX_V)

        wg2 = jnp.zeros((LANES, GLA_HEADS * GLA_DK), F32).at[PS_GLR:PS_GLR + GLA_GATE_RANK].set(gla_w_g2[l])
        y_gla = _gla(p, ps, cast(wg2), row(gla_b_g2[l]), row(gla_norm[l]), batch=batch, seq=seq, tm=t["gla_tm"])

        dtb_row = jnp.zeros((1, LANES), F32).at[0, PS_DT:PS_DT + SSD_HEADS].set(ssd_dt_bias[l])
        y_ssd = _ssd(p, ps, pst, ssd_conv_w[l].astype(F32), row(ssd_conv_b[l]), dtb_row, col8(ssd_dt_bias[l]),
                     rep(ssd_a_log[l]), col8(ssd_a_log[l]), rep(ssd_d[l]), row(ssd_norm[l]), expand,
                     batch=batch, seq=seq, tm=t["ssd_tm"])

        xs = _merge(xs, row(mix_norm[l]), (y_mla, y_fox, y_gla, y_ssd), wsq, wbr, layer=l, tm=t["merge_tm"], tc=t["merge_tc"])

        kv = _norm_matmul(mem2, row(mem_norm[l]), w_kv, layer=l, tm=min(512, batch * n_mem), tn=1024)
        xs = _xattn(xs, row(xattn_norm[l]), wq_x, kv, wo_x, layer=l,
                    seq=seq, n_mem=n_mem, tm=t["xattn_tm"], tc=t["xattn_tc"])

        xs = ffn(xs, row(ffn2_norm[l]), f2g, f2u, f2d, row(final_norm), layer=l, final_norm=(l == depth - 1))

    return xs.reshape(batch, seq, d)
```

```python
import functools

import numpy as np
import jax
import jax.numpy as jnp
from jax import lax
from jax.experimental import pallas as pl
from jax.experimental.pallas import tpu as pltpu

D_MODEL = 2048
DEPTH = 4
EPS = 1e-6
N_BRANCH = 4
BRANCH_WIDTH = 512
MLA_HEADS, MLA_NOPE, MLA_ROPE, MLA_V = 4, 128, 64, 128
MLA_Q_RANK, MLA_KV_RANK = 384, 256
MLA_DK_PAD = 256
ROPE_THETA = 10000.0
FOX_HEADS, FOX_DH = 4, 128
GLA_HEADS, GLA_DK, GLA_DV = 4, 64, 128
GLA_GATE_RANK, GLA_GATE_NORM, GLA_CHUNK = 16, 16.0, 64
SSD_HEADS, SSD_HEADDIM, SSD_GROUPS, SSD_STATE, SSD_CONV, SSD_CHUNK = 8, 64, 2, 128, 4, 128
SSD_INNER = SSD_HEADS * SSD_HEADDIM
XATTN_HEADS = 4
XATTN_DH = D_MODEL // XATTN_HEADS
D_FF = 5632

LANES = 128
SUBLANES = 8
V7X_VMEM_LIMIT_BYTES = 60 * 1024 * 1024

MXU_DTYPE = jnp.bfloat16
FFN_SLOTS = 3
F32 = jnp.float32
LOG2E = 1.4426950408889634

P_CQ, P_CKV, P_KR, P_KRROT = 0, 512, 768, 896
P_FOX_Q, P_FOX_K, P_FOX_V = 1024, 1536, 2048
P_GLA_Q, P_GLA_K, P_GLA_V, P_GLA_R = 2560, 2816, 3072, 3584
P_SSD_Z, P_SSD_X, P_SSD_B, P_SSD_C = 4096, 4608, 5120, 5376
P_WIDTH = 5632
PS_GLR, PS_DT, PS_F = 0, 16, 24
PST_ROWS = 32


def _cparams(sem, vmem_bytes):
    return pltpu.CompilerParams(dimension_semantics=sem,
                                vmem_limit_bytes=int(min(vmem_bytes, V7X_VMEM_LIMIT_BYTES)))


def _dot(a, b):
    return jnp.dot(a, b, preferred_element_type=F32)


def _dot_nt(a, b):
    return lax.dot_general(a, b, (((1,), (1,)), ((), ())), preferred_element_type=F32)


def _dot_tn(a, b):
    return lax.dot_general(a, b, (((0,), (0,)), ((), ())), preferred_element_type=F32)


def _split3(x):
    hi = x.astype(MXU_DTYPE)
    r1 = x - hi.astype(F32)
    mid = r1.astype(MXU_DTYPE)
    lo = (r1 - mid.astype(F32)).astype(MXU_DTYPE)
    return hi, mid, lo


def _dot01_right(x, m01):
    if MXU_DTYPE == F32:
        return _dot(x, m01.astype(F32))
    m = m01.astype(MXU_DTYPE)
    hi, mid, lo = _split3(x)
    return _dot(hi, m) + _dot(mid, m) + _dot(lo, m)


def _dot01_left_multi(m01s, x):
    if MXU_DTYPE == F32:
        return [_dot(m01.astype(F32), x) for m01 in m01s]
    parts = _split3(x)
    out = []
    for m01 in m01s:
        m = m01.astype(MXU_DTYPE)
        out.append(_dot(m, parts[0]) + _dot(m, parts[1]) + _dot(m, parts[2]))
    return out


def _dot01_left(m01, x):
    return _dot01_left_multi((m01,), x)[0]


def _rmsnorm(x, g):
    ms = jnp.mean(x * x, axis=-1, keepdims=True)
    return x * lax.rsqrt(ms + EPS) * g


def _log_sigmoid(x):
    return jnp.minimum(x, 0.0) - jnp.log1p(jnp.exp(-jnp.abs(x)))


def _softplus(x):
    return jnp.maximum(x, 0.0) + jnp.log1p(jnp.exp(-jnp.abs(x)))


def _silu(x):
    return x * jax.nn.sigmoid(x)


def _chunk_masks(n, chunk):
    shift = chunk.bit_length() - 1
    assert 1 << shift == chunk
    r = lax.broadcasted_iota(jnp.int32, (n, n), 0)
    c = lax.broadcasted_iota(jnp.int32, (n, n), 1)
    same = jnp.right_shift(r, shift) == jnp.right_shift(c, shift)
    return jnp.logical_and(same, r >= c), jnp.logical_and(same, r <= c), same


def _norm_rows(x_ref, g_ref, h_ref, copy_ref=None, *, rows=256):
    tm = x_ref.shape[0]
    rows = min(rows, tm)
    g = g_ref[...]
    for r in range(tm // rows):
        sl = slice(r * rows, (r + 1) * rows)
        x = x_ref[sl, :]
        h_ref[sl, :] = _rmsnorm(x, g).astype(h_ref.dtype)
        if copy_ref is not None:
            copy_ref[sl, :] = x


def _ffn_kernel(x_ref, g_ref, wg_hbm, wu_hbm, wd_hbm, fg_ref, o_ref, h_ref, wg_buf, wu_buf, wd_buf, sem,
                *, final_norm, layer, tf, n_chunks):
    i = pl.program_id(0)

    def chunk_copies(j, slot):
        cols = pl.ds(j * tf, tf)
        return (pltpu.make_async_copy(wg_hbm.at[layer, :, cols], wg_buf.at[slot], sem.at[0, slot]),
                pltpu.make_async_copy(wu_hbm.at[layer, :, cols], wu_buf.at[slot], sem.at[1, slot]),
                pltpu.make_async_copy(wd_hbm.at[layer, cols, :], wd_buf.at[slot], sem.at[2, slot]))

    def start(j, slot):
        for cp in chunk_copies(j, slot):
            cp.start()

    @pl.when(i == 0)
    def _():
        start(0, 0)

    _norm_rows(x_ref, g_ref, h_ref, o_ref)
    h = h_ref[...]
    for j in range(n_chunks):
        slot = j % FFN_SLOTS
        for cp in chunk_copies(j, slot):
            cp.wait()
        if j + 1 < n_chunks:
            start(j + 1, (j + 1) % FFN_SLOTS)
        else:
            @pl.when(i + 1 < pl.num_programs(0))
            def _():
                start(0, 0)
        a = _dot(h, wg_buf[slot])
        u = _dot(h, wu_buf[slot])
        t = (_silu(a) * (0.5 * u)).astype(MXU_DTYPE)
        o_ref[...] += _dot(t, wd_buf[slot])

    if final_norm:
        _norm_rows(o_ref, fg_ref, o_ref)


def _ffn(x, g, wg, wu, wd, fg, *, layer, final_norm, tm, tf):
    m, d = x.shape
    f = wg.shape[-1]
    n_chunks = f // tf
    assert n_chunks * tf == f and n_chunks > FFN_SLOTS and (n_chunks - 1) % FFN_SLOTS != 0
    wb = jnp.dtype(wg.dtype).itemsize
    vmem = 4 * tm * d * 4 + tm * d * wb + FFN_SLOTS * 3 * d * tf * wb + 4 * tm * tf * 4 + tm * d * 4 + (4 << 20)
    return pl.pallas_call(
        functools.partial(_ffn_kernel, final_norm=final_norm, layer=layer, tf=tf, n_chunks=n_chunks),
        out_shape=jax.ShapeDtypeStruct((m, d), F32),
        grid=(m // tm,),
        in_specs=[
            pl.BlockSpec((tm, d), lambda i: (i, 0)),
            pl.BlockSpec((1, d), lambda i: (0, 0)),
            pl.BlockSpec(memory_space=pl.ANY),
            pl.BlockSpec(memory_space=pl.ANY),
            pl.BlockSpec(memory_space=pl.ANY),
            pl.BlockSpec((1, d), lambda i: (0, 0)),
        ],
        out_specs=pl.BlockSpec((tm, d), lambda i: (i, 0)),
        scratch_shapes=[pltpu.VMEM((tm, d), wg.dtype),
                        pltpu.VMEM((FFN_SLOTS, d, tf), wg.dtype),
                        pltpu.VMEM((FFN_SLOTS, d, tf), wg.dtype),
                        pltpu.VMEM((FFN_SLOTS, tf, d), wg.dtype),
                        pltpu.SemaphoreType.DMA((3, FFN_SLOTS))],
        compiler_params=_cparams(("arbitrary",), vmem),
        name="ffn",
    )(x, g, wg, wu, wd, fg)


def _proj_kernel(x_ref, g_ref, w_ref, cs_ref, ws_ref, p_ref, ps_ref, pst_ref, h_ref):
    j = pl.program_id(1)

    @pl.when(j == 0)
    def _():
        _norm_rows(x_ref, g_ref, h_ref)
        ps = _dot(h_ref[...], ws_ref[...])
        ps_ref[...] = ps
        pst_ref[...] = jnp.transpose(ps)[:PST_ROWS, :]

    p_ref[...] = (_dot(h_ref[...], w_ref[...]) * cs_ref[...]).astype(p_ref.dtype)


def _proj(x, g, w, cs, ws, *, layer, tm, tn):
    m, d = x.shape
    n = w.shape[-1]
    wb = jnp.dtype(w.dtype).itemsize
    vmem = 2 * tm * d * 4 + tm * d * wb + 2 * d * tn * wb + 2 * tm * tn * wb + 2 * tm * tn * 4 + (6 << 20)
    return pl.pallas_call(
        _proj_kernel,
        out_shape=(jax.ShapeDtypeStruct((m, n), w.dtype),
                   jax.ShapeDtypeStruct((m, LANES), F32),
                   jax.ShapeDtypeStruct((PST_ROWS, m), F32)),
        grid=(m // tm, n // tn),
        in_specs=[
            pl.BlockSpec((tm, d), lambda i, j: (i, 0)),
            pl.BlockSpec((1, d), lambda i, j: (0, 0)),
            pl.BlockSpec((None, d, tn), lambda i, j: (layer, 0, j)),
            pl.BlockSpec((1, tn), lambda i, j: (0, j)),
            pl.BlockSpec((None, d, LANES), lambda i, j: (layer, 0, 0)),
        ],
        out_specs=(pl.BlockSpec((tm, tn), lambda i, j: (i, j)),
                   pl.BlockSpec((tm, LANES), lambda i, j: (i, 0)),
                   pl.BlockSpec((PST_ROWS, tm), lambda i, j: (0, i))),
        scratch_shapes=[pltpu.VMEM((tm, d), w.dtype)],
        compiler_params=_cparams(("parallel", "arbitrary"), vmem),
        name="proj",
    )(x, g, w, cs, ws)


def _norm_matmul_kernel(x_ref, g_ref, w_ref, o_ref):
    h = _rmsnorm(x_ref[...], g_ref[...]).astype(w_ref.dtype)
    o_ref[...] = _dot(h, w_ref[...]).astype(o_ref.dtype)


def _norm_matmul(x, g, w, *, layer, tm, tn):
    m, d = x.shape
    n = w.shape[-1]
    wb = jnp.dtype(w.dtype).itemsize
    vmem = 3 * tm * d * 4 + 2 * d * tn * wb + 3 * tm * tn * 4 + (4 << 20)
    return pl.pallas_call(
        _norm_matmul_kernel,
        out_shape=jax.ShapeDtypeStruct((m, n), w.dtype),
        grid=(m // tm, n // tn),
        in_specs=[pl.BlockSpec((tm, d), lambda i, j: (i, 0)),
                  pl.BlockSpec((1, d), lambda i, j: (0, 0)),
                  pl.BlockSpec((None, d, tn), lambda i, j: (layer, 0, j))],
        out_specs=pl.BlockSpec((tm, tn), lambda i, j: (i, j)),
        compiler_params=_cparams(("parallel", "parallel"), vmem),
        name="mem_kv",
    )(x, g, w)


def _rope_kernel(pos_ref, ct_ref, st_ref):
    pos = pos_ref[...].astype(F32)
    lane = lax.broadcasted_iota(jnp.int32, (1, LANES), 1)
    half = MLA_ROPE // 2
    f = jnp.bitwise_and(lane, half - 1).astype(F32)
    inv_freq = jnp.power(jnp.float32(ROPE_THETA), -f / half)
    ang = pos * inv_freq
    valid = lane < MLA_ROPE
    ct_ref[...] = jnp.where(valid, jnp.cos(ang), 0.0)
    st_ref[...] = jnp.where(valid, jnp.sin(ang), 0.0)


def _rope_tables(pos_col, *, tm):
    m = pos_col.shape[0]
    return pl.pallas_call(
        _rope_kernel,
        out_shape=(jax.ShapeDtypeStruct((m, LANES), F32), jax.ShapeDtypeStruct((m, LANES), F32)),
        grid=(m // tm,),
        in_specs=[pl.BlockSpec((tm, 1), lambda i: (i, 0))],
        out_specs=(pl.BlockSpec((tm, LANES), lambda i: (i, 0)),
                   pl.BlockSpec((tm, LANES), lambda i: (i, 0))),
        compiler_params=_cparams(("parallel",), 16 << 20),
        name="rope_tables",
    )(pos_col)


def _mla_prep_kernel(cq_ref, ckv_ref, kr_ref, krrot_ref, ct_ref, st_ref, qn_ref, kvn_ref,
                     wa_ref, wb_ref, wkv_ref, q_ref, k_ref, v_ref):
    ct = ct_ref[...]
    st = st_ref[...]
    scale = (MLA_NOPE + MLA_ROPE) ** -0.5 * LOG2E

    ckv = ckv_ref[...].astype(F32)
    ckvn = _rmsnorm(ckv, kvn_ref[...]).astype(MXU_DTYPE)
    kv = _dot(ckvn, wkv_ref[...])
    kpe = kr_ref[...].astype(F32) * ct + krrot_ref[...].astype(F32) * st
    kpe = kpe.astype(k_ref.dtype)
    hw = MLA_HEADS * MLA_NOPE
    for h in range(MLA_HEADS):
        base = h * MLA_DK_PAD
        k_ref[:, base:base + MLA_NOPE] = kv[:, h * MLA_NOPE:(h + 1) * MLA_NOPE].astype(k_ref.dtype)
        k_ref[:, base + MLA_NOPE:base + MLA_DK_PAD] = kpe
    v_ref[...] = kv[:, hw:].astype(v_ref.dtype)

    cq = cq_ref[...].astype(F32)[:, :MLA_Q_RANK]
    cqn = _rmsnorm(cq, qn_ref[...]).astype(MXU_DTYPE)
    qa = _dot(cqn, wa_ref[...])
    qb = _dot(cqn, wb_ref[...])
    for h in range(MLA_HEADS):
        base = h * MLA_DK_PAD
        q_ref[:, base:base + MLA_NOPE] = (qa[:, base:base + MLA_NOPE] * scale).astype(q_ref.dtype)
        pe = qa[:, base + MLA_NOPE:base + MLA_DK_PAD] * ct + qb[:, h * LANES:(h + 1) * LANES] * st
        q_ref[:, base + MLA_NOPE:base + MLA_DK_PAD] = (pe * scale).astype(q_ref.dtype)


def _mla_prep(p, ct, st, qn, kvn, wa, wb, wkv, *, layer, tm):
    m = p.shape[0]
    qk_w = MLA_HEADS * MLA_DK_PAD
    v_w = MLA_HEADS * MLA_V
    full = lambda a: pl.BlockSpec(a.shape, lambda i: (0,) * a.ndim)
    of_layer = lambda a: pl.BlockSpec((None,) + a.shape[1:], lambda i: (layer,) + (0,) * (a.ndim - 1))
    return pl.pallas_call(
        _mla_prep_kernel,
        out_shape=(jax.ShapeDtypeStruct((m, qk_w), p.dtype),
                   jax.ShapeDtypeStruct((m, qk_w), p.dtype),
                   jax.ShapeDtypeStruct((m, v_w), p.dtype)),
        grid=(m // tm,),
        in_specs=[
            pl.BlockSpec((tm, 512), lambda i: (i, P_CQ // 512)),
            pl.BlockSpec((tm, 256), lambda i: (i, P_CKV // 256)),
            pl.BlockSpec((tm, 128), lambda i: (i, P_KR // 128)),
            pl.BlockSpec((tm, 128), lambda i: (i, P_KRROT // 128)),
            pl.BlockSpec((tm, LANES), lambda i: (i, 0)),
            pl.BlockSpec((tm, LANES), lambda i: (i, 0)),
            full(qn), full(kvn), of_layer(wa), of_layer(wb), of_layer(wkv),
        ],
        out_specs=(pl.BlockSpec((tm, qk_w), lambda i: (i, 0)),
                   pl.BlockSpec((tm, qk_w), lambda i: (i, 0)),
                   pl.BlockSpec((tm, v_w), lambda i: (i, 0))),
        compiler_params=_cparams(("parallel",), 40 << 20),
        name="mla_prep",
    )(p, p, p, p, ct, st, qn, kvn, wa, wb, wkv)


def _fox_cum_kernel(ft_ref, fb_ref, c_ref, *, seq):
    lf = _log_sigmoid(ft_ref[...] + fb_ref[...])
    r = lax.broadcasted_iota(jnp.int32, (LANES, LANES), 0)
    c = lax.broadcasted_iota(jnp.int32, (LANES, LANES), 1)
    upper = r <= c
    carry = jnp.zeros((SUBLANES, 1), F32)
    for blk in range(seq // LANES):
        sl = slice(blk * LANES, (blk + 1) * LANES)
        cs = _dot01_right(lf[:, sl], upper) + carry
        for h in range(FOX_HEADS):
            c_ref[h, :, sl] = cs[h:h + 1, :] * LOG2E
        carry = cs[:, LANES - 1:LANES]


def _fox_cum(pst, fb_col, *, batch, seq):
    return pl.pallas_call(
        functools.partial(_fox_cum_kernel, seq=seq),
        out_shape=jax.ShapeDtypeStruct((batch * FOX_HEADS, 1, seq), F32),
        grid=(batch,),
        in_specs=[pl.BlockSpec((SUBLANES, seq), lambda b: (PS_F // SUBLANES, b)),
                  pl.BlockSpec((SUBLANES, 1), lambda b: (0, 0))],
        out_specs=pl.BlockSpec((FOX_HEADS, 1, seq), lambda b: (b, 0, 0)),
        compiler_params=_cparams(("parallel",), 16 << 20),
        name="fox_cum",
    )(pst, fb_col)


def _flash_kernel(*refs, seq, tb, rmult, groups, dk, dv, has_bias):
    refs = list(refs)
    q_ref, k_ref, v_ref = refs[:3]
    c_ref = refs[3] if has_bias else None
    o_ref = refs[3 + int(has_bias)]
    rows_st = tb * rmult

    def update(state, s, v_blk):
        m, l, acc = state
        m_new = jnp.maximum(m, jnp.max(s, axis=-1, keepdims=True))
        alpha = jnp.exp2(m - m_new)
        p = jnp.exp2(s - m_new)
        l = alpha * l + jnp.sum(p, axis=-1, keepdims=True)
        acc = alpha * acc + _dot(p.astype(MXU_DTYPE), v_blk)
        return m_new, l, acc

    def super_tile(pj, unused):
        r0 = pl.multiple_of(pj * rows_st, rows_st)
        qs = [q_ref[pl.ds(r0, rows_st), g * dk:(g + 1) * dk] for g in range(groups)]
        if has_bias:
            c0 = [jnp.max(c_ref[g, :, pl.ds(r0, tb)], axis=-1, keepdims=True) for g in range(groups)]

        def scores(q, g, start, width=tb):
            s = _dot_nt(q, k_ref[pl.ds(start, width), g * dk:(g + 1) * dk])
            if has_bias:
                s = s + (c0[g] - c_ref[g, :, pl.ds(start, width)])
            return s

        def v_block(g, start, width=tb):
            return v_ref[pl.ds(start, width), g * dv:(g + 1) * dv]

        def body(ki, carry):
            start = pl.multiple_of(ki * rows_st, rows_st)
            return tuple(update(carry[g], scores(qs[g], g, start, rows_st), v_block(g, start, rows_st))
                         for g in range(groups))

        init = tuple((jnp.full((rows_st, 1), -jnp.inf, F32), jnp.zeros((rows_st, 1), F32),
                      jnp.zeros((rows_st, dv), F32)) for _ in range(groups))
        carry = lax.fori_loop(0, pj, body, init)

        for g in range(groups):
            m, l, acc = carry[g]
            for a in range(rmult):
                rs = slice(a * tb, (a + 1) * tb)
                width = (a + 1) * tb
                row = lax.broadcasted_iota(jnp.int32, (tb, width), 0)
                col = lax.broadcasted_iota(jnp.int32, (tb, width), 1)
                s = jnp.where(col <= row + a * tb, scores(qs[g][rs], g, r0, width), -jnp.inf)
                state = update((m[rs], l[rs], acc[rs]), s, v_block(g, r0, width))
                o_ref[pl.ds(pl.multiple_of(r0 + a * tb, tb), tb), g * dv:(g + 1) * dv] = \
                    (state[2] / state[1]).astype(o_ref.dtype)
        return unused

    lax.fori_loop(0, seq // rows_st, super_tile, 0)


def _flash(q_arr, k_arr, v_arr, c_arr, *, batch, seq, heads, groups, dk, dv,
           q_off, k_off, v_off, tb, rmult):
    rows_st = tb * rmult
    has_bias = c_arr is not None
    gk, gv = groups * dk, groups * dv
    in_specs = [pl.BlockSpec((seq, gk), lambda b, h: (b, q_off // gk + h)),
                pl.BlockSpec((seq, gk), lambda b, h: (b, k_off // gk + h)),
                pl.BlockSpec((seq, gv), lambda b, h: (b, v_off // gv + h))]
    args = [q_arr, k_arr, v_arr]
    if has_bias:
        in_specs.append(pl.BlockSpec((groups, 1, seq), lambda b, h: (b * (heads // groups) + h, 0, 0)))
        args.append(c_arr)
    vmem = 2 * seq * (2 * gk + 2 * gv) * 2 + groups * (3 * rows_st * rows_st * 4 + 4 * rows_st * dv * 4) + (8 << 20)
    return pl.pallas_call(
        functools.partial(_flash_kernel, seq=seq, tb=tb, rmult=rmult, groups=groups, dk=dk, dv=dv,
                          has_bias=has_bias),
        out_shape=jax.ShapeDtypeStruct((batch * seq, heads * dv), q_arr.dtype),
        grid=(batch, heads // groups),
        in_specs=in_specs,
        out_specs=pl.BlockSpec((seq, gv), lambda b, h: (b, h)),
        compiler_params=_cparams(("parallel", "parallel"), vmem),
        name="flash_fox" if has_bias else "flash_mla",
    )(*args)


def _gla_kernel(q_ref, k_ref, v_ref, r_ref, ps_ref, wg2_ref, bg2_ref, nw_ref, o_ref, st_ref, *, tm):
    @pl.when(pl.program_id(1) == 0)
    def _():
        st_ref[...] = jnp.zeros_like(st_ref)

    nc = tm // GLA_CHUNK
    glin = _dot(ps_ref[...].astype(MXU_DTYPE), wg2_ref[...]) + bg2_ref[...]
    gl = _log_sigmoid(glin) / GLA_GATE_NORM
    tril, _, same = _chunk_masks(tm, GLA_CHUNK)
    gcum, gend = _dot01_left_multi((tril, same), gl)
    q = q_ref[...].astype(F32)
    k = k_ref[...].astype(F32)
    qd = (q * jnp.exp(gcum)).astype(MXU_DTYPE)
    kd = (k * jnp.exp(-gcum)).astype(MXU_DTYPE)
    ku = (k * jnp.exp(gend - gcum)).astype(MXU_DTYPE)
    dec = jnp.exp(gend)
    nw = nw_ref[...]
    er = lax.broadcasted_iota(jnp.int32, (tm, nc * GLA_DK), 0)
    ec = lax.broadcasted_iota(jnp.int32, (tm, nc * GLA_DK), 1)
    own = jnp.right_shift(er, GLA_CHUNK.bit_length() - 1) == jnp.right_shift(ec, GLA_DK.bit_length() - 1)
    zero = jnp.zeros((), MXU_DTYPE)

    for h in range(GLA_HEADS):
        kc = slice(h * GLA_DK, (h + 1) * GLA_DK)
        vc = slice(h * GLA_DV, (h + 1) * GLA_DV)
        qd_h, kd_h, ku_h = qd[:, kc], kd[:, kc], ku[:, kc]
        v_h = v_ref[:, vc]
        att = jnp.where(tril, _dot_nt(qd_h, kd_h), 0.0)
        o = _dot(att.astype(MXU_DTYPE), v_h)
        qd_x = jnp.where(own, jnp.concatenate([qd_h] * nc, axis=1), zero)
        ku_x = jnp.where(own, jnp.concatenate([ku_h] * nc, axis=1), zero)
        upd = _dot_tn(v_h, ku_x)
        state = st_ref[:, kc]
        entering = []
        for c in range(nc):
            entering.append(state)
            state = state * dec[c * GLA_CHUNK:c * GLA_CHUNK + 1, kc] + upd[:, c * GLA_DK:(c + 1) * GLA_DK]
        st_ref[:, kc] = state
        s_all = jnp.concatenate(entering, axis=1).astype(MXU_DTYPE)
        o = o + _dot_nt(qd_x, s_all)
        o = _rmsnorm(o, nw)
        o_ref[:, vc] = (o * _silu(r_ref[:, vc].astype(F32))).astype(o_ref.dtype)


def _gla(p, ps, wg2, bg2, nw, *, batch, seq, tm):
    nt = seq // tm
    full = lambda a: pl.BlockSpec(a.shape, lambda b, i: (0,) * a.ndim)
    row = lambda b, i: b * nt + i
    return pl.pallas_call(
        functools.partial(_gla_kernel, tm=tm),
        out_shape=jax.ShapeDtypeStruct((p.shape[0], GLA_HEADS * GLA_DV), p.dtype),
        grid=(batch, nt),
        in_specs=[
            pl.BlockSpec((tm, 256), lambda b, i: (row(b, i), P_GLA_Q // 256)),
            pl.BlockSpec((tm, 256), lambda b, i: (row(b, i), P_GLA_K // 256)),
            pl.BlockSpec((tm, 512), lambda b, i: (row(b, i), P_GLA_V // 512)),
            pl.BlockSpec((tm, 512), lambda b, i: (row(b, i), P_GLA_R // 512)),
            pl.BlockSpec((tm, LANES), lambda b, i: (row(b, i), 0)),
            full(wg2), full(bg2), full(nw),
        ],
        out_specs=pl.BlockSpec((tm, GLA_HEADS * GLA_DV), lambda b, i: (row(b, i), 0)),
        scratch_shapes=[pltpu.VMEM((GLA_DV, GLA_HEADS * GLA_DK), F32)],
        compiler_params=_cparams(("parallel", "arbitrary"), 32 << 20),
        name="gla",
    )(p, p, p, p, ps, wg2, bg2, nw)


def _ssd_kernel(z_ref, x_ref, b_ref, c_ref, ps_ref, pst_ref, cw_ref, cb_ref, dtb_row_ref,
                dtb_col_ref, alog_row_ref, alog_col_ref, dskip_ref, nw_ref, exp_ref,
                o_ref, st_ref, tail_ref, *, tm):
    @pl.when(pl.program_id(1) == 0)
    def _():
        st_ref[...] = jnp.zeros_like(st_ref)
        tail_ref[...] = jnp.zeros_like(tail_ref)

    inner = SSD_INNER
    gw = inner // SSD_GROUPS
    xbc = jnp.concatenate([x_ref[...], b_ref[...], c_ref[...]], axis=-1).astype(F32)
    ext = jnp.concatenate([tail_ref[...], xbc], axis=0)
    tail_ref[...] = xbc[tm - SUBLANES:, :]
    conv = cb_ref[...]
    for j in range(SSD_CONV):
        shift = SSD_CONV - 1 - j
        src = ext if shift == 0 else pltpu.roll(ext, shift, axis=0)
        conv = conv + cw_ref[j:j + 1, :] * src[SUBLANES:, :]
    xc = _silu(conv)
    xs = xc[:, :inner]
    bm = xc[:, inner:inner + SSD_GROUPS * SSD_STATE].astype(MXU_DTYPE)
    cm = xc[:, inner + SSD_GROUPS * SSD_STATE:].astype(MXU_DTYPE)

    dt_exp = _dot01_right(_softplus(ps_ref[...] + dtb_row_ref[...]), exp_ref[...])
    a_row = -jnp.exp(alog_row_ref[...])
    dta = dt_exp * a_row
    tril, triu, same = _chunk_masks(tm, SSD_CHUNK)
    acum, aend = _dot01_left_multi((tril, same), dta)
    dta_t = _softplus(pst_ref[...] + dtb_col_ref[...]) * (-jnp.exp(alog_col_ref[...]))
    acum_t = _dot01_right(dta_t, triu)
    x_dt = xs * dt_exp
    xw = (x_dt * jnp.exp(aend - acum)).astype(MXU_DTYPE)
    x_dt_m = x_dt.astype(MXU_DTYPE)
    e_acum = jnp.exp(acum)
    cr = lax.broadcasted_iota(jnp.int32, (SSD_CHUNK, SSD_CHUNK), 0)
    cc = lax.broadcasted_iota(jnp.int32, (SSD_CHUNK, SSD_CHUNK), 1)
    causal = cr >= cc
    dskip = dskip_ref[...]
    nw = nw_ref[...]
    hg = SSD_HEADS // SSD_GROUPS

    for c in range(tm // SSD_CHUNK):
        r0 = c * SSD_CHUNK
        rows = slice(r0, r0 + SSD_CHUNK)
        for g in range(SSD_GROUPS):
            gl = slice(g * gw, (g + 1) * gw)
            nl = slice(g * SSD_STATE, (g + 1) * SSD_STATE)
            b_g, c_g = bm[rows, nl], cm[rows, nl]
            cb = _dot_nt(c_g, b_g)
            parts = []
            for hh in range(hg):
                h = g * hg + hh
                lane0 = h * SSD_HEADDIM
                colv = jnp.broadcast_to(acum[rows, lane0:lane0 + 1], (SSD_CHUNK, SSD_CHUNK))
                rowv = acum_t[h:h + 1, r0:r0 + SSD_CHUNK]
                decay = jnp.where(causal, jnp.exp(colv - rowv), 0.0)
                mmat = (cb * decay).astype(MXU_DTYPE)
                parts.append(_dot(mmat, x_dt_m[rows, lane0:lane0 + SSD_HEADDIM]))
            y_diag = jnp.concatenate(parts, axis=-1)
            state_t = st_ref[:, gl]
            y_off = _dot(c_g, state_t.astype(MXU_DTYPE)) * e_acum[rows, gl]
            st_ref[:, gl] = state_t * jnp.exp(aend[r0:r0 + 1, gl]) + _dot_tn(b_g, xw[rows, gl])
            y = y_diag + y_off + dskip[:, gl] * xs[rows, gl]
            y = y * _silu(z_ref[rows, gl].astype(F32))
            o_ref[rows, gl] = _rmsnorm(y, nw[:, gl]).astype(o_ref.dtype)


def _ssd(p, ps, pst, cw, cb, dtb_row, dtb_col, alog_row, alog_col, dskip, nw, expand, *, batch, seq, tm):
    nt = seq // tm
    full = lambda a: pl.BlockSpec(a.shape, lambda b, i: (0,) * a.ndim)
    row = lambda b, i: b * nt + i
    conv_dim = SSD_INNER + 2 * SSD_GROUPS * SSD_STATE
    return pl.pallas_call(
        functools.partial(_ssd_kernel, tm=tm),
        out_shape=jax.ShapeDtypeStruct((p.shape[0], SSD_INNER), p.dtype),
        grid=(batch, nt),
        in_specs=[
            pl.BlockSpec((tm, 512), lambda b, i: (row(b, i), P_SSD_Z // 512)),
            pl.BlockSpec((tm, 512), lambda b, i: (row(b, i), P_SSD_X // 512)),
            pl.BlockSpec((tm, 256), lambda b, i: (row(b, i), P_SSD_B // 256)),
            pl.BlockSpec((tm, 256), lambda b, i: (row(b, i), P_SSD_C // 256)),
            pl.BlockSpec((tm, LANES), lambda b, i: (row(b, i), 0)),
            pl.BlockSpec((SUBLANES, tm), lambda b, i: (PS_DT // SUBLANES, row(b, i))),
            full(cw), full(cb), full(dtb_row), full(dtb_col), full(alog_row), full(alog_col),
            full(dskip), full(nw), full(expand),
        ],
        out_specs=pl.BlockSpec((tm, SSD_INNER), lambda b, i: (row(b, i), 0)),
        scratch_shapes=[pltpu.VMEM((SSD_STATE, SSD_INNER), F32),
                        pltpu.VMEM((SUBLANES, conv_dim), F32)],
        compiler_params=_cparams(("parallel", "arbitrary"), 40 << 20),
        name="ssd",
    )(p, p, p, p, ps, pst, cw, cb, dtb_row, dtb_col, alog_row, alog_col, dskip, nw, expand)


def _merge_kernel(x_ref, g_ref, y0_ref, y1_ref, y2_ref, y3_ref, wsq_ref, wbr_ref, o_ref, h_ref, acc_ref,
                  *, tc):
    j = pl.program_id(1)
    d = x_ref.shape[-1]

    @pl.when(j == 0)
    def _():
        _norm_rows(x_ref, g_ref, h_ref)
        acc_ref[...] = jnp.zeros_like(acc_ref)

    def branch(y_ref):
        h = h_ref[...]
        y = y_ref[...]
        for c in range(d // tc):
            cols = slice(c * tc, (c + 1) * tc)
            gate = jax.nn.sigmoid(_dot(h, wsq_ref[:, cols]))
            acc_ref[:, cols] += gate * _dot(y, wbr_ref[:, cols])

    for b, y_ref in enumerate((y0_ref, y1_ref, y2_ref, y3_ref)):
        pl.when(j == b)(functools.partial(branch, y_ref))

    @pl.when(j == N_BRANCH)
    def _():
        merged = acc_ref[...].astype(MXU_DTYPE)
        for c in range(d // tc):
            cols = slice(c * tc, (c + 1) * tc)
            o_ref[:, cols] = x_ref[:, cols] + _dot(merged, wsq_ref[:, cols])


def _merge(x, g, ys, wsq, wbr, *, layer, tm, tc):
    m, d = x.shape
    assert len(ys) == N_BRANCH
    bw = ys[0].shape[-1]
    wbytes = jnp.dtype(wsq.dtype).itemsize
    vmem = 4 * tm * d * 4 + tm * d * wbytes + tm * d * 4 + 2 * d * d * wbytes \
        + 2 * bw * d * wbytes + 3 * N_BRANCH * tm * bw * wbytes + 4 * tm * tc * 4 + (4 << 20)
    last = N_BRANCH - 1
    y_spec = pl.BlockSpec((tm, bw), lambda i, j: (i, 0))
    return pl.pallas_call(
        functools.partial(_merge_kernel, tc=tc),
        out_shape=jax.ShapeDtypeStruct((m, d), F32),
        grid=(m // tm, N_BRANCH + 1),
        in_specs=[
            pl.BlockSpec((tm, d), lambda i, j: (i, 0)),
            pl.BlockSpec((1, d), lambda i, j: (0, 0)),
            y_spec, y_spec, y_spec, y_spec,
            pl.BlockSpec((None, d, d), lambda i, j: (layer, 0, j)),
            pl.BlockSpec((None, None, bw, d), lambda i, j: (layer, jnp.minimum(j, last), 0, 0)),
        ],
        out_specs=pl.BlockSpec((tm, d), lambda i, j: (i, 0)),
        scratch_shapes=[pltpu.VMEM((tm, d), wsq.dtype), pltpu.VMEM((tm, d), F32)],
        compiler_params=_cparams(("parallel", "arbitrary"), vmem),
        name="merge",
    )(x, g, *ys, wsq, wbr)


def _xattn_kernel(x_ref, g_ref, wq_ref, k_ref, v_ref, wo_ref, o_ref, h_ref, oc_ref, *, tc):
    _norm_rows(x_ref, g_ref, h_ref)
    h = h_ref[...]
    dh = XATTN_DH
    for hd in range(XATTN_HEADS):
        cols = slice(hd * dh, (hd + 1) * dh)
        q = _dot(h, wq_ref[:, cols]).astype(MXU_DTYPE)
        s = _dot_nt(q, k_ref[:, cols]) * (dh ** -0.5 * LOG2E)
        p = jnp.exp2(s - jnp.max(s, axis=-1, keepdims=True))
        p = p / jnp.sum(p, axis=-1, keepdims=True)
        oc_ref[:, cols] = _dot(p.astype(MXU_DTYPE), v_ref[:, cols]).astype(oc_ref.dtype)
    oc = oc_ref[...]
    d = x_ref.shape[-1]
    for c in range(d // tc):
        cols = slice(c * tc, (c + 1) * tc)
        o_ref[:, cols] = x_ref[:, cols] + _dot(oc, wo_ref[:, cols])


def _xattn(x, g, wq, kv, wo, *, layer, seq, n_mem, tm, tc):
    m, d = x.shape
    wbytes = jnp.dtype(wq.dtype).itemsize
    vmem = 4 * tm * d * 4 + 2 * tm * d * wbytes + 2 * d * d * wbytes + 4 * n_mem * d * wbytes \
        + 3 * tm * XATTN_DH * 4 + 3 * tm * n_mem * 4 + 2 * tm * tc * 4 + (6 << 20)
    tiles_per_seq = seq // tm
    return pl.pallas_call(
        functools.partial(_xattn_kernel, tc=tc),
        out_shape=jax.ShapeDtypeStruct((m, d), F32),
        grid=(m // tm,),
        in_specs=[
            pl.BlockSpec((tm, d), lambda i: (i, 0)),
            pl.BlockSpec((1, d), lambda i: (0, 0)),
            pl.BlockSpec((None, d, d), lambda i: (layer, 0, 0), pipeline_mode=pl.Buffered(1)),
            pl.BlockSpec((n_mem, d), lambda i: (i // tiles_per_seq, 0)),
            pl.BlockSpec((n_mem, d), lambda i: (i // tiles_per_seq, 1)),
            pl.BlockSpec((None, d, d), lambda i: (layer, 0, 0), pipeline_mode=pl.Buffered(1)),
        ],
        out_specs=pl.BlockSpec((tm, d), lambda i: (i, 0)),
        scratch_shapes=[pltpu.VMEM((tm, d), wq.dtype), pltpu.VMEM((tm, d), wq.dtype)],
        compiler_params=_cparams(("parallel",), vmem),
        name="xattn",
    )(x, g, wq, kv, kv, wo)


_IN_SIZES = (N_BRANCH * D_MODEL, MLA_Q_RANK, MLA_KV_RANK, MLA_ROPE, 3 * FOX_HEADS * FOX_DH, FOX_HEADS,
             GLA_HEADS * GLA_DK, GLA_HEADS * GLA_DK, GLA_HEADS * GLA_DV, GLA_GATE_RANK,
             GLA_HEADS * GLA_DV, SSD_INNER, SSD_INNER + 2 * SSD_GROUPS * SSD_STATE, SSD_HEADS)


def _rot_cols(w):
    half = w.shape[-1] // 2
    return jnp.concatenate([-w[..., half:], w[..., :half]], axis=-1)


def _prep_w_in(w_mix):
    w_in = w_mix.astype(MXU_DTYPE)
    nl, d = w_in.shape[:2]
    offs = np.cumsum((0,) + _IN_SIZES[1:])
    assert offs[-1] == w_in.shape[2]
    seg = lambda i: w_in[:, :, offs[i]:offs[i + 1]]
    (cq, ckv, kr, fox, fox_f, gq, gk, gv, glr, gr, sz, sxbc, sdt) = [seg(i) for i in range(len(_IN_SIZES) - 1)]
    z = lambda n: jnp.zeros((nl, d, n), w_in.dtype)
    w_p = jnp.concatenate([
        cq, z(512 - MLA_Q_RANK), ckv, kr, z(LANES - MLA_ROPE), _rot_cols(kr), z(LANES - MLA_ROPE),
        fox, gq, gk, gv, gr, sz, sxbc], axis=2)
    assert w_p.shape[2] == P_WIDTH
    cs = np.ones((1, P_WIDTH), np.float32)
    cs[:, P_FOX_Q:P_FOX_K] = FOX_DH ** -0.5 * LOG2E
    cs[:, P_GLA_Q:P_GLA_K] = GLA_DK ** -0.5
    w_s = jnp.concatenate([glr, sdt, fox_f, z(LANES - GLA_GATE_RANK - SSD_HEADS - FOX_HEADS)], axis=2)
    return w_p.astype(MXU_DTYPE), jnp.asarray(cs), w_s.astype(MXU_DTYPE)


def _prep_mla(w_uq, w_ukv):
    nl, r = w_uq.shape[:2]
    wq = w_uq.reshape(nl, r, MLA_HEADS, MLA_NOPE + MLA_ROPE)
    nope, pe = wq[..., :MLA_NOPE], wq[..., MLA_NOPE:]
    zpad = jnp.zeros((nl, r, MLA_HEADS, MLA_DK_PAD - MLA_NOPE - MLA_ROPE), w_uq.dtype)
    wa = jnp.concatenate([nope, pe, zpad], axis=-1).reshape(nl, r, MLA_HEADS * MLA_DK_PAD)
    wb = jnp.concatenate([_rot_cols(pe), zpad], axis=-1).reshape(nl, r, MLA_HEADS * LANES)
    rk = w_ukv.shape[1]
    wkv = w_ukv.reshape(nl, rk, MLA_HEADS, MLA_NOPE + MLA_V)
    wkv = jnp.concatenate([wkv[..., :MLA_NOPE].reshape(nl, rk, -1), wkv[..., MLA_NOPE:].reshape(nl, rk, -1)],
                          axis=2)
    return wa.astype(MXU_DTYPE), wb.astype(MXU_DTYPE), wkv.astype(MXU_DTYPE)


def _tiles(m, seq):
    mid = min(512, seq)
    return dict(ffn_tm=mid, ffn_tf=512, proj_tm=mid, proj_tn=2816, prep_tm=min(1024, seq),
                flash_tb=mid, flash_rmult=min(2, seq // mid), gla_tm=min(512, seq), ssd_tm=min(256, seq), merge_tm=mid, merge_tc=1024,
                xattn_tm=mid, xattn_tc=1024, rope_tm=min(1024, m))


@jax.jit
def kernel(x, mem, positions, ffn1_norm, ffn1_w_gate, ffn1_w_up, ffn1_w_down, mix_norm, w_in, mla_q_norm, mla_kv_norm, mla_w_uq, mla_w_ukv, fox_f_bias, gla_w_g2, gla_b_g2, gla_norm, ssd_conv_w, ssd_conv_b, ssd_dt_bias, ssd_a_log, ssd_d, ssd_norm, w_branch, w_out, xattn_norm, mem_norm, xattn_w_q, xattn_w_k, xattn_w_v, xattn_w_o, ffn2_norm, ffn2_w_gate, ffn2_w_up, ffn2_w_down, final_norm):
    batch, seq, d = x.shape
    n_mem = mem.shape[1]
    m = batch * seq
    t = _tiles(m, seq)
    depth = w_in.shape[0]
    cast = lambda a: a.astype(MXU_DTYPE)
    row = lambda a: a.reshape(1, -1).astype(F32)
    col8 = lambda v: jnp.zeros((SUBLANES, 1), F32).at[:v.shape[0], 0].set(v.astype(F32))

    xs = x.reshape(m, d)
    mem2 = mem.reshape(batch * n_mem, d)
    ct, st = _rope_tables(positions.reshape(m, 1), tm=t["rope_tm"])

    expand = np.zeros((LANES, SSD_INNER), np.float32)
    for h in range(SSD_HEADS):
        expand[PS_DT + h, h * SSD_HEADDIM:(h + 1) * SSD_HEADDIM] = 1.0
    expand = jnp.asarray(expand, MXU_DTYPE)
    rep = lambda v: jnp.repeat(v.astype(F32), SSD_HEADDIM).reshape(1, SSD_INNER)
    ffn = functools.partial(_ffn, tm=t["ffn_tm"], tf=t["ffn_tf"])
    flash = functools.partial(_flash, batch=batch, seq=seq, groups=2, tb=t["flash_tb"], rmult=t["flash_rmult"])

    f1g, f1u, f1d = cast(ffn1_w_gate), cast(ffn1_w_up), cast(ffn1_w_down)
    f2g, f2u, f2d = cast(ffn2_w_gate), cast(ffn2_w_up), cast(ffn2_w_down)
    w_p, cs, w_s = _prep_w_in(w_in[:, :, N_BRANCH * d:])
    wa, wb, wkv = _prep_mla(mla_w_uq, mla_w_ukv)
    wsq = jnp.concatenate([cast(w_in[:, :, :N_BRANCH * d]), cast(w_out)], axis=2)
    wbr = cast(w_branch)
    w_kv = cast(jnp.concatenate([xattn_w_k, xattn_w_v], axis=2))
    wq_x, wo_x = cast(xattn_w_q), cast(xattn_w_o)

    for l in range(depth):
        xs = ffn(xs, row(ffn1_norm[l]), f1g, f1u, f1d, row(final_norm), layer=l, final_norm=False)

        p, ps, pst = _proj(xs, row(mix_norm[l]), w_p, cs, w_s, layer=l, tm=t["proj_tm"], tn=t["proj_tn"])

        q_mla, k_mla, v_mla = _mla_prep(p, ct, st, row(mla_q_norm[l]), row(mla_kv_norm[l]), wa, wb, wkv,
                                        layer=l, tm=t["prep_tm"])
        y_mla = flash(q_mla, k_mla, v_mla, None, heads=MLA_HEADS,
                      dk=MLA_DK_PAD, dv=MLA_V, q_off=0, k_off=0, v_off=0)

        c_fox = _fox_cum(pst, col8(fox_f_bias[l]), batch=batch, seq=seq)
        y_fox = flash(p, p, p, c_fox, heads=FOX_HEADS,
                      dk=FOX_DH, dv=FOX_DH, q_off=P_FOX_Q, k_off=P_FOX_K, v_off=P_FOX_V)

        wg2 = jnp.zeros((LANES, GLA_HEADS * GLA_DK), F32).at[PS_GLR:PS_GLR + GLA_GATE_RANK].set(gla_w_g2[l])
        y_gla = _gla(p, ps, cast(wg2), row(gla_b_g2[l]), row(gla_norm[l]), batch=batch, seq=seq, tm=t["gla_tm"])

        dtb_row = jnp.zeros((1, LANES), F32).at[0, PS_DT:PS_DT + SSD_HEADS].set(ssd_dt_bias[l])
        y_ssd = _ssd(p, ps, pst, ssd_conv_w[l].astype(F32), row(ssd_conv_b[l]), dtb_row, col8(ssd_dt_bias[l]),
                     rep(ssd_a_log[l]), col8(ssd_a_log[l]), rep(ssd_d[l]), row(ssd_norm[l]), expand,
                     batch=batch, seq=seq, tm=t["ssd_tm"])

        xs = _merge(xs, row(mix_norm[l]), (y_mla, y_fox, y_gla, y_ssd), wsq, wbr, layer=l, tm=t["merge_tm"], tc=t["merge_tc"])

        kv = _norm_matmul(mem2, row(mem_norm[l]), w_kv, layer=l, tm=min(512, batch * n_mem), tn=1024)
        xs = _xattn(xs, row(xattn_norm[l]), wq_x, kv, wo_x, layer=l,
                    seq=seq, n_mem=n_mem, tm=t["xattn_tm"], tc=t["xattn_tc"])

        xs = ffn(xs, row(ffn2_norm[l]), f2g, f2u, f2d, row(final_norm), layer=l, final_norm=(l == depth - 1))

    return xs.reshape(batch, seq, d)
```

```python
import functools

import numpy as np
import jax
import jax.numpy as jnp
from jax import lax
from jax.experimental import pallas as pl
from jax.experimental.pallas import tpu as pltpu

D_MODEL = 2048
DEPTH = 4
EPS = 1e-6
N_BRANCH = 4
BRANCH_WIDTH = 512
MLA_HEADS, MLA_NOPE, MLA_ROPE, MLA_V = 4, 128, 64, 128
MLA_Q_RANK, MLA_KV_RANK = 384, 256
MLA_DK_PAD = 256
ROPE_THETA = 10000.0
FOX_HEADS, FOX_DH = 4, 128
GLA_HEADS, GLA_DK, GLA_DV = 4, 64, 128
GLA_GATE_RANK, GLA_GATE_NORM, GLA_CHUNK = 16, 16.0, 64
SSD_HEADS, SSD_HEADDIM, SSD_GROUPS, SSD_STATE, SSD_CONV, SSD_CHUNK = 8, 64, 2, 128, 4, 128
SSD_INNER = SSD_HEADS * SSD_HEADDIM
XATTN_HEADS = 4
XATTN_DH = D_MODEL // XATTN_HEADS
D_FF = 5632

LANES = 128
SUBLANES = 8
V7X_VMEM_LIMIT_BYTES = 60 * 1024 * 1024

MXU_DTYPE = jnp.bfloat16
FFN_SLOTS = 3
F32 = jnp.float32
LOG2E = 1.4426950408889634

P_CQ, P_CKV, P_KR, P_KRROT = 0, 512, 768, 896
P_FOX_Q, P_FOX_K, P_FOX_V = 1024, 1536, 2048
P_GLA_Q, P_GLA_K, P_GLA_V, P_GLA_R = 2560, 2816, 3072, 3584
P_SSD_Z, P_SSD_X, P_SSD_B, P_SSD_C = 4096, 4608, 5120, 5376
P_WIDTH = 5632
PS_GLR, PS_DT, PS_F = 0, 16, 24
PST_ROWS = 32


def _cparams(sem, vmem_bytes):
    return pltpu.CompilerParams(dimension_semantics=sem,
                                vmem_limit_bytes=int(min(vmem_bytes, V7X_VMEM_LIMIT_BYTES)))


def _dot(a, b):
    return jnp.dot(a, b, preferred_element_type=F32)


def _dot_nt(a, b):
    return lax.dot_general(a, b, (((1,), (1,)), ((), ())), preferred_element_type=F32)


def _dot_tn(a, b):
    return lax.dot_general(a, b, (((0,), (0,)), ((), ())), preferred_element_type=F32)


def _split3(x):
    hi = x.astype(MXU_DTYPE)
    r1 = x - hi.astype(F32)
    mid = r1.astype(MXU_DTYPE)
    lo = (r1 - mid.astype(F32)).astype(MXU_DTYPE)
    return hi, mid, lo


def _dot01_right(x, m01):
    if MXU_DTYPE == F32:
        return _dot(x, m01.astype(F32))
    m = m01.astype(MXU_DTYPE)
    hi, mid, lo = _split3(x)
    return _dot(hi, m) + _dot(mid, m) + _dot(lo, m)


def _dot01_left_multi(m01s, x):
    if MXU_DTYPE == F32:
        return [_dot(m01.astype(F32), x) for m01 in m01s]
    parts = _split3(x)
    out = []
    for m01 in m01s:
        m = m01.astype(MXU_DTYPE)
        out.append(_dot(m, parts[0]) + _dot(m, parts[1]) + _dot(m, parts[2]))
    return out


def _dot01_left(m01, x):
    return _dot01_left_multi((m01,), x)[0]


def _rmsnorm(x, g):
    ms = jnp.mean(x * x, axis=-1, keepdims=True)
    return x * lax.rsqrt(ms + EPS) * g


def _log_sigmoid(x):
    return jnp.minimum(x, 0.0) - jnp.log1p(jnp.exp(-jnp.abs(x)))


def _softplus(x):
    return jnp.maximum(x, 0.0) + jnp.log1p(jnp.exp(-jnp.abs(x)))


def _silu(x):
    return x * jax.nn.sigmoid(x)


def _chunk_masks(n, chunk):
    shift = chunk.bit_length() - 1
    assert 1 << shift == chunk
    r = lax.broadcasted_iota(jnp.int32, (n, n), 0)
    c = lax.broadcasted_iota(jnp.int32, (n, n), 1)
    same = jnp.right_shift(r, shift) == jnp.right_shift(c, shift)
    return jnp.logical_and(same, r >= c), jnp.logical_and(same, r <= c), same


def _norm_rows(x_ref, g_ref, h_ref, copy_ref=None, *, rows=256):
    tm = x_ref.shape[0]
    rows = min(rows, tm)
    g = g_ref[...]
    for r in range(tm // rows):
        sl = slice(r * rows, (r + 1) * rows)
        x = x_ref[sl, :]
        h_ref[sl, :] = _rmsnorm(x, g).astype(h_ref.dtype)
        if copy_ref is not None:
            copy_ref[sl, :] = x


def _ffn_kernel(x_ref, g_ref, wg_hbm, wu_hbm, wd_hbm, fg_ref, o_ref, h_ref, wg_buf, wu_buf, wd_buf, sem,
                *, final_norm, layer, tf, n_chunks):
    i = pl.program_id(0)

    def chunk_copies(j, slot):
        cols = pl.ds(j * tf, tf)
        return (pltpu.make_async_copy(wg_hbm.at[layer, :, cols], wg_buf.at[slot], sem.at[0, slot]),
                pltpu.make_async_copy(wu_hbm.at[layer, :, cols], wu_buf.at[slot], sem.at[1, slot]),
                pltpu.make_async_copy(wd_hbm.at[layer, cols, :], wd_buf.at[slot], sem.at[2, slot]))

    def start(j, slot):
        for cp in chunk_copies(j, slot):
            cp.start()

    @pl.when(i == 0)
    def _():
        start(0, 0)

    _norm_rows(x_ref, g_ref, h_ref, o_ref)
    h = h_ref[...]
    for j in range(n_chunks):
        slot = j % FFN_SLOTS
        for cp in chunk_copies(j, slot):
            cp.wait()
        if j + 1 < n_chunks:
            start(j + 1, (j + 1) % FFN_SLOTS)
        else:
            @pl.when(i + 1 < pl.num_programs(0))
            def _():
                start(0, 0)
        a = _dot(h, wg_buf[slot])
        u = _dot(h, wu_buf[slot])
        t = (_silu(a) * (0.5 * u)).astype(MXU_DTYPE)
        o_ref[...] += _dot(t, wd_buf[slot])

    if final_norm:
        _norm_rows(o_ref, fg_ref, o_ref)


def _ffn(x, g, wg, wu, wd, fg, *, layer, final_norm, tm, tf):
    m, d = x.shape
    f = wg.shape[-1]
    n_chunks = f // tf
    assert n_chunks * tf == f and n_chunks > FFN_SLOTS and (n_chunks - 1) % FFN_SLOTS != 0
    wb = jnp.dtype(wg.dtype).itemsize
    vmem = 4 * tm * d * 4 + tm * d * wb + FFN_SLOTS * 3 * d * tf * wb + 4 * tm * tf * 4 + tm * d * 4 + (4 << 20)
    return pl.pallas_call(
        functools.partial(_ffn_kernel, final_norm=final_norm, layer=layer, tf=tf, n_chunks=n_chunks),
        out_shape=jax.ShapeDtypeStruct((m, d), F32),
        grid=(m // tm,),
        in_specs=[
            pl.BlockSpec((tm, d), lambda i: (i, 0)),
            pl.BlockSpec((1, d), lambda i: (0, 0)),
            pl.BlockSpec(memory_space=pl.ANY),
            pl.BlockSpec(memory_space=pl.ANY),
            pl.BlockSpec(memory_space=pl.ANY),
            pl.BlockSpec((1, d), lambda i: (0, 0)),
        ],
        out_specs=pl.BlockSpec((tm, d), lambda i: (i, 0)),
        scratch_shapes=[pltpu.VMEM((tm, d), wg.dtype),
                        pltpu.VMEM((FFN_SLOTS, d, tf), wg.dtype),
                        pltpu.VMEM((FFN_SLOTS, d, tf), wg.dtype),
                        pltpu.VMEM((FFN_SLOTS, tf, d), wg.dtype),
                        pltpu.SemaphoreType.DMA((3, FFN_SLOTS))],
        compiler_params=_cparams(("arbitrary",), vmem),
        name="ffn",
    )(x, g, wg, wu, wd, fg)


def _proj_kernel(x_ref, g_ref, w_ref, cs_ref, ws_ref, p_ref, ps_ref, pst_ref, h_ref):
    j = pl.program_id(1)

    @pl.when(j == 0)
    def _():
        _norm_rows(x_ref, g_ref, h_ref)
        ps = _dot(h_ref[...], ws_ref[...])
        ps_ref[...] = ps
        pst_ref[...] = jnp.transpose(ps)[:PST_ROWS, :]

    p_ref[...] = (_dot(h_ref[...], w_ref[...]) * cs_ref[...]).astype(p_ref.dtype)


def _proj(x, g, w, cs, ws, *, layer, tm, tn):
    m, d = x.shape
    n = w.shape[-1]
    wb = jnp.dtype(w.dtype).itemsize
    vmem = 2 * tm * d * 4 + tm * d * wb + 2 * d * tn * wb + 2 * tm * tn * wb + 2 * tm * tn * 4 + (6 << 20)
    return pl.pallas_call(
        _proj_kernel,
        out_shape=(jax.ShapeDtypeStruct((m, n), w.dtype),
                   jax.ShapeDtypeStruct((m, LANES), F32),
                   jax.ShapeDtypeStruct((PST_ROWS, m), F32)),
        grid=(m // tm, n // tn),
        in_specs=[
            pl.BlockSpec((tm, d), lambda i, j: (i, 0)),
            pl.BlockSpec((1, d), lambda i, j: (0, 0)),
            pl.BlockSpec((None, d, tn), lambda i, j: (layer, 0, j)),
            pl.BlockSpec((1, tn), lambda i, j: (0, j)),
            pl.BlockSpec((None, d, LANES), lambda i, j: (layer, 0, 0)),
        ],
        out_specs=(pl.BlockSpec((tm, tn), lambda i, j: (i, j)),
                   pl.BlockSpec((tm, LANES), lambda i, j: (i, 0)),
                   pl.BlockSpec((PST_ROWS, tm), lambda i, j: (0, i))),
        scratch_shapes=[pltpu.VMEM((tm, d), w.dtype)],
        compiler_params=_cparams(("parallel", "arbitrary"), vmem),
        name="proj",
    )(x, g, w, cs, ws)


def _norm_matmul_kernel(x_ref, g_ref, w_ref, o_ref):
    h = _rmsnorm(x_ref[...], g_ref[...]).astype(w_ref.dtype)
    o_ref[...] = _dot(h, w_ref[...]).astype(o_ref.dtype)


def _norm_matmul(x, g, w, *, layer, tm, tn):
    m, d = x.shape
    n = w.shape[-1]
    wb = jnp.dtype(w.dtype).itemsize
    vmem = 3 * tm * d * 4 + 2 * d * tn * wb + 3 * tm * tn * 4 + (4 << 20)
    return pl.pallas_call(
        _norm_matmul_kernel,
        out_shape=jax.ShapeDtypeStruct((m, n), w.dtype),
        grid=(m // tm, n // tn),
        in_specs=[pl.BlockSpec((tm, d), lambda i, j: (i, 0)),
                  pl.BlockSpec((1, d), lambda i, j: (0, 0)),
                  pl.BlockSpec((None, d, tn), lambda i, j: (layer, 0, j))],
        out_specs=pl.BlockSpec((tm, tn), lambda i, j: (i, j)),
        compiler_params=_cparams(("parallel", "parallel"), vmem),
        name="mem_kv",
    )(x, g, w)


def _rope_kernel(pos_ref, ct_ref, st_ref):
    pos = pos_ref[...].astype(F32)
    lane = lax.broadcasted_iota(jnp.int32, (1, LANES), 1)
    half = MLA_ROPE // 2
    f = jnp.bitwise_and(lane, half - 1).astype(F32)
    inv_freq = jnp.power(jnp.float32(ROPE_THETA), -f / half)
    ang = pos * inv_freq
    valid = lane < MLA_ROPE
    ct_ref[...] = jnp.where(valid, jnp.cos(ang), 0.0)
    st_ref[...] = jnp.where(valid, jnp.sin(ang), 0.0)


def _rope_tables(pos_col, *, tm):
    m = pos_col.shape[0]
    return pl.pallas_call(
        _rope_kernel,
        out_shape=(jax.ShapeDtypeStruct((m, LANES), F32), jax.ShapeDtypeStruct((m, LANES), F32)),
        grid=(m // tm,),
        in_specs=[pl.BlockSpec((tm, 1), lambda i: (i, 0))],
        out_specs=(pl.BlockSpec((tm, LANES), lambda i: (i, 0)),
                   pl.BlockSpec((tm, LANES), lambda i: (i, 0))),
        compiler_params=_cparams(("parallel",), 16 << 20),
        name="rope_tables",
    )(pos_col)


def _mla_prep_kernel(cq_ref, ckv_ref, kr_ref, krrot_ref, ct_ref, st_ref, qn_ref, kvn_ref,
                     wa_ref, wb_ref, wkv_ref, q_ref, k_ref, v_ref):
    ct = ct_ref[...]
    st = st_ref[...]
    scale = (MLA_NOPE + MLA_ROPE) ** -0.5 * LOG2E

    ckv = ckv_ref[...].astype(F32)
    ckvn = _rmsnorm(ckv, kvn_ref[...]).astype(MXU_DTYPE)
    kv = _dot(ckvn, wkv_ref[...])
    kpe = kr_ref[...].astype(F32) * ct + krrot_ref[...].astype(F32) * st
    kpe = kpe.astype(k_ref.dtype)
    hw = MLA_HEADS * MLA_NOPE
    for h in range(MLA_HEADS):
        base = h * MLA_DK_PAD
        k_ref[:, base:base + MLA_NOPE] = kv[:, h * MLA_NOPE:(h + 1) * MLA_NOPE].astype(k_ref.dtype)
        k_ref[:, base + MLA_NOPE:base + MLA_DK_PAD] = kpe
    v_ref[...] = kv[:, hw:].astype(v_ref.dtype)

    cq = cq_ref[...].astype(F32)[:, :MLA_Q_RANK]
    cqn = _rmsnorm(cq, qn_ref[...]).astype(MXU_DTYPE)
    qa = _dot(cqn, wa_ref[...])
    qb = _dot(cqn, wb_ref[...])
    for h in range(MLA_HEADS):
        base = h * MLA_DK_PAD
        q_ref[:, base:base + MLA_NOPE] = (qa[:, base:base + MLA_NOPE] * scale).astype(q_ref.dtype)
        pe = qa[:, base + MLA_NOPE:base + MLA_DK_PAD] * ct + qb[:, h * LANES:(h + 1) * LANES] * st
        q_ref[:, base + MLA_NOPE:base + MLA_DK_PAD] = (pe * scale).astype(q_ref.dtype)


def _mla_prep(p, ct, st, qn, kvn, wa, wb, wkv, *, layer, tm):
    m = p.shape[0]
    qk_w = MLA_HEADS * MLA_DK_PAD
    v_w = MLA_HEADS * MLA_V
    full = lambda a: pl.BlockSpec(a.shape, lambda i: (0,) * a.ndim)
    of_layer = lambda a: pl.BlockSpec((None,) + a.shape[1:], lambda i: (layer,) + (0,) * (a.ndim - 1))
    return pl.pallas_call(
        _mla_prep_kernel,
        out_shape=(jax.ShapeDtypeStruct((m, qk_w), p.dtype),
                   jax.ShapeDtypeStruct((m, qk_w), p.dtype),
                   jax.ShapeDtypeStruct((m, v_w), p.dtype)),
        grid=(m // tm,),
        in_specs=[
            pl.BlockSpec((tm, 512), lambda i: (i, P_CQ // 512)),
            pl.BlockSpec((tm, 256), lambda i: (i, P_CKV // 256)),
            pl.BlockSpec((tm, 128), lambda i: (i, P_KR // 128)),
            pl.BlockSpec((tm, 128), lambda i: (i, P_KRROT // 128)),
            pl.BlockSpec((tm, LANES), lambda i: (i, 0)),
            pl.BlockSpec((tm, LANES), lambda i: (i, 0)),
            full(qn), full(kvn), of_layer(wa), of_layer(wb), of_layer(wkv),
        ],
        out_specs=(pl.BlockSpec((tm, qk_w), lambda i: (i, 0)),
                   pl.BlockSpec((tm, qk_w), lambda i: (i, 0)),
                   pl.BlockSpec((tm, v_w), lambda i: (i, 0))),
        compiler_params=_cparams(("parallel",), 40 << 20),
        name="mla_prep",
    )(p, p, p, p, ct, st, qn, kvn, wa, wb, wkv)


def _fox_cum_kernel(ft_ref, fb_ref, c_ref, *, seq):
    lf = _log_sigmoid(ft_ref[...] + fb_ref[...])
    r = lax.broadcasted_iota(jnp.int32, (LANES, LANES), 0)
    c = lax.broadcasted_iota(jnp.int32, (LANES, LANES), 1)
    upper = r <= c
    carry = jnp.zeros((SUBLANES, 1), F32)
    for blk in range(seq // LANES):
        sl = slice(blk * LANES, (blk + 1) * LANES)
        cs = _dot01_right(lf[:, sl], upper) + carry
        for h in range(FOX_HEADS):
            c_ref[h, :, sl] = cs[h:h + 1, :] * LOG2E
        carry = cs[:, LANES - 1:LANES]


def _fox_cum(pst, fb_col, *, batch, seq):
    return pl.pallas_call(
        functools.partial(_fox_cum_kernel, seq=seq),
        out_shape=jax.ShapeDtypeStruct((batch * FOX_HEADS, 1, seq), F32),
        grid=(batch,),
        in_specs=[pl.BlockSpec((SUBLANES, seq), lambda b: (PS_F // SUBLANES, b)),
                  pl.BlockSpec((SUBLANES, 1), lambda b: (0, 0))],
        out_specs=pl.BlockSpec((FOX_HEADS, 1, seq), lambda b: (b, 0, 0)),
        compiler_params=_cparams(("parallel",), 16 << 20),
        name="fox_cum",
    )(pst, fb_col)


def _flash_kernel(*refs, seq, tb, rmult, groups, dk, dv, has_bias):
    refs = list(refs)
    q_ref, k_ref, v_ref = refs[:3]
    c_ref = refs[3] if has_bias else None
    o_ref = refs[3 + int(has_bias)]
    rows_st = tb * rmult

    def update(state, s, v_blk):
        m, l, acc = state
        m_new = jnp.maximum(m, jnp.max(s, axis=-1, keepdims=True))
        alpha = jnp.exp2(m - m_new)
        p = jnp.exp2(s - m_new)
        l = alpha * l + jnp.sum(p, axis=-1, keepdims=True)
        acc = alpha * acc + _dot(p.astype(MXU_DTYPE), v_blk)
        return m_new, l, acc

    def super_tile(pj, unused):
        r0 = pl.multiple_of(pj * rows_st, rows_st)
        qs = [q_ref[pl.ds(r0, rows_st), g * dk:(g + 1) * dk] for g in range(groups)]
        if has_bias:
            c0 = [jnp.max(c_ref[g, :, pl.ds(r0, tb)], axis=-1, keepdims=True) for g in range(groups)]

        def scores(q, g, start, width=tb):
            s = _dot_nt(q, k_ref[pl.ds(start, width), g * dk:(g + 1) * dk])
            if has_bias:
                s = s + (c0[g] - c_ref[g, :, pl.ds(start, width)])
            return s

        def v_block(g, start, width=tb):
            return v_ref[pl.ds(start, width), g * dv:(g + 1) * dv]

        def body(ki, carry):
            start = pl.multiple_of(ki * rows_st, rows_st)
            return tuple(update(carry[g], scores(qs[g], g, start, rows_st), v_block(g, start, rows_st))
                         for g in range(groups))

        init = tuple((jnp.full((rows_st, 1), -jnp.inf, F32), jnp.zeros((rows_st, 1), F32),
                      jnp.zeros((rows_st, dv), F32)) for _ in range(groups))
        carry = lax.fori_loop(0, pj, body, init)

        for g in range(groups):
            m, l, acc = carry[g]
            for a in range(rmult):
                rs = slice(a * tb, (a + 1) * tb)
                width = (a + 1) * tb
                row = lax.broadcasted_iota(jnp.int32, (tb, width), 0)
                col = lax.broadcasted_iota(jnp.int32, (tb, width), 1)
                s = jnp.where(col <= row + a * tb, scores(qs[g][rs], g, r0, width), -jnp.inf)
                state = update((m[rs], l[rs], acc[rs]), s, v_block(g, r0, width))
                o_ref[pl.ds(pl.multiple_of(r0 + a * tb, tb), tb), g * dv:(g + 1) * dv] = \
                    (state[2] / state[1]).astype(o_ref.dtype)
        return unused

    lax.fori_loop(0, seq // rows_st, super_tile, 0)


def _flash(q_arr, k_arr, v_arr, c_arr, *, batch, seq, heads, groups, dk, dv,
           q_off, k_off, v_off, tb, rmult):
    rows_st = tb * rmult
    has_bias = c_arr is not None
    gk, gv = groups * dk, groups * dv
    in_specs = [pl.BlockSpec((seq, gk), lambda b, h: (b, q_off // gk + h)),
                pl.BlockSpec((seq, gk), lambda b, h: (b, k_off // gk + h)),
                pl.BlockSpec((seq, gv), lambda b, h: (b, v_off // gv + h))]
    args = [q_arr, k_arr, v_arr]
    if has_bias:
        in_specs.append(pl.BlockSpec((groups, 1, seq), lambda b, h: (b * (heads // groups) + h, 0, 0)))
        args.append(c_arr)
    vmem = 2 * seq * (2 * gk + 2 * gv) * 2 + groups * (3 * rows_st * rows_st * 4 + 4 * rows_st * dv * 4) + (8 << 20)
    return pl.pallas_call(
        functools.partial(_flash_kernel, seq=seq, tb=tb, rmult=rmult, groups=groups, dk=dk, dv=dv,
                          has_bias=has_bias),
        out_shape=jax.ShapeDtypeStruct((batch * seq, heads * dv), q_arr.dtype),
        grid=(batch, heads // groups),
        in_specs=in_specs,
        out_specs=pl.BlockSpec((seq, gv), lambda b, h: (b, h)),
        compiler_params=_cparams(("parallel", "parallel"), vmem),
        name="flash_fox" if has_bias else "flash_mla",
    )(*args)


def _gla_kernel(q_ref, k_ref, v_ref, r_ref, ps_ref, wg2_ref, bg2_ref, nw_ref, o_ref, st_ref, *, tm):
    @pl.when(pl.program_id(1) == 0)
    def _():
        st_ref[...] = jnp.zeros_like(st_ref)

    nc = tm // GLA_CHUNK
    glin = _dot(ps_ref[...].astype(MXU_DTYPE), wg2_ref[...]) + bg2_ref[...]
    gl = _log_sigmoid(glin) / GLA_GATE_NORM
    tril, _, same = _chunk_masks(tm, GLA_CHUNK)
    gcum, gend = _dot01_left_multi((tril, same), gl)
    q = q_ref[...].astype(F32)
    k = k_ref[...].astype(F32)
    qd = (q * jnp.exp(gcum)).astype(MXU_DTYPE)
    kd = (k * jnp.exp(-gcum)).astype(MXU_DTYPE)
    ku = (k * jnp.exp(gend - gcum)).astype(MXU_DTYPE)
    dec = jnp.exp(gend)
    nw = nw_ref[...]
    er = lax.broadcasted_iota(jnp.int32, (tm, nc * GLA_DK), 0)
    ec = lax.broadcasted_iota(jnp.int32, (tm, nc * GLA_DK), 1)
    own = jnp.right_shift(er, GLA_CHUNK.bit_length() - 1) == jnp.right_shift(ec, GLA_DK.bit_length() - 1)
    zero = jnp.zeros((), MXU_DTYPE)

    for h in range(GLA_HEADS):
        kc = slice(h * GLA_DK, (h + 1) * GLA_DK)
        vc = slice(h * GLA_DV, (h + 1) * GLA_DV)
        qd_h, kd_h, ku_h = qd[:, kc], kd[:, kc], ku[:, kc]
        v_h = v_ref[:, vc]
        att = jnp.where(tril, _dot_nt(qd_h, kd_h), 0.0)
        o = _dot(att.astype(MXU_DTYPE), v_h)
        qd_x = jnp.where(own, jnp.concatenate([qd_h] * nc, axis=1), zero)
        ku_x = jnp.where(own, jnp.concatenate([ku_h] * nc, axis=1), zero)
        upd = _dot_tn(v_h, ku_x)
        state = st_ref[:, kc]
        entering = []
        for c in range(nc):
            entering.append(state)
            state = state * dec[c * GLA_CHUNK:c * GLA_CHUNK + 1, kc] + upd[:, c * GLA_DK:(c + 1) * GLA_DK]
        st_ref[:, kc] = state
        s_all = jnp.concatenate(entering, axis=1).astype(MXU_DTYPE)
        o = o + _dot_nt(qd_x, s_all)
        o = _rmsnorm(o, nw)
        o_ref[:, vc] = (o * _silu(r_ref[:, vc].astype(F32))).astype(o_ref.dtype)


def _gla(p, ps, wg2, bg2, nw, *, batch, seq, tm):
    nt = seq // tm
    full = lambda a: pl.BlockSpec(a.shape, lambda b, i: (0,) * a.ndim)
    row = lambda b, i: b * nt + i
    return pl.pallas_call(
        functools.partial(_gla_kernel, tm=tm),
        out_shape=jax.ShapeDtypeStruct((p.shape[0], GLA_HEADS * GLA_DV), p.dtype),
        grid=(batch, nt),
        in_specs=[
            pl.BlockSpec((tm, 256), lambda b, i: (row(b, i), P_GLA_Q // 256)),
            pl.BlockSpec((tm, 256), lambda b, i: (row(b, i), P_GLA_K // 256)),
            pl.BlockSpec((tm, 512), lambda b, i: (row(b, i), P_GLA_V // 512)),
            pl.BlockSpec((tm, 512), lambda b, i: (row(b, i), P_GLA_R // 512)),
            pl.BlockSpec((tm, LANES), lambda b, i: (row(b, i), 0)),
            full(wg2), full(bg2), full(nw),
        ],
        out_specs=pl.BlockSpec((tm, GLA_HEADS * GLA_DV), lambda b, i: (row(b, i), 0)),
        scratch_shapes=[pltpu.VMEM((GLA_DV, GLA_HEADS * GLA_DK), F32)],
        compiler_params=_cparams(("parallel", "arbitrary"), 32 << 20),
        name="gla",
    )(p, p, p, p, ps, wg2, bg2, nw)


def _ssd_kernel(z_ref, x_ref, b_ref, c_ref, ps_ref, pst_ref, cw_ref, cb_ref, dtb_row_ref,
                dtb_col_ref, alog_row_ref, alog_col_ref, dskip_ref, nw_ref, exp_ref,
                o_ref, st_ref, tail_ref, *, tm):
    @pl.when(pl.program_id(1) == 0)
    def _():
        st_ref[...] = jnp.zeros_like(st_ref)
        tail_ref[...] = jnp.zeros_like(tail_ref)

    inner = SSD_INNER
    gw = inner // SSD_GROUPS
    xbc = jnp.concatenate([x_ref[...], b_ref[...], c_ref[...]], axis=-1).astype(F32)
    ext = jnp.concatenate([tail_ref[...], xbc], axis=0)
    tail_ref[...] = xbc[tm - SUBLANES:, :]
    conv = cb_ref[...]
    for j in range(SSD_CONV):
        shift = SSD_CONV - 1 - j
        src = ext if shift == 0 else pltpu.roll(ext, shift, axis=0)
        conv = conv + cw_ref[j:j + 1, :] * src[SUBLANES:, :]
    xc = _silu(conv)
    xs = xc[:, :inner]
    bm = xc[:, inner:inner + SSD_GROUPS * SSD_STATE].astype(MXU_DTYPE)
    cm = xc[:, inner + SSD_GROUPS * SSD_STATE:].astype(MXU_DTYPE)

    dt_exp = _dot01_right(_softplus(ps_ref[...] + dtb_row_ref[...]), exp_ref[...])
    a_row = -jnp.exp(alog_row_ref[...])
    dta = dt_exp * a_row
    tril, triu, same = _chunk_masks(tm, SSD_CHUNK)
    acum, aend = _dot01_left_multi((tril, same), dta)
    dta_t = _softplus(pst_ref[...] + dtb_col_ref[...]) * (-jnp.exp(alog_col_ref[...]))
    acum_t = _dot01_right(dta_t, triu)
    x_dt = xs * dt_exp
    xw = (x_dt * jnp.exp(aend - acum)).astype(MXU_DTYPE)
    x_dt_m = x_dt.astype(MXU_DTYPE)
    e_acum = jnp.exp(acum)
    cr = lax.broadcasted_iota(jnp.int32, (SSD_CHUNK, SSD_CHUNK), 0)
    cc = lax.broadcasted_iota(jnp.int32, (SSD_CHUNK, SSD_CHUNK), 1)
    causal = cr >= cc
    dskip = dskip_ref[...]
    nw = nw_ref[...]
    hg = SSD_HEADS // SSD_GROUPS

    for c in range(tm // SSD_CHUNK):
        r0 = c * SSD_CHUNK
        rows = slice(r0, r0 + SSD_CHUNK)
        for g in range(SSD_GROUPS):
            gl = slice(g * gw, (g + 1) * gw)
            nl = slice(g * SSD_STATE, (g + 1) * SSD_STATE)
            b_g, c_g = bm[rows, nl], cm[rows, nl]
            cb = _dot_nt(c_g, b_g)
            parts = []
            for hh in range(hg):
                h = g * hg + hh
                lane0 = h * SSD_HEADDIM
                colv = jnp.broadcast_to(acum[rows, lane0:lane0 + 1], (SSD_CHUNK, SSD_CHUNK))
                rowv = acum_t[h:h + 1, r0:r0 + SSD_CHUNK]
                decay = jnp.where(causal, jnp.exp(colv - rowv), 0.0)
                mmat = (cb * decay).astype(MXU_DTYPE)
                parts.append(_dot(mmat, x_dt_m[rows, lane0:lane0 + SSD_HEADDIM]))
            y_diag = jnp.concatenate(parts, axis=-1)
            state_t = st_ref[:, gl]
            y_off = _dot(c_g, state_t.astype(MXU_DTYPE)) * e_acum[rows, gl]
            st_ref[:, gl] = state_t * jnp.exp(aend[r0:r0 + 1, gl]) + _dot_tn(b_g, xw[rows, gl])
            y = y_diag + y_off + dskip[:, gl] * xs[rows, gl]
            y = y * _silu(z_ref[rows, gl].astype(F32))
            o_ref[rows, gl] = _rmsnorm(y, nw[:, gl]).astype(o_ref.dtype)


def _ssd(p, ps, pst, cw, cb, dtb_row, dtb_col, alog_row, alog_col, dskip, nw, expand, *, batch, seq, tm):
    nt = seq // tm
    full = lambda a: pl.BlockSpec(a.shape, lambda b, i: (0,) * a.ndim)
    row = lambda b, i: b * nt + i
    conv_dim = SSD_INNER + 2 * SSD_GROUPS * SSD_STATE
    return pl.pallas_call(
        functools.partial(_ssd_kernel, tm=tm),
        out_shape=jax.ShapeDtypeStruct((p.shape[0], SSD_INNER), p.dtype),
        grid=(batch, nt),
        in_specs=[
            pl.BlockSpec((tm, 512), lambda b, i: (row(b, i), P_SSD_Z // 512)),
            pl.BlockSpec((tm, 512), lambda b, i: (row(b, i), P_SSD_X // 512)),
            pl.BlockSpec((tm, 256), lambda b, i: (row(b, i), P_SSD_B // 256)),
            pl.BlockSpec((tm, 256), lambda b, i: (row(b, i), P_SSD_C // 256)),
            pl.BlockSpec((tm, LANES), lambda b, i: (row(b, i), 0)),
            pl.BlockSpec((SUBLANES, tm), lambda b, i: (PS_DT // SUBLANES, row(b, i))),
            full(cw), full(cb), full(dtb_row), full(dtb_col), full(alog_row), full(alog_col),
            full(dskip), full(nw), full(expand),
        ],
        out_specs=pl.BlockSpec((tm, SSD_INNER), lambda b, i: (row(b, i), 0)),
        scratch_shapes=[pltpu.VMEM((SSD_STATE, SSD_INNER), F32),
                        pltpu.VMEM((SUBLANES, conv_dim), F32)],
        compiler_params=_cparams(("parallel", "arbitrary"), 40 << 20),
        name="ssd",
    )(p, p, p, p, ps, pst, cw, cb, dtb_row, dtb_col, alog_row, alog_col, dskip, nw, expand)


def _merge_kernel(x_ref, g_ref, y0_ref, y1_ref, y2_ref, y3_ref, wsq_ref, wbr_ref, o_ref, h_ref, acc_ref,
                  *, tc):
    j = pl.program_id(1)
    d = x_ref.shape[-1]

    @pl.when(j == 0)
    def _():
        _norm_rows(x_ref, g_ref, h_ref)
        acc_ref[...] = jnp.zeros_like(acc_ref)

    def branch(y_ref):
        h = h_ref[...]
        y = y_ref[...]
        for c in range(d // tc):
            cols = slice(c * tc, (c + 1) * tc)
            gate = jax.nn.sigmoid(_dot(h, wsq_ref[:, cols]))
            acc_ref[:, cols] += gate * _dot(y, wbr_ref[:, cols])

    for b, y_ref in enumerate((y0_ref, y1_ref, y2_ref, y3_ref)):
        pl.when(j == b)(functools.partial(branch, y_ref))

    @pl.when(j == N_BRANCH)
    def _():
        merged = acc_ref[...].astype(MXU_DTYPE)
        for c in range(d // tc):
            cols = slice(c * tc, (c + 1) * tc)
            o_ref[:, cols] = x_ref[:, cols] + _dot(merged, wsq_ref[:, cols])


def _merge(x, g, ys, wsq, wbr, *, layer, tm, tc):
    m, d = x.shape
    assert len(ys) == N_BRANCH
    bw = ys[0].shape[-1]
    wbytes = jnp.dtype(wsq.dtype).itemsize
    vmem = 4 * tm * d * 4 + tm * d * wbytes + tm * d * 4 + 2 * d * d * wbytes \
        + 2 * bw * d * wbytes + 3 * N_BRANCH * tm * bw * wbytes + 4 * tm * tc * 4 + (4 << 20)
    last = N_BRANCH - 1
    y_spec = pl.BlockSpec((tm, bw), lambda i, j: (i, 0))
    return pl.pallas_call(
        functools.partial(_merge_kernel, tc=tc),
        out_shape=jax.ShapeDtypeStruct((m, d), F32),
        grid=(m // tm, N_BRANCH + 1),
        in_specs=[
            pl.BlockSpec((tm, d), lambda i, j: (i, 0)),
            pl.BlockSpec((1, d), lambda i, j: (0, 0)),
            y_spec, y_spec, y_spec, y_spec,
            pl.BlockSpec((None, d, d), lambda i, j: (layer, 0, j)),
            pl.BlockSpec((None, None, bw, d), lambda i, j: (layer, jnp.minimum(j, last), 0, 0)),
        ],
        out_specs=pl.BlockSpec((tm, d), lambda i, j: (i, 0)),
        scratch_shapes=[pltpu.VMEM((tm, d), wsq.dtype), pltpu.VMEM((tm, d), F32)],
        compiler_params=_cparams(("parallel", "arbitrary"), vmem),
        name="merge",
    )(x, g, *ys, wsq, wbr)


def _xattn_kernel(x_ref, g_ref, wq_ref, k_ref, v_ref, wo_ref, o_ref, h_ref, oc_ref, *, tc):
    _norm_rows(x_ref, g_ref, h_ref)
    h = h_ref[...]
    dh = XATTN_DH
    for hd in range(XATTN_HEADS):
        cols = slice(hd * dh, (hd + 1) * dh)
        q = _dot(h, wq_ref[:, cols]).astype(MXU_DTYPE)
        s = _dot_nt(q, k_ref[:, cols]) * (dh ** -0.5 * LOG2E)
        p = jnp.exp2(s - jnp.max(s, axis=-1, keepdims=True))
        p = p / jnp.sum(p, axis=-1, keepdims=True)
        oc_ref[:, cols] = _dot(p.astype(MXU_DTYPE), v_ref[:, cols]).astype(oc_ref.dtype)
    oc = oc_ref[...]
    d = x_ref.shape[-1]
    for c in range(d // tc):
        cols = slice(c * tc, (c + 1) * tc)
        o_ref[:, cols] = x_ref[:, cols] + _dot(oc, wo_ref[:, cols])


def _xattn(x, g, wq, kv, wo, *, layer, seq, n_mem, tm, tc):
    m, d = x.shape
    wbytes = jnp.dtype(wq.dtype).itemsize
    vmem = 4 * tm * d * 4 + 2 * tm * d * wbytes + 2 * d * d * wbytes + 4 * n_mem * d * wbytes \
        + 3 * tm * XATTN_DH * 4 + 3 * tm * n_mem * 4 + 2 * tm * tc * 4 + (6 << 20)
    tiles_per_seq = seq // tm
    return pl.pallas_call(
        functools.partial(_xattn_kernel, tc=tc),
        out_shape=jax.ShapeDtypeStruct((m, d), F32),
        grid=(m // tm,),
        in_specs=[
            pl.BlockSpec((tm, d), lambda i: (i, 0)),
            pl.BlockSpec((1, d), lambda i: (0, 0)),
            pl.BlockSpec((None, d, d), lambda i: (layer, 0, 0), pipeline_mode=pl.Buffered(1)),
            pl.BlockSpec((n_mem, d), lambda i: (i // tiles_per_seq, 0)),
            pl.BlockSpec((n_mem, d), lambda i: (i // tiles_per_seq, 1)),
            pl.BlockSpec((None, d, d), lambda i: (layer, 0, 0), pipeline_mode=pl.Buffered(1)),
        ],
        out_specs=pl.BlockSpec((tm, d), lambda i: (i, 0)),
        scratch_shapes=[pltpu.VMEM((tm, d), wq.dtype), pltpu.VMEM((tm, d), wq.dtype)],
        compiler_params=_cparams(("parallel",), vmem),
        name="xattn",
    )(x, g, wq, kv, kv, wo)


_IN_SIZES = (N_BRANCH * D_MODEL, MLA_Q_RANK, MLA_KV_RANK, MLA_ROPE, 3 * FOX_HEADS * FOX_DH, FOX_HEADS,
             GLA_HEADS * GLA_DK, GLA_HEADS * GLA_DK, GLA_HEADS * GLA_DV, GLA_GATE_RANK,
             GLA_HEADS * GLA_DV, SSD_INNER, SSD_INNER + 2 * SSD_GROUPS * SSD_STATE, SSD_HEADS)


def _rot_cols(w):
    half = w.shape[-1] // 2
    return jnp.concatenate([-w[..., half:], w[..., :half]], axis=-1)


def _prep_w_in(w_mix):
    w_in = w_mix.astype(MXU_DTYPE)
    nl, d = w_in.shape[:2]
    offs = np.cumsum((0,) + _IN_SIZES[1:])
    assert offs[-1] == w_in.shape[2]
    seg = lambda i: w_in[:, :, offs[i]:offs[i + 1]]
    (cq, ckv, kr, fox, fox_f, gq, gk, gv, glr, gr, sz, sxbc, sdt) = [seg(i) for i in range(len(_IN_SIZES) - 1)]
    z = lambda n: jnp.zeros((nl, d, n), w_in.dtype)
    w_p = jnp.concatenate([
        cq, z(512 - MLA_Q_RANK), ckv, kr, z(LANES - MLA_ROPE), _rot_cols(kr), z(LANES - MLA_ROPE),
        fox, gq, gk, gv, gr, sz, sxbc], axis=2)
    assert w_p.shape[2] == P_WIDTH
    cs = np.ones((1, P_WIDTH), np.float32)
    cs[:, P_FOX_Q:P_FOX_K] = FOX_DH ** -0.5 * LOG2E
    cs[:, P_GLA_Q:P_GLA_K] = GLA_DK ** -0.5
    w_s = jnp.concatenate([glr, sdt, fox_f, z(LANES - GLA_GATE_RANK - SSD_HEADS - FOX_HEADS)], axis=2)
    return w_p.astype(MXU_DTYPE), jnp.asarray(cs), w_s.astype(MXU_DTYPE)


def _prep_mla(w_uq, w_ukv):
    nl, r = w_uq.shape[:2]
    wq = w_uq.reshape(nl, r, MLA_HEADS, MLA_NOPE + MLA_ROPE)
    nope, pe = wq[..., :MLA_NOPE], wq[..., MLA_NOPE:]
    zpad = jnp.zeros((nl, r, MLA_HEADS, MLA_DK_PAD - MLA_NOPE - MLA_ROPE), w_uq.dtype)
    wa = jnp.concatenate([nope, pe, zpad], axis=-1).reshape(nl, r, MLA_HEADS * MLA_DK_PAD)
    wb = jnp.concatenate([_rot_cols(pe), zpad], axis=-1).reshape(nl, r, MLA_HEADS * LANES)
    rk = w_ukv.shape[1]
    wkv = w_ukv.reshape(nl, rk, MLA_HEADS, MLA_NOPE + MLA_V)
    wkv = jnp.concatenate([wkv[..., :MLA_NOPE].reshape(nl, rk, -1), wkv[..., MLA_NOPE:].reshape(nl, rk, -1)],
                          axis=2)
    return wa.astype(MXU_DTYPE), wb.astype(MXU_DTYPE), wkv.astype(MXU_DTYPE)


def _tiles(m, seq):
    mid = min(512, seq)
    return dict(ffn_tm=mid, ffn_tf=512, proj_tm=mid, proj_tn=2816, prep_tm=min(1024, seq),
                flash_tb=mid, flash_rmult=min(2, seq // mid), gla_tm=min(512, seq), ssd_tm=min(256, seq), merge_tm=mid, merge_tc=256,
                xattn_tm=mid, xattn_tc=512, rope_tm=min(1024, m))


@jax.jit
def kernel(x, mem, positions, ffn1_norm, ffn1_w_gate, ffn1_w_up, ffn1_w_down, mix_norm, w_in, mla_q_norm, mla_kv_norm, mla_w_uq, mla_w_ukv, fox_f_bias, gla_w_g2, gla_b_g2, gla_norm, ssd_conv_w, ssd_conv_b, ssd_dt_bias, ssd_a_log, ssd_d, ssd_norm, w_branch, w_out, xattn_norm, mem_norm, xattn_w_q, xattn_w_k, xattn_w_v, xattn_w_o, ffn2_norm, ffn2_w_gate, ffn2_w_up, ffn2_w_down, final_norm):
    batch, seq, d = x.shape
    n_mem = mem.shape[1]
    m = batch * seq
    t = _tiles(m, seq)
    depth = w_in.shape[0]
    cast = lambda a: a.astype(MXU_DTYPE)
    row = lambda a: a.reshape(1, -1).astype(F32)
    col8 = lambda v: jnp.zeros((SUBLANES, 1), F32).at[:v.shape[0], 0].set(v.astype(F32))

    xs = x.reshape(m, d)
    mem2 = mem.reshape(batch * n_mem, d)
    ct, st = _rope_tables(positions.reshape(m, 1), tm=t["rope_tm"])

    expand = np.zeros((LANES, SSD_INNER), np.float32)
    for h in range(SSD_HEADS):
        expand[PS_DT + h, h * SSD_HEADDIM:(h + 1) * SSD_HEADDIM] = 1.0
    expand = jnp.asarray(expand, MXU_DTYPE)
    rep = lambda v: jnp.repeat(v.astype(F32), SSD_HEADDIM).reshape(1, SSD_INNER)
    ffn = functools.partial(_ffn, tm=t["ffn_tm"], tf=t["ffn_tf"])
    flash = functools.partial(_flash, batch=batch, seq=seq, groups=2, tb=t["flash_tb"], rmult=t["flash_rmult"])

    f1g, f1u, f1d = cast(ffn1_w_gate), cast(ffn1_w_up), cast(ffn1_w_down)
    f2g, f2u, f2d = cast(ffn2_w_gate), cast(ffn2_w_up), cast(ffn2_w_down)
    w_p, cs, w_s = _prep_w_in(w_in[:, :, N_BRANCH * d:])
    wa, wb, wkv = _prep_mla(mla_w_uq, mla_w_ukv)
    wsq = jnp.concatenate([cast(w_in[:, :, :N_BRANCH * d]), cast(w_out)], axis=2)
    wbr = cast(w_branch)
    w_kv = cast(jnp.concatenate([xattn_w_k, xattn_w_v], axis=2))
    wq_x, wo_x = cast(xattn_w_q), cast(xattn_w_o)

    for l in range(depth):
        xs = ffn(xs, row(ffn1_norm[l]), f1g, f1u, f1d, row(final_norm), layer=l, final_norm=False)

        p, ps, pst = _proj(xs, row(mix_norm[l]), w_p, cs, w_s, layer=l, tm=t["proj_tm"], tn=t["proj_tn"])

        q_mla, k_mla, v_mla = _mla_prep(p, ct, st, row(mla_q_norm[l]), row(mla_kv_norm[l]), wa, wb, wkv,
                                        layer=l, tm=t["prep_tm"])
        y_mla = flash(q_mla, k_mla, v_mla, None, heads=MLA_HEADS,
                      dk=MLA_DK_PAD, dv=MLA_V, q_off=0, k_off=0, v_off=0)

        c_fox = _fox_cum(pst, col8(fox_f_bias[l]), batch=batch, seq=seq)
        y_fox = flash(p, p, p, c_fox, heads=FOX_HEADS,
                      dk=FOX_DH, dv=FOX_DH, q_off=P_FOX_Q, k_off=P_FOX_K, v_off=P_FOX_V)

        wg2 = jnp.zeros((LANES, GLA_HEADS * GLA_DK), F32).at[PS_GLR:PS_GLR + GLA_GATE_RANK].set(gla_w_g2[l])
        y_gla = _gla(p, ps, cast(wg2), row(gla_b_g2[l]), row(gla_norm[l]), batch=batch, seq=seq, tm=t["gla_tm"])

        dtb_row = jnp.zeros((1, LANES), F32).at[0, PS_DT:PS_DT + SSD_HEADS].set(ssd_dt_bias[l])
        y_ssd = _ssd(p, ps, pst, ssd_conv_w[l].astype(F32), row(ssd_conv_b[l]), dtb_row, col8(ssd_dt_bias[l]),
                     rep(ssd_a_log[l]), col8(ssd_a_log[l]), rep(ssd_d[l]), row(ssd_norm[l]), expand,
                     batch=batch, seq=seq, tm=t["ssd_tm"])

        xs = _merge(xs, row(mix_norm[l]), (y_mla, y_fox, y_gla, y_ssd), wsq, wbr, layer=l, tm=t["merge_tm"], tc=t["merge_tc"])

        kv = _norm_matmul(mem2, row(mem_norm[l]), w_kv, layer=l, tm=min(512, batch * n_mem), tn=1024)
        xs = _xattn(xs, row(xattn_norm[l]), wq_x, kv, wo_x, layer=l,
                    seq=seq, n_mem=n_mem, tm=t["xattn_tm"], tc=t["xattn_tc"])

        xs = ffn(xs, row(ffn2_norm[l]), f2g, f2u, f2d, row(final_norm), layer=l, final_norm=(l == depth - 1))

    return xs.reshape(batch, seq, d)
```
